```python
import math
import jax, jax.numpy as jnp
from jax import lax
import numpy as np

D_MODEL = 1024
BATCH = 2
SEQ = 8192
DEPTH = 2

GRID_W = 64
CTX_LEN = 256

MIX_W = D_MODEL
POOL_W = MIX_W // 2
POOL_GROUPS = 4
POOL_WINDOWS = (2, 4, 8, 16)
POOL_GC = POOL_W // POOL_GROUPS
MLA_HEADS = 8
QK_NOPE = 64
QK_ROPE = 32
V_HEAD = 64
Q_LORA = D_MODEL // 4
KV_LORA = D_MODEL // 4
IN_W = POOL_W + Q_LORA + KV_LORA + QK_ROPE
ROPE_AXIS = QK_ROPE // 2
ROPE_BASE = 10000.0
Q_BLOCK = 128
D_FF = 2816
N_EXPERTS = 8
TOP_K = 2
D_FF_EXPERT = 3584
HY_WIDTH = D_MODEL
HY_SHORT = 3
HY_BANDS = 16
HY_EMB = 1 + 2 * HY_BANDS
HY_FILTER_HIDDEN = 64
HY_FAST_DECAY = 0.3
HY_SLOW_DECAY = 1.5
HY_TARGET = 1e-2
HY_MIN_DECAY = math.log(HY_TARGET) / HY_SLOW_DECAY
HY_MAX_DECAY = math.log(HY_TARGET) / HY_FAST_DECAY
LN_EPS = 1e-5
RMS_EPS = 1e-6

kernel_name = 'hybrid_pool_mla_hyena_moe_diffusion_block'


def layer_norm(x, g, b):
    xf = x.astype(jnp.float32)
    mu = jnp.mean(xf, axis=-1, keepdims=True)
    var = jnp.mean(jnp.square(xf - mu), axis=-1, keepdims=True)
    return ((xf - mu) * lax.rsqrt(var + LN_EPS) * g.astype(jnp.float32) + b.astype(jnp.float32)).astype(x.dtype)


def rms_norm(x, g):
    xf = x.astype(jnp.float32)
    return (xf * lax.rsqrt(jnp.mean(jnp.square(xf), axis=-1, keepdims=True) + RMS_EPS) * g.astype(jnp.float32)).astype(x.dtype)


def ada_chunks(s, w, b, k):
    mod = s @ w[:, :k * D_MODEL] + b[:k * D_MODEL]
    return [m[:, None, :] for m in jnp.split(mod, k, axis=-1)]


def axial_angles(n):
    rows = n // GRID_W
    r = jnp.repeat(jnp.arange(rows, dtype=jnp.float32), GRID_W)
    col = jnp.tile(jnp.arange(GRID_W, dtype=jnp.float32), rows)
    inv = ROPE_BASE ** (-jnp.arange(0, ROPE_AXIS, 2, dtype=jnp.float32) / ROPE_AXIS)
    return r[:, None] * inv, col[:, None] * inv


def rotate_pairs(x, ang):
    half = x.shape[-1] // 2
    x1, x2 = x[..., :half], x[..., half:]
    cos = jnp.cos(ang).astype(x.dtype)
    sin = jnp.sin(ang).astype(x.dtype)
    return jnp.concatenate([x1 * cos - x2 * sin, x2 * cos + x1 * sin], axis=-1)


def axial_rope(x, ang_r, ang_c):
    return jnp.concatenate([rotate_pairs(x[..., :ROPE_AXIS], ang_r), rotate_pairs(x[..., ROPE_AXIS:], ang_c)], axis=-1)


def multiscale_pool(u, pool_w, pool_scale):
    b, n, _ = u.shape
    ug = u.reshape(b, n, POOL_GROUPS, POOL_GC)
    cs = jnp.cumsum(ug.astype(jnp.float32), axis=1)
    cs = jnp.concatenate([jnp.zeros((b, 1, POOL_GROUPS, POOL_GC), jnp.float32), cs], axis=1)
    t = jnp.arange(n)[:, None]
    half = jnp.array([w // 2 for w in POOL_WINDOWS], dtype=jnp.int32)[None, :]
    lo = jnp.clip(t - half, 0, n)
    hi = jnp.clip(t + half, 0, n)
    grp = jnp.arange(POOL_GROUPS)[None, :]
    win_sum = cs[:, hi, grp] - cs[:, lo, grp]
    mean = win_sum / (hi - lo).astype(jnp.float32)[None, :, :, None]
    d = (mean - ug.astype(jnp.float32)).astype(u.dtype)
    y = jnp.einsum('bngc,gcd->bngd', d, pool_w)
    return y.reshape(b, n, POOL_W) * pool_scale


def mla_queries(q_lat, q_norm, q_up, ang):
    b, n, _ = q_lat.shape
    q = (rms_norm(q_lat, q_norm) @ q_up).reshape(b, n, MLA_HEADS, QK_NOPE + QK_ROPE)
    q_nope, q_rope = q[..., :QK_NOPE], q[..., QK_NOPE:]
    if ang is not None:
        q_rope = axial_rope(q_rope, ang[0][:, None, :], ang[1][:, None, :])
    return jnp.concatenate([q_nope, q_rope], axis=-1) * (QK_NOPE + QK_ROPE) ** -0.5


def mla_keys_values(kv_in, kv_norm, kv_up, ang):
    b, n, _ = kv_in.shape
    kv_lat, k_rope = kv_in[..., :KV_LORA], kv_in[..., KV_LORA:]
    kv = (rms_norm(kv_lat, kv_norm) @ kv_up).reshape(b, n, MLA_HEADS, QK_NOPE + V_HEAD)
    k_nope, v = kv[..., :QK_NOPE], kv[..., QK_NOPE:]
    if ang is not None:
        k_rope = axial_rope(k_rope, ang[0], ang[1])
    k = jnp.concatenate([k_nope, jnp.broadcast_to(k_rope[:, :, None, :], (b, n, MLA_HEADS, QK_ROPE))], axis=-1)
    return k, v


def attend(q, k, v):
    s = jnp.einsum('bqhd,bkhd->bhqk', q, k).astype(jnp.float32)
    p = jax.nn.softmax(s, axis=-1).astype(v.dtype)
    return jnp.einsum('bhqk,bkhd->bqhd', p, v)


def blocked_attend(q, k, v):
    b, n, h, d = q.shape
    qb = q.reshape(b, n // Q_BLOCK, Q_BLOCK, h, d).transpose(1, 0, 2, 3, 4)
    o = lax.map(lambda qq: attend(qq, k, v), qb)
    return o.transpose(1, 0, 2, 3, 4).reshape(b, n, h, V_HEAD)


def pool_mla_mixer(u, u_ctx, ctx_out, ang, in_w, pool_w, pool_scale, q_norm, q_up, kv_norm, kv_up, out_w):
    b, n, _ = u.shape
    kv0 = POOL_W + Q_LORA
    proj = u @ in_w
    if ctx_out:
        proj_c = u_ctx @ in_w
        kv_c_in = proj_c[..., kv0:]
    else:
        kv_c_in = u_ctx @ in_w[:, kv0:]
    k_c, v_c = mla_keys_values(kv_c_in, kv_norm, kv_up, None)
    k_l, v_l = mla_keys_values(proj[..., kv0:], kv_norm, kv_up, ang)
    q_l = mla_queries(proj[..., POOL_W:kv0], q_norm, q_up, ang)
    o_l = blocked_attend(q_l, jnp.concatenate([k_c, k_l], axis=1), jnp.concatenate([v_c, v_l], axis=1))
    y_l = jnp.concatenate([multiscale_pool(proj[..., :POOL_W], pool_w, pool_scale), o_l.reshape(b, n, MLA_HEADS * V_HEAD)], axis=-1) @ out_w
    y_c = None
    if ctx_out:
        m = u_ctx.shape[1]
        q_c = mla_queries(proj_c[..., POOL_W:kv0], q_norm, q_up, None)
        o_c = attend(q_c, k_c, v_c)
        y_c = jnp.concatenate([multiscale_pool(proj_c[..., :POOL_W], pool_w, pool_scale), o_c.reshape(b, m, MLA_HEADS * V_HEAD)], axis=-1) @ out_w
    return y_l, y_c


def short_conv(z, w, bias):
    y = lax.conv_general_dilated(z, w[:, None, :], window_strides=(1,), padding=[(HY_SHORT // 2, HY_SHORT // 2)],
                                 dimension_numbers=('NWC', 'WIO', 'NWC'), feature_group_count=z.shape[-1])
    return y + bias


def hyena_filters(n, fw1, fb1, fw2, fb2, fw3, fb3, fout, freq):
    t = jnp.linspace(0.0, 1.0, n, dtype=jnp.float32)[:, None]
    w_ang = (2.0 * math.pi / n) * jnp.arange(n, dtype=jnp.float32)[:, None]
    bands = jnp.linspace(1e-4, HY_BANDS - 1, HY_BANDS, dtype=jnp.float32)[None, :]
    z = jnp.concatenate([t, jnp.cos(bands * w_ang), -jnp.sin(bands * w_ang)], axis=-1)
    freq = freq.astype(jnp.float32)
    h = jnp.sin(freq * (z @ fw1.astype(jnp.float32) + fb1.astype(jnp.float32)))
    h = jnp.sin(freq * (h @ fw2.astype(jnp.float32) + fb2.astype(jnp.float32)))
    h = jnp.sin(freq * (h @ fw3.astype(jnp.float32) + fb3.astype(jnp.float32)))
    h = (h @ fout.astype(jnp.float32)).reshape(n, 2, HY_WIDTH)
    deltas = jnp.abs(jnp.linspace(HY_MIN_DECAY, HY_MAX_DECAY, HY_WIDTH, dtype=jnp.float32))
    h = h * jnp.exp(-t * deltas)[:, None, :]
    return h[:, 0], h[:, 1]


def bidir_long_conv(v, h_f, h_b, skip):
    b, n, ch = v.shape
    vf = v.astype(jnp.float32)
    k = jnp.concatenate([h_f, jnp.zeros((1, ch), jnp.float32), jnp.flip(h_b[1:], axis=0)], axis=0)
    spec = jnp.fft.rfft(vf, n=2 * n, axis=1) * jnp.fft.rfft(k, n=2 * n, axis=0)[None]
    y = jnp.fft.irfft(spec, n=2 * n, axis=1)[:, :n]
    return (y + vf * skip.astype(jnp.float32)).astype(v.dtype)


def hyena_mixer(u, in_w, conv_w, conv_b, fw1, fb1, fw2, fb2, fw3, fb3, fout, freq, skip, out_w):
    n = u.shape[1]
    z = short_conv(u @ in_w, conv_w, conv_b)
    x0, x1, v = jnp.split(z, 3, axis=-1)
    h_f, h_b = hyena_filters(n, fw1, fb1, fw2, fb2, fw3, fb3, fout, freq)
    y = x0 * bidir_long_conv(v * x1, h_f, h_b, skip)
    return y @ out_w


def swiglu(h, wg, wu, wd):
    return (jax.nn.silu(h @ wg) * (h @ wu)) @ wd


def moe_swiglu(h, router_w, wg, wu, wd):
    shp = h.shape
    t = h.reshape(-1, shp[-1])
    logits = (t @ router_w).astype(jnp.float32)
    top_v, top_i = lax.top_k(logits, TOP_K)
    gates = jax.nn.softmax(top_v, axis=-1)
    combine = jnp.sum(jax.nn.one_hot(top_i, N_EXPERTS, dtype=jnp.float32) * gates[..., None], axis=1).astype(h.dtype)
    out = jnp.zeros_like(t)
    for e in range(N_EXPERTS):
        out = out + combine[:, e:e + 1] * swiglu(t, wg[e], wu[e], wd[e])
    return out.reshape(shp)


def setup_inputs(seed: int = 0) -> dict:
    key = jax.random.key(seed)
    keys = jax.random.split(key, 48)
    counter = [0]

    def nrm(shape, scale):
        k = keys[counter[0]]
        counter[0] += 1
        return jax.random.normal(k, shape, jnp.float32) * scale

    D = D_MODEL
    ne = (DEPTH + 1) // 2
    no = DEPTH // 2
    beta = (8.0 * DEPTH) ** -0.25
    fh = HY_FILTER_HIDDEN
    return {
        'x': nrm((BATCH, SEQ, D), 1.0),
        'c': nrm((BATCH, D), 1.0),
        'ctx': nrm((BATCH, CTX_LEN, D), 1.0),
        'c_ctx': nrm((D,), 1.0),
        'ada_w': nrm((DEPTH, D, 6 * D), 0.5 * D ** -0.5),
        'ada_b': nrm((DEPTH, 6 * D), 0.02),
        'ln_g': 1.0 + nrm((DEPTH, 2, D), 0.02),
        'ln_b': nrm((DEPTH, 2, D), 0.02),
        'mix_in_w': nrm((ne, D, IN_W), D ** -0.5),
        'pool_w': nrm((ne, POOL_GROUPS, POOL_GC, POOL_GC), POOL_GC ** -0.5),
        'pool_scale': 1.0 + nrm((ne, POOL_W), 0.02),
        'q_norm': 1.0 + nrm((ne, Q_LORA), 0.02),
        'q_up': nrm((ne, Q_LORA, MLA_HEADS * (QK_NOPE + QK_ROPE)), Q_LORA ** -0.5),
        'kv_norm': 1.0 + nrm((ne, KV_LORA), 0.02),
        'kv_up': nrm((ne, KV_LORA, MLA_HEADS * (QK_NOPE + V_HEAD)), KV_LORA ** -0.5),
        'mix_out_w': nrm((ne, MIX_W, D), beta * MIX_W ** -0.5),
        'ffn_gate': nrm((ne, D, D_FF), D ** -0.5),
        'ffn_up': nrm((ne, D, D_FF), D ** -0.5),
        'ffn_down': nrm((ne, D_FF, D), beta * D_FF ** -0.5),
        'hy_in_w': nrm((no, D, 3 * HY_WIDTH), D ** -0.5),
        'hy_conv_w': nrm((no, HY_SHORT, 3 * HY_WIDTH), HY_SHORT ** -0.5),
        'hy_conv_b': nrm((no, 3 * HY_WIDTH), 0.02),
        'hy_fw1': nrm((no, HY_EMB, fh), HY_EMB ** -0.5),
        'hy_fb1': nrm((no, fh), 0.1),
        'hy_fw2': nrm((no, fh, fh), fh ** -0.5),
        'hy_fb2': nrm((no, fh), 0.1),
        'hy_fw3': nrm((no, fh, fh), fh ** -0.5),
        'hy_fb3': nrm((no, fh), 0.1),
        'hy_fout': nrm((no, fh, 2 * HY_WIDTH), 0.1 * fh ** -0.5),
        'hy_freq': 1.0 + nrm((no, fh), 0.02),
        'hy_skip': nrm((no, HY_WIDTH), 1.0),
        'hy_out_w': nrm((no, HY_WIDTH, D), beta * HY_WIDTH ** -0.5),
        'router_w': nrm((no, D, N_EXPERTS), D ** -0.5),
        'moe_gate': nrm((no, N_EXPERTS, D, D_FF_EXPERT), D ** -0.5),
        'moe_up': nrm((no, N_EXPERTS, D, D_FF_EXPERT), D ** -0.5),
        'moe_down': nrm((no, N_EXPERTS, D_FF_EXPERT, D), beta * D_FF_EXPERT ** -0.5),
    }


def reference(x, c, ctx, c_ctx, ada_w, ada_b, ln_g, ln_b, mix_in_w, pool_w, pool_scale, q_norm, q_up, kv_norm, kv_up,
              mix_out_w, ffn_gate, ffn_up, ffn_down, hy_in_w, hy_conv_w, hy_conv_b, hy_fw1, hy_fb1, hy_fw2, hy_fb2,
              hy_fw3, hy_fb3, hy_fout, hy_freq, hy_skip, hy_out_w, router_w, moe_gate, moe_up, moe_down):
    alpha = (2.0 * DEPTH) ** 0.25
    n = x.shape[1]
    ang = axial_angles(n)
    s_lat = jax.nn.silu(c)
    s_ctx = jax.nn.silu(c_ctx)[None, :]
    last_reader = (DEPTH - 1) - ((DEPTH - 1) % 2)
    h, hc = x, ctx
    for l in range(DEPTH):
        i = l // 2
        even = (l % 2 == 0)
        ctx_live = l <= last_reader
        ctx_out = l < last_reader
        sh1, sc1, g1, sh2, sc2, g2 = ada_chunks(s_lat, ada_w[l], ada_b[l], 6)
        u = h * (1 + sc1) + sh1
        uc, mc = None, None
        if ctx_live:
            mc = ada_chunks(s_ctx, ada_w[l], ada_b[l], 6 if ctx_out else 2)
            uc = hc * (1 + mc[1]) + mc[0]
        if even:
            y, yc = pool_mla_mixer(u, uc, ctx_out, ang, mix_in_w[i], pool_w[i], pool_scale[i], q_norm[i], q_up[i],
                                   kv_norm[i], kv_up[i], mix_out_w[i])
        else:
            hy_args = (hy_in_w[i], hy_conv_w[i], hy_conv_b[i], hy_fw1[i], hy_fb1[i], hy_fw2[i], hy_fb2[i],
                       hy_fw3[i], hy_fb3[i], hy_fout[i], hy_freq[i], hy_skip[i], hy_out_w[i])
            y = hyena_mixer(u, *hy_args)
            yc = hyena_mixer(uc, *hy_args) if ctx_out else None

        def channel_mix(z):
            if even:
                return swiglu(z, ffn_gate[i], ffn_up[i], ffn_down[i])
            return moe_swiglu(z, router_w[i], moe_gate[i], moe_up[i], moe_down[i])

        h = layer_norm(alpha * h + g1 * y, ln_g[l, 0], ln_b[l, 0])
        h = layer_norm(alpha * h + g2 * channel_mix(h * (1 + sc2) + sh2), ln_g[l, 1], ln_b[l, 1])
        if ctx_out:
            hc = layer_norm(alpha * hc + mc[2] * yc, ln_g[l, 0], ln_b[l, 0])
            hc = layer_norm(alpha * hc + mc[5] * channel_mix(hc * (1 + mc[4]) + mc[3]), ln_g[l, 1], ln_b[l, 1])
    return h
```

```python
import functools
import math

import numpy as np
import jax
import jax.numpy as jnp
from jax import lax
from jax.experimental import pallas as pl
from jax.experimental.pallas import tpu as pltpu

F32 = jnp.float32
BF16 = jnp.bfloat16
HIGHEST = lax.Precision.HIGHEST

D_MODEL = 1024
GRID_W = 64
POOL_W = 512
POOL_WINDOWS = (2, 4, 8, 16)
POOL_GC = 128
MLA_HEADS = 8
QK_NOPE = 64
QK_ROPE = 32
V_HEAD = 64
Q_LORA = 256
KV_LORA = 256
ROPE_AXIS = 16
ROPE_BASE = 10000.0
N_EXPERTS = 8
TOP_K = 2
HY_BANDS = 16
HY_EMB = 1 + 2 * HY_BANDS
HY_FAST_DECAY = 0.3
HY_SLOW_DECAY = 1.5
HY_TARGET = 1e-2
HY_MIN_DECAY = math.log(HY_TARGET) / HY_SLOW_DECAY
HY_MAX_DECAY = math.log(HY_TARGET) / HY_FAST_DECAY
LN_EPS = 1e-5
RMS_EPS = 1e-6
DEPTH = 2
ALPHA = (2.0 * DEPTH) ** 0.25

LANE = 128
HEAD_SLOT = 128
DFT_N = 128
DFT_HALF = DFT_N // 2 + 1
VMEM_LIMIT = 56 * 1024 * 1024


def _cparams(*sem):
    return pltpu.CompilerParams(dimension_semantics=sem, vmem_limit_bytes=VMEM_LIMIT)


def _layer_norm(v, g, b):
    mu = jnp.mean(v, axis=-1, keepdims=True)
    c = v - mu
    var = jnp.mean(c * c, axis=-1, keepdims=True)
    return c * lax.rsqrt(var + LN_EPS) * g + b


def _rms_norm(v, g):
    return v * lax.rsqrt(jnp.mean(v * v, axis=-1, keepdims=True) + RMS_EPS) * g


def _ada_kernel(s_ref, w_ref, b_ref, o_ref):
    s = s_ref[...]
    s = s * jax.nn.sigmoid(s)
    o_ref[0] = jnp.dot(s, w_ref[0], preferred_element_type=F32, precision=HIGHEST) + b_ref[0]


def _ada(s_rows, ada_w, ada_b):
    depth, d, n6 = ada_w.shape
    tn = 768
    return pl.pallas_call(
        _ada_kernel,
        out_shape=jax.ShapeDtypeStruct((depth, 8, n6), F32),
        grid=(depth, n6 // tn),
        in_specs=[
            pl.BlockSpec((8, d), lambda l, j: (0, 0)),
            pl.BlockSpec((1, d, tn), lambda l, j: (l, 0, j)),
            pl.BlockSpec((1, 1, tn), lambda l, j: (l, 0, j)),
        ],
        out_specs=pl.BlockSpec((1, 8, tn), lambda l, j: (l, 0, j)),
        compiler_params=_cparams("parallel", "parallel"),
        name="ada_mod",
    )(s_rows, ada_w, ada_b.reshape(depth, 1, n6))


def _l0a_kernel(x_ref, c_ref, sc_ref, sh_ref, scc_ref, shc_ref, win_ref, qn_ref, kvn_ref,
                wqa_ref, wqb_ref, wk_ref, wv_ref, cos_ref, sin_ref,
                pu_ref, q_ref, k_ref, v_ref, u_scr, *, n_lat_tiles):
    i = pl.program_id(1)

    @pl.when(i < n_lat_tiles)
    def _():
        u_scr[...] = (x_ref[0] * (1.0 + sc_ref[0]) + sh_ref[0]).astype(BF16)

    @pl.when(i >= n_lat_tiles)
    def _():
        u_scr[...] = (c_ref[0] * (1.0 + scc_ref[0]) + shc_ref[0]).astype(BF16)

    proj = jnp.dot(u_scr[...], win_ref[...], preferred_element_type=F32)
    pu_ref[0] = proj[:, :POOL_W]
    cos = cos_ref[...]
    sin = sin_ref[...]
    q0 = POOL_W
    kv0 = POOL_W + Q_LORA
    r0 = kv0 + KV_LORA
    qn = _rms_norm(proj[:, q0:kv0], qn_ref[...]).astype(BF16)
    kvn = _rms_norm(proj[:, kv0:r0], kvn_ref[...]).astype(BF16)
    qa = jnp.dot(qn, wqa_ref[...], preferred_element_type=F32)
    qb = jnp.dot(qn, wqb_ref[...], preferred_element_type=F32)
    kn = jnp.dot(kvn, wk_ref[...], preferred_element_type=F32)
    vv = jnp.dot(kvn, wv_ref[...], preferred_element_type=F32)
    krot = proj[:, r0:r0 + HEAD_SLOT] * cos + proj[:, r0 + HEAD_SLOT:r0 + 2 * HEAD_SLOT] * sin
    for h in range(MLA_HEADS):
        sl = slice(h * HEAD_SLOT, (h + 1) * HEAD_SLOT)
        q_ref[0, h] = (qa[:, sl] * cos + qb[:, sl] * sin).astype(BF16)
        k_ref[0, h] = (kn[:, sl] + krot).astype(BF16)
    for p in range(MLA_HEADS // 2):
        v_ref[0, p] = vv[:, p * LANE:(p + 1) * LANE].astype(BF16)


def _rope_swap_index():
    half = ROPE_AXIS // 2
    idx = []
    for a in range(2):
        base = a * ROPE_AXIS
        idx += list(range(base + half, base + ROPE_AXIS)) + list(range(base, base + half))
    return np.array(idx)


def _rope_tables(n, n_ctx):
    rows = n // GRID_W
    r = jnp.repeat(jnp.arange(rows, dtype=F32), GRID_W)
    col = jnp.tile(jnp.arange(GRID_W, dtype=F32), rows)
    inv = ROPE_BASE ** (-jnp.arange(0, ROPE_AXIS, 2, dtype=F32) / ROPE_AXIS)
    ang_r = r[:, None] * inv
    ang_c = col[:, None] * inv
    cos32 = jnp.concatenate([jnp.cos(ang_r), jnp.cos(ang_r), jnp.cos(ang_c), jnp.cos(ang_c)], axis=-1)
    sin32 = jnp.concatenate([-jnp.sin(ang_r), jnp.sin(ang_r), -jnp.sin(ang_c), jnp.sin(ang_c)], axis=-1)
    pad = HEAD_SLOT - QK_NOPE - QK_ROPE
    cos = jnp.concatenate([jnp.ones((n, QK_NOPE), F32), cos32, jnp.ones((n, pad), F32)], axis=-1)
    sin = jnp.concatenate([jnp.zeros((n, QK_NOPE), F32), sin32, jnp.zeros((n, pad), F32)], axis=-1)
    cos = jnp.concatenate([cos, jnp.ones((n_ctx, HEAD_SLOT), F32)], axis=0)
    sin = jnp.concatenate([sin, jnp.zeros((n_ctx, HEAD_SLOT), F32)], axis=0)
    return cos, sin


def _l0a_weights(in_w, q_up, kv_up):
    swap = _rope_swap_index()
    d = in_w.shape[0]
    r0 = POOL_W + Q_LORA + KV_LORA
    w_rope = in_w[:, r0:]
    pad_l = jnp.zeros((d, QK_NOPE), F32)
    pad_r = jnp.zeros((d, HEAD_SLOT - QK_NOPE - QK_ROPE), F32)
    kr_a = jnp.concatenate([pad_l, w_rope, pad_r], axis=1)
    kr_b = jnp.concatenate([pad_l, w_rope[:, swap], pad_r], axis=1)
    w_in = jnp.concatenate([in_w[:, :r0], kr_a, kr_b], axis=1).astype(BF16)

    scale = (QK_NOPE + QK_ROPE) ** -0.5
    qu = q_up.reshape(Q_LORA, MLA_HEADS, QK_NOPE + QK_ROPE) * scale
    zpad = jnp.zeros((Q_LORA, MLA_HEADS, HEAD_SLOT - QK_NOPE - QK_ROPE), F32)
    wq_a = jnp.concatenate([qu, zpad], axis=-1).reshape(Q_LORA, MLA_HEADS * HEAD_SLOT).astype(BF16)
    wq_b = jnp.concatenate([jnp.zeros((Q_LORA, MLA_HEADS, QK_NOPE), F32), qu[..., QK_NOPE:][..., swap], zpad],
                           axis=-1).reshape(Q_LORA, MLA_HEADS * HEAD_SLOT).astype(BF16)
    kvu = kv_up.reshape(KV_LORA, MLA_HEADS, QK_NOPE + V_HEAD)
    wk = jnp.concatenate([kvu[..., :QK_NOPE], jnp.zeros((KV_LORA, MLA_HEADS, HEAD_SLOT - QK_NOPE), F32)],
                         axis=-1).reshape(KV_LORA, MLA_HEADS * HEAD_SLOT).astype(BF16)
    wv = kvu[..., QK_NOPE:].reshape(KV_LORA, MLA_HEADS * V_HEAD).astype(BF16)
    return w_in, wq_a, wq_b, wk, wv


def _l0a(x, ctx, sc, sh, scc, shc, in_w, q_norm, q_up, kv_norm, kv_up):
    b, n, d = x.shape
    n_ctx = ctx.shape[1]
    tm = 256
    nl = n // tm
    nt = (n + n_ctx) // tm
    w_in, wq_a, wq_b, wk, wv = _l0a_weights(in_w, q_up, kv_up)
    cos, sin = _rope_tables(n, n_ctx)
    hw = MLA_HEADS * HEAD_SLOT
    full = lambda shape: pl.BlockSpec(shape, lambda bi, i: (0,) * len(shape))
    vec = pl.BlockSpec((1, 1, d), lambda bi, i: (bi, 0, 0))
    return pl.pallas_call(
        functools.partial(_l0a_kernel, n_lat_tiles=nl),
        out_shape=[
            jax.ShapeDtypeStruct((b, n + n_ctx, POOL_W), F32),
            jax.ShapeDtypeStruct((b, MLA_HEADS, n + n_ctx, HEAD_SLOT), BF16),
            jax.ShapeDtypeStruct((b, MLA_HEADS, n + n_ctx, HEAD_SLOT), BF16),
            jax.ShapeDtypeStruct((b, MLA_HEADS // 2, n + n_ctx, LANE), BF16),
        ],
        grid=(b, nt),
        in_specs=[
            pl.BlockSpec((1, tm, d), lambda bi, i: (bi, jnp.minimum(i, nl - 1), 0)),
            pl.BlockSpec((1, tm, d), lambda bi, i: (bi, jnp.maximum(i - nl, 0), 0)),
            vec, vec, full((1, 1, d)), full((1, 1, d)),
            full(w_in.shape), full((1, Q_LORA)), full((1, KV_LORA)),
            full((Q_LORA, hw)), full((Q_LORA, hw)), full((KV_LORA, hw)), full((KV_LORA, MLA_HEADS * V_HEAD)),
            pl.BlockSpec((tm, HEAD_SLOT), lambda bi, i: (i, 0)),
            pl.BlockSpec((tm, HEAD_SLOT), lambda bi, i: (i, 0)),
        ],
        out_specs=[
            pl.BlockSpec((1, tm, POOL_W), lambda bi, i: (bi, i, 0)),
            pl.BlockSpec((1, MLA_HEADS, tm, HEAD_SLOT), lambda bi, i: (bi, 0, i, 0)),
            pl.BlockSpec((1, MLA_HEADS, tm, HEAD_SLOT), lambda bi, i: (bi, 0, i, 0)),
            pl.BlockSpec((1, MLA_HEADS // 2, tm, LANE), lambda bi, i: (bi, 0, i, 0)),
        ],
        scratch_shapes=[pltpu.VMEM((tm, d), BF16)],
        compiler_params=_cparams("parallel", "arbitrary"),
        name="l0_in_proj",
    )(x, ctx, sc, sh, scc, shc, w_in, q_norm.reshape(1, -1), kv_norm.reshape(1, -1),
      wq_a, wq_b, wk, wv, cos, sin)


def _attn_kernel(q_ref, k_ref, v_ref, o_ref, m_ref, l_ref, acc_ref, *, nk):
    ki = pl.program_id(2)

    @pl.when(ki == 0)
    def _():
        m_ref[...] = jnp.full(m_ref.shape, -jnp.inf, F32)
        l_ref[...] = jnp.zeros(l_ref.shape, F32)
        acc_ref[...] = jnp.zeros(acc_ref.shape, F32)

    for h in range(MLA_HEADS):
        s = lax.dot_general(q_ref[0, h], k_ref[0, h], (((1,), (1,)), ((), ())),
                            preferred_element_type=F32)
        m_prev = m_ref[h]
        m_new = jnp.maximum(m_prev, jnp.max(s, axis=-1, keepdims=True))
        a = jnp.exp(m_prev - m_new)
        p = jnp.exp(s - m_new)
        l_ref[h] = a * l_ref[h] + jnp.sum(p, axis=-1, keepdims=True)
        pv = jnp.dot(p.astype(BF16), v_ref[0, h // 2], preferred_element_type=F32)
        acc_ref[h] = a * acc_ref[h] + pv
        m_ref[h] = m_new

    @pl.when(ki == nk - 1)
    def _():
        lane = lax.broadcasted_iota(jnp.int32, acc_ref.shape[1:], 1)
        for p in range(MLA_HEADS // 2):
            lo = acc_ref[2 * p] / l_ref[2 * p]
            hi = acc_ref[2 * p + 1] / l_ref[2 * p + 1]
            o_ref[0, :, p * LANE:(p + 1) * LANE] = jnp.where(lane < V_HEAD, lo, hi).astype(o_ref.dtype)


def _attention(q, k, v, n):
    b, h, n_all, _ = k.shape
    tq, tk = 512, 768
    nk = n_all // tk
    return pl.pallas_call(
        functools.partial(_attn_kernel, nk=nk),
        out_shape=jax.ShapeDtypeStruct((b, n, h * V_HEAD), BF16),
        grid=(b, n // tq, nk),
        in_specs=[
            pl.BlockSpec((1, h, tq, HEAD_SLOT), lambda bi, qi, ki: (bi, 0, qi, 0)),
            pl.BlockSpec((1, h, tk, HEAD_SLOT), lambda bi, qi, ki: (bi, 0, ki, 0)),
            pl.BlockSpec((1, h // 2, tk, LANE), lambda bi, qi, ki: (bi, 0, ki, 0)),
        ],
        out_specs=pl.BlockSpec((1, tq, h * V_HEAD), lambda bi, qi, ki: (bi, qi, 0)),
        scratch_shapes=[
            pltpu.VMEM((h, tq, 1), F32),
            pltpu.VMEM((h, tq, 1), F32),
            pltpu.VMEM((h, tq, LANE), F32),
        ],
        compiler_params=_cparams("parallel", "parallel", "arbitrary"),
        name="mla_attention",
    )(q, k, v)


def _l0b_kernel(pp_ref, pc_ref, pn_ref, o_ref, x_ref, g1_ref, sc2_ref, sh2_ref, pw_ref, ps_ref, ow_ref,
                lg_ref, lb_ref, h_ref, z_ref, *, tm, n):
    i = pl.program_id(1)
    halo = 8
    ext = jnp.concatenate([pp_ref[0], pc_ref[0], pn_ref[0]], axis=0)
    pos = i * tm - halo + lax.broadcasted_iota(jnp.int32, (tm + 2 * halo, 1), 0)
    ext = jnp.where((pos >= 0) & (pos < n), ext, 0.0)
    t = i * tm + lax.broadcasted_iota(jnp.int32, (tm, 1), 0)
    y = jnp.dot(o_ref[0], ow_ref[POOL_W:, :], preferred_element_type=F32)
    for g, w in enumerate(POOL_WINDOWS):
        hw = w // 2
        e = ext[:, g * POOL_GC:(g + 1) * POOL_GC]
        s = e
        width = 1
        while width < w:
            s = s[:s.shape[0] - width] + s[width:]
            width *= 2
        win = s[halo - hw:halo - hw + tm]
        cnt = (jnp.minimum(t + hw, n) - jnp.maximum(t - hw, 0)).astype(F32)
        dd = (win / cnt - e[halo:halo + tm]).astype(BF16)
        yg = jnp.dot(dd, pw_ref[g], preferred_element_type=F32) * ps_ref[:, g * POOL_GC:(g + 1) * POOL_GC]
        y = y + jnp.dot(yg.astype(BF16), ow_ref[g * POOL_GC:(g + 1) * POOL_GC, :], preferred_element_type=F32)
    hh = _layer_norm(ALPHA * x_ref[0] + g1_ref[0] * y, lg_ref[...], lb_ref[...])
    h_ref[0] = hh
    z_ref[0] = (hh * (1.0 + sc2_ref[0]) + sh2_ref[0]).astype(BF16)


def _l0b(pool_u, attn_o, x, g1, sc2, sh2, pool_w, pool_scale, out_w, ln_g, ln_b):
    b, n, d = x.shape
    tm = 512
    hb = tm // 8
    vec = pl.BlockSpec((1, 1, d), lambda bi, i: (bi, 0, 0))
    full = lambda shape: pl.BlockSpec(shape, lambda bi, i: (0,) * len(shape))
    return pl.pallas_call(
        functools.partial(_l0b_kernel, tm=tm, n=n),
        out_shape=[jax.ShapeDtypeStruct((b, n, d), F32), jax.ShapeDtypeStruct((b, n, d), BF16)],
        grid=(b, n // tm),
        in_specs=[
            pl.BlockSpec((1, 8, POOL_W), lambda bi, i: (bi, jnp.maximum(i * hb - 1, 0), 0)),
            pl.BlockSpec((1, tm, POOL_W), lambda bi, i: (bi, i, 0)),
            pl.BlockSpec((1, 8, POOL_W), lambda bi, i: (bi, (i + 1) * hb, 0)),
            pl.BlockSpec((1, tm, POOL_W), lambda bi, i: (bi, i, 0)),
            pl.BlockSpec((1, tm, d), lambda bi, i: (bi, i, 0)),
            vec, vec, vec,
            full(pool_w.shape), full((1, POOL_W)), full(out_w.shape), full((1, d)), full((1, d)),
        ],
        out_specs=[pl.BlockSpec((1, tm, d), lambda bi, i: (bi, i, 0)),
                   pl.BlockSpec((1, tm, d), lambda bi, i: (bi, i, 0))],
        compiler_params=_cparams("parallel", "parallel"),
        name="l0_pool_out_ln",
    )(pool_u, pool_u, pool_u, attn_o, x, g1, sc2, sh2, pool_w.astype(BF16), pool_scale.reshape(1, -1),
      out_w.astype(BF16), ln_g.reshape(1, -1), ln_b.reshape(1, -1))


def _ffn_kernel(te_ref, tv_ref, z_ref, wg_ref, wu_ref, wd_ref, gate_ref, o_ref, acc_ref, *, nj):
    i = pl.program_id(0)
    j = pl.program_id(1)

    @pl.when(j == 0)
    def _():
        acc_ref[...] = jnp.zeros(acc_ref.shape, F32)

    @pl.when(tv_ref[i] > 0)
    def _():
        z = z_ref[...]
        g = jnp.dot(z, wg_ref[0], preferred_element_type=F32)
        u = jnp.dot(z, wu_ref[0], preferred_element_type=F32)
        a = (g * jax.nn.sigmoid(g) * u).astype(BF16)
        acc_ref[...] += jnp.dot(a, wd_ref[0], preferred_element_type=F32)

    @pl.when(j == nj - 1)
    def _():
        o_ref[...] = acc_ref[...] * gate_ref[...]


def _ffn_grouped(z, wg, wu, wd, gate, tile_expert, tile_valid, tm, tf):
    p, d = z.shape
    f = wg.shape[-1]
    nj = f // tf
    return pl.pallas_call(
        functools.partial(_ffn_kernel, nj=nj),
        out_shape=jax.ShapeDtypeStruct((p, d), F32),
        grid_spec=pltpu.PrefetchScalarGridSpec(
            num_scalar_prefetch=2,
            grid=(p // tm, nj),
            in_specs=[
                pl.BlockSpec((tm, d), lambda i, j, te, tv: (i, 0)),
                pl.BlockSpec((1, d, tf), lambda i, j, te, tv: (te[i], 0, j)),
                pl.BlockSpec((1, d, tf), lambda i, j, te, tv: (te[i], 0, j)),
                pl.BlockSpec((1, tf, d), lambda i, j, te, tv: (te[i], j, 0)),
                pl.BlockSpec((tm, 1), lambda i, j, te, tv: (i, 0)),
            ],
            out_specs=pl.BlockSpec((tm, d), lambda i, j, te, tv: (i, 0)),
            scratch_shapes=[pltpu.VMEM((tm, d), F32)],
        ),
        compiler_params=_cparams("parallel", "arbitrary"),
        name="swiglu_grouped",
    )(tile_expert, tile_valid, z, wg, wu, wd, gate)


def _resid_ln_kernel(*refs, n_y, with_mod):
    h_ref = refs[0]
    y_refs = refs[1:1 + n_y]
    g_ref, lg_ref, lb_ref = refs[1 + n_y:4 + n_y]
    rest = refs[4 + n_y:]
    y = y_refs[0][0]
    for r in y_refs[1:]:
        y = y + r[0]
    hh = _layer_norm(ALPHA * h_ref[0] + g_ref[0] * y, lg_ref[...], lb_ref[...])
    if with_mod:
        sc_ref, sh_ref, ho_ref, u_ref = rest
        ho_ref[0] = hh
        u_ref[0] = (hh * (1.0 + sc_ref[0]) + sh_ref[0]).astype(BF16)
    else:
        rest[0][0] = hh


def _resid_ln(h, ys, g, ln_g, ln_b, mod=None):
    b, n, d = h.shape
    tm = 1024
    blk = pl.BlockSpec((1, tm, d), lambda bi, i: (bi, i, 0))
    vec = pl.BlockSpec((1, 1, d), lambda bi, i: (bi, 0, 0))
    row = pl.BlockSpec((1, d), lambda bi, i: (0, 0))
    in_specs = [blk] + [blk] * len(ys) + [vec, row, row]
    args = [h, *ys, g, ln_g.reshape(1, -1), ln_b.reshape(1, -1)]
    out_shape = [jax.ShapeDtypeStruct((b, n, d), F32)]
    out_specs = [blk]
    if mod is not None:
        in_specs += [vec, vec]
        args += list(mod)
        out_shape.append(jax.ShapeDtypeStruct((b, n, d), BF16))
        out_specs.append(blk)
    return pl.pallas_call(
        functools.partial(_resid_ln_kernel, n_y=len(ys), with_mod=mod is not None),
        out_shape=out_shape,
        grid=(b, n // tm),
        in_specs=in_specs,
        out_specs=out_specs,
        compiler_params=_cparams("parallel", "parallel"),
        name="resid_ln",
    )(*args)


def _l1a_kernel(up_ref, uc_ref, un_ref, w_ref, cw_ref, cb_ref, x0_ref, vx_ref, *, tm, n):
    i = pl.program_id(1)
    halo = 16
    ext = jnp.concatenate([up_ref[0], uc_ref[0], un_ref[0]], axis=0)
    pos = i * tm - halo + lax.broadcasted_iota(jnp.int32, (tm + 2 * halo, 1), 0)
    ext = jnp.where((pos >= 0) & (pos < n), ext, jnp.zeros_like(ext))
    z = jnp.dot(ext, w_ref[...], preferred_element_type=F32)
    cw = cw_ref[...]
    zc = (cw[0:1] * z[halo - 1:halo - 1 + tm] + cw[1:2] * z[halo:halo + tm]
          + cw[2:3] * z[halo + 1:halo + 1 + tm] + cb_ref[...])
    c = zc.shape[1] // 3
    x0_ref[0] = zc[:, :c]
    vx_ref[0] = zc[:, 2 * c:] * zc[:, c:2 * c]


def _l1a(u, hy_in_w, conv_w, conv_b):
    b, n, d = u.shape
    c3 = hy_in_w.shape[1]
    c = c3 // 3
    tm = 512
    hb = tm // 16
    nb16 = n // 16
    full = lambda shape: pl.BlockSpec(shape, lambda bi, i: (0,) * len(shape))
    return pl.pallas_call(
        functools.partial(_l1a_kernel, tm=tm, n=n),
        out_shape=[jax.ShapeDtypeStruct((b, n, c), F32), jax.ShapeDtypeStruct((b, n, c), F32)],
        grid=(b, n // tm),
        in_specs=[
            pl.BlockSpec((1, 16, d), lambda bi, i: (bi, jnp.maximum(i * hb - 1, 0), 0)),
            pl.BlockSpec((1, tm, d), lambda bi, i: (bi, i, 0)),
            pl.BlockSpec((1, 16, d), lambda bi, i: (bi, jnp.minimum((i + 1) * hb, nb16 - 1), 0)),
            full((d, c3)), full((3, c3)), full((1, c3)),
        ],
        out_specs=[pl.BlockSpec((1, tm, c), lambda bi, i: (bi, i, 0)),
                   pl.BlockSpec((1, tm, c), lambda bi, i: (bi, i, 0))],
        compiler_params=_cparams("parallel", "parallel"),
        name="hyena_in_conv",
    )(u, u, u, hy_in_w.astype(BF16), conv_w, conv_b.reshape(1, -1))


def _filter_kernel(z_ref, w1_ref, b1_ref, w2_ref, b2_ref, w3_ref, b3_ref, wo_ref, fr_ref, dl_ref, hf_ref, hb_ref):
    z = z_ref[...]
    fr = fr_ref[...]
    dot = functools.partial(jnp.dot, preferred_element_type=F32, precision=HIGHEST)
    h = jnp.sin(fr * (dot(z, w1_ref[...]) + b1_ref[...]))
    h = jnp.sin(fr * (dot(h, w2_ref[...]) + b2_ref[...]))
    h = jnp.sin(fr * (dot(h, w3_ref[...]) + b3_ref[...]))
    o = dot(h, wo_ref[...])
    c = hf_ref.shape[1]
    decay = jnp.exp(-z[:, 0:1] * dl_ref[...])
    hf_ref[...] = o[:, :c] * decay
    hb_ref[...] = o[:, c:] * decay


def _hyena_filters(n, fw1, fb1, fw2, fb2, fw3, fb3, fout, freq):
    c = fout.shape[1] // 2
    fh = fw2.shape[0]
    t = jnp.linspace(0.0, 1.0, n, dtype=F32)[:, None]
    w_ang = (2.0 * math.pi / n) * jnp.arange(n, dtype=F32)[:, None]
    bands = jnp.linspace(1e-4, HY_BANDS - 1, HY_BANDS, dtype=F32)[None, :]
    z = jnp.concatenate([t, jnp.cos(bands * w_ang), -jnp.sin(bands * w_ang),
                         jnp.zeros((n, LANE - HY_EMB), F32)], axis=-1)
    deltas = jnp.abs(jnp.linspace(HY_MIN_DECAY, HY_MAX_DECAY, c, dtype=F32))[None, :]
    padc = lambda a: jnp.pad(a, ((0, 0), (0, LANE - a.shape[1])))
    padr = lambda a: jnp.pad(a, ((0, LANE - a.shape[0]), (0, 0)))
    row = lambda a: padc(a.reshape(1, -1))
    tm = 1024
    full = lambda shape: pl.BlockSpec(shape, lambda i: (0,) * len(shape))
    return pl.pallas_call(
        _filter_kernel,
        out_shape=[jax.ShapeDtypeStruct((n, c), F32), jax.ShapeDtypeStruct((n, c), F32)],
        grid=(n // tm,),
        in_specs=[pl.BlockSpec((tm, LANE), lambda i: (i, 0)),
                  full((LANE, LANE)), full((1, LANE)), full((LANE, LANE)), full((1, LANE)),
                  full((LANE, LANE)), full((1, LANE)), full((LANE, 2 * c)), full((1, LANE)), full((1, c))],
        out_specs=[pl.BlockSpec((tm, c), lambda i: (i, 0)), pl.BlockSpec((tm, c), lambda i: (i, 0))],
        compiler_params=_cparams("parallel"),
        name="hyena_filters",
    )(z, padc(padr(fw1)), row(fb1), padc(padr(fw2)), row(fb2), padc(padr(fw3)), row(fb3),
      padr(fout), row(freq), deltas)


def _dft_consts():
    k = np.arange(DFT_HALF)[:, None]
    n1 = np.arange(DFT_N)[None, :]
    th = 2.0 * np.pi * k * n1 / DFT_N
    fa_r, fa_i = np.cos(th), -np.sin(th)
    kk = np.arange(DFT_N)[:, None] * np.arange(DFT_N)[None, :]
    c, s = np.cos(2.0 * np.pi * kk / DFT_N), np.sin(2.0 * np.pi * kk / DFT_N)
    g_fwd = np.block([[c, s], [-s, c]])
    g_inv = np.block([[c, -s], [s, c]])
    ph = 2.0 * np.pi * np.arange(DFT_HALF)[:, None] * np.arange(DFT_N)[None, :] / (DFT_N * DFT_N)
    tw_c, tw_s = np.cos(ph)[:, :, None], np.sin(ph)[:, :, None]
    n_out = np.arange(DFT_N // 2)[:, None]
    ps = 2.0 * np.pi * n_out * np.arange(DFT_HALF)[None, :] / DFT_N
    wgt = np.full((1, DFT_HALF), 2.0)
    wgt[0, 0] = wgt[0, -1] = 1.0
    length = DFT_N * DFT_N
    fo_r = wgt * np.cos(ps) / length
    fo_i = -wgt * np.sin(ps) / length
    return dict(fa_r=fa_r, fa_i=fa_i, g_fwd=g_fwd, g_inv=g_inv, tw_c=tw_c, tw_s=tw_s, fo_r=fo_r, fo_i=fo_i)


def _dft_rows_kernel(fr_ref, fi_ref, x_ref, or_ref, oi_ref):
    x = x_ref[0].astype(BF16)
    or_ref[0] = jnp.dot(fr_ref[...], x, preferred_element_type=F32).astype(or_ref.dtype)
    oi_ref[0] = jnp.dot(fi_ref[...], x, preferred_element_type=F32).astype(oi_ref.dtype)


def _dft_rows(fr, fi, x, out_dtype):
    b, k, w = x.shape
    r = fr.shape[0]
    tn = 4096
    return pl.pallas_call(
        _dft_rows_kernel,
        out_shape=[jax.ShapeDtypeStruct((b, r, w), out_dtype)] * 2,
        grid=(b, w // tn),
        in_specs=[pl.BlockSpec((r, k), lambda bi, j: (0, 0)), pl.BlockSpec((r, k), lambda bi, j: (0, 0)),
                  pl.BlockSpec((1, k, tn), lambda bi, j: (bi, 0, j))],
        out_specs=[pl.BlockSpec((1, r, tn), lambda bi, j: (bi, 0, j))] * 2,
        compiler_params=_cparams("parallel", "parallel"),
        name="dft_stage_a",
    )(fr, fi, x)


def _spectrum_kernel(ar_ref, ai_ref, tc_ref, ts_ref, g_ref, hr_ref, hi_ref):
    tc = tc_ref[0]
    ts = ts_ref[0]
    ar = ar_ref[0, 0]
    ai = ai_ref[0, 0]
    a2 = jnp.concatenate([ar * tc + ai * ts, ai * tc - ar * ts], axis=0).astype(BF16)
    x = jnp.dot(g_ref[...], a2, preferred_element_type=F32)
    hr_ref[0] = x[:DFT_N]
    hi_ref[0] = x[DFT_N:]


def _filter_spectrum(kr, ki, consts):
    c = kr.shape[-1]
    plane = pl.BlockSpec((1, 1, DFT_N, c), lambda k1: (0, k1, 0, 0))
    tw = pl.BlockSpec((1, DFT_N, 1), lambda k1: (k1, 0, 0))
    out = pl.BlockSpec((1, DFT_N, c), lambda k1: (k1, 0, 0))
    return pl.pallas_call(
        _spectrum_kernel,
        out_shape=[jax.ShapeDtypeStruct((DFT_HALF, DFT_N, c), F32)] * 2,
        grid=(DFT_HALF,),
        in_specs=[plane, plane, tw, tw, pl.BlockSpec((2 * DFT_N, 2 * DFT_N), lambda k1: (0, 0))],
        out_specs=[out, out],
        compiler_params=_cparams("parallel"),
        name="filter_spectrum",
    )(kr, ki, consts["tw_c"], consts["tw_s"], consts["g_fwd"])


def _fft_mid_kernel(ar_ref, ai_ref, hr_ref, hi_ref, tc_ref, ts_ref, gf_ref, gi_ref, br_ref, bi_ref):
    tc = tc_ref[0]
    ts = ts_ref[0]
    hr = hr_ref[0]
    hi = hi_ref[0]
    for b in range(ar_ref.shape[0]):
        ar = ar_ref[b, 0].astype(F32)
        ai = ai_ref[b, 0].astype(F32)
        a2 = jnp.concatenate([ar * tc + ai * ts, ai * tc - ar * ts], axis=0).astype(BF16)
        x = jnp.dot(gf_ref[...], a2, preferred_element_type=F32)
        xr, xi = x[:DFT_N], x[DFT_N:]
        z = jnp.concatenate([xr * hr - xi * hi, xr * hi + xi * hr], axis=0).astype(BF16)
        y = jnp.dot(gi_ref[...], z, preferred_element_type=F32)
        yr, yi = y[:DFT_N], y[DFT_N:]
        br_ref[b, 0] = (yr * tc - yi * ts).astype(br_ref.dtype)
        bi_ref[b, 0] = (yr * ts + yi * tc).astype(bi_ref.dtype)


def _fft_mid(ar, ai, hr, hi, consts):
    b, _, _, c = ar.shape
    plane = pl.BlockSpec((b, 1, DFT_N, c), lambda k1: (0, k1, 0, 0))
    hspec = pl.BlockSpec((1, DFT_N, c), lambda k1: (k1, 0, 0))
    tw = pl.BlockSpec((1, DFT_N, 1), lambda k1: (k1, 0, 0))
    gspec = pl.BlockSpec((2 * DFT_N, 2 * DFT_N), lambda k1: (0, 0))
    return pl.pallas_call(
        _fft_mid_kernel,
        out_shape=[jax.ShapeDtypeStruct(ar.shape, BF16)] * 2,
        grid=(DFT_HALF,),
        in_specs=[plane, plane, hspec, hspec, tw, tw, gspec, gspec],
        out_specs=[plane, plane],
        compiler_params=_cparams("parallel"),
        name="fft_mid",
    )(ar, ai, hr, hi, consts["tw_c"], consts["tw_s"], consts["g_fwd"], consts["g_inv"])


def _idft_gate_kernel(fr_ref, fi_ref, br_ref, bi_ref, x0_ref, vx_ref, sk_ref, o_ref):
    y = (jnp.dot(fr_ref[...], br_ref[0], preferred_element_type=F32)
         + jnp.dot(fi_ref[...], bi_ref[0], preferred_element_type=F32))
    o_ref[0] = (x0_ref[0] * (y + vx_ref[0] * sk_ref[...])).astype(o_ref.dtype)


def _idft_gate(fo_r, fo_i, br, bi, x0, vx, skip_row):
    b, r, w = br.shape
    m = fo_r.shape[0]
    tn = 4096
    fspec = pl.BlockSpec((m, r), lambda bi_, j: (0, 0))
    rows = pl.BlockSpec((1, r, tn), lambda bi_, j: (bi_, 0, j))
    blk = pl.BlockSpec((1, m, tn), lambda bi_, j: (bi_, 0, j))
    return pl.pallas_call(
        _idft_gate_kernel,
        out_shape=jax.ShapeDtypeStruct((b, m, w), BF16),
        grid=(b, w // tn),
        in_specs=[fspec, fspec, rows, rows, blk, blk, pl.BlockSpec((1, tn), lambda bi_, j: (0, j))],
        out_specs=blk,
        compiler_params=_cparams("parallel", "parallel"),
        name="idft_gate",
    )(fo_r, fo_i, br, bi, x0, vx, skip_row)


def _hyena_long_conv(x0, vx, h_f, h_b, skip):
    b, n, c = vx.shape
    n1 = n // DFT_N
    w = DFT_N * c
    cn = _dft_consts()
    bf = lambda a: jnp.asarray(a, BF16)
    f32 = lambda a: jnp.asarray(a, F32)
    consts = dict(tw_c=f32(cn["tw_c"]), tw_s=f32(cn["tw_s"]), g_fwd=bf(cn["g_fwd"]), g_inv=bf(cn["g_inv"]))
    kfull = jnp.concatenate([h_f, jnp.zeros((1, c), F32), jnp.flip(h_b[1:], axis=0)], axis=0)
    kr, ki = _dft_rows(bf(cn["fa_r"]), bf(cn["fa_i"]), kfull.reshape(1, DFT_N, w), F32)
    hr, hi = _filter_spectrum(kr.reshape(1, DFT_HALF, DFT_N, c), ki.reshape(1, DFT_HALF, DFT_N, c), consts)
    ar, ai = _dft_rows(bf(cn["fa_r"][:, :n1]), bf(cn["fa_i"][:, :n1]), vx.reshape(b, n1, w), BF16)
    br, bi = _fft_mid(ar.reshape(b, DFT_HALF, DFT_N, c), ai.reshape(b, DFT_HALF, DFT_N, c), hr, hi, consts)
    y = _idft_gate(bf(cn["fo_r"]), bf(cn["fo_i"]), br.reshape(b, DFT_HALF, w), bi.reshape(b, DFT_HALF, w),
                   x0.reshape(b, n1, w), vx.reshape(b, n1, w), jnp.tile(skip.reshape(1, c), (1, DFT_N)))
    return y.reshape(b, n, c)


def _l1c_kernel(y_ref, h_ref, g1_ref, sc2_ref, sh2_ref, w_ref, lg_ref, lb_ref, rw_ref, ho_ref, z_ref, lo_ref):
    y = jnp.dot(y_ref[0], w_ref[...], preferred_element_type=F32)
    hh = _layer_norm(ALPHA * h_ref[0] + g1_ref[0] * y, lg_ref[...], lb_ref[...])
    ho_ref[0] = hh
    z = hh * (1.0 + sc2_ref[0]) + sh2_ref[0]
    z_ref[0] = z.astype(BF16)
    lo_ref[0] = jnp.dot(z, rw_ref[...], preferred_element_type=F32, precision=HIGHEST)


def _l1c(y, h, g1, sc2, sh2, out_w, ln_g, ln_b, router_w):
    b, n, d = h.shape
    tm = 512
    blk = pl.BlockSpec((1, tm, d), lambda bi, i: (bi, i, 0))
    vec = pl.BlockSpec((1, 1, d), lambda bi, i: (bi, 0, 0))
    full = lambda shape: pl.BlockSpec(shape, lambda bi, i: (0,) * len(shape))
    rw = jnp.pad(router_w, ((0, 0), (0, LANE - router_w.shape[1])))
    return pl.pallas_call(
        _l1c_kernel,
        out_shape=[jax.ShapeDtypeStruct((b, n, d), F32), jax.ShapeDtypeStruct((b, n, d), BF16),
                   jax.ShapeDtypeStruct((b, n, LANE), F32)],
        grid=(b, n // tm),
        in_specs=[blk, blk, vec, vec, vec, full((d, d)), full((1, d)), full((1, d)), full((d, LANE))],
        out_specs=[blk, blk, pl.BlockSpec((1, tm, LANE), lambda bi, i: (bi, i, 0))],
        compiler_params=_cparams("parallel", "parallel"),
        name="hyena_out_ln_router",
    )(y, h, g1, sc2, sh2, out_w.astype(BF16), ln_g.reshape(1, -1), ln_b.reshape(1, -1), rw)


def _route(logits, tm):
    t = logits.shape[0]
    top_v, top_i = lax.top_k(logits, TOP_K)
    gates = jax.nn.softmax(top_v, axis=-1)
    flat_e = top_i.reshape(-1).astype(jnp.int32)
    flat_g = gates.reshape(-1)
    n_sel = t * TOP_K
    p = n_sel + N_EXPERTS * tm
    order = jnp.argsort(flat_e, stable=True).astype(jnp.int32)
    counts = jnp.sum(flat_e[:, None] == jnp.arange(N_EXPERTS, dtype=jnp.int32)[None, :], axis=0).astype(jnp.int32)
    padded = ((counts + tm - 1) // tm) * tm
    start_p = jnp.cumsum(padded) - padded
    start = jnp.cumsum(counts) - counts
    e_sorted = flat_e[order]
    dest = start_p[e_sorted] + (jnp.arange(n_sel, dtype=jnp.int32) - start[e_sorted])
    row_token = jnp.zeros((p,), jnp.int32).at[dest].set(order // TOP_K)
    row_gate = jnp.zeros((p,), F32).at[dest].set(flat_g[order])
    pos = jnp.zeros((n_sel,), jnp.int32).at[order].set(dest).reshape(t, TOP_K)
    tile_start = jnp.arange(p // tm, dtype=jnp.int32) * tm
    end_p = start_p + padded
    tile_expert = jnp.sum(tile_start[:, None] >= end_p[None, :], axis=1).astype(jnp.int32)
    tile_valid = (tile_expert < N_EXPERTS).astype(jnp.int32)
    tile_expert = jnp.minimum(tile_expert, N_EXPERTS - 1)
    return row_token, row_gate[:, None], tile_expert, tile_valid, pos


def kernel(x, c, ctx, c_ctx, ada_w, ada_b, ln_g, ln_b, mix_in_w, pool_w, pool_scale, q_norm, q_up, kv_norm, kv_up, mix_out_w, ffn_gate, ffn_up, ffn_down, hy_in_w, hy_conv_w, hy_conv_b, hy_fw1, hy_fb1, hy_fw2, hy_fb2, hy_fw3, hy_fb3, hy_fout, hy_freq, hy_skip, hy_out_w, router_w, moe_gate, moe_up, moe_down):
    b, n, d = x.shape
    t = b * n
    assert b + 1 <= 8
    s_rows = jnp.concatenate([c, c_ctx[None, :], jnp.zeros((8 - b - 1, d), F32)], axis=0)
    mod = _ada(s_rows, ada_w, ada_b)

    def chunks(l, rows):
        m = mod[l, rows].reshape(-1, 6, d)
        return [m[:, k][:, None, :] for k in range(6)]

    sh1, sc1, g1, sh2, sc2, g2 = chunks(0, slice(0, b))
    shc, scc = chunks(0, slice(b, b + 1))[:2]
    pool_u, q, k, v = _l0a(x, ctx, sc1, sh1, scc, shc, mix_in_w[0], q_norm[0], q_up[0], kv_norm[0], kv_up[0])
    attn_o = _attention(q, k, v, n)
    h1, z1 = _l0b(pool_u, attn_o, x, g1, sc2, sh2, pool_w[0], pool_scale[0], mix_out_w[0], ln_g[0, 0], ln_b[0, 0])
    tm_ffn = 512
    ones_tiles = jnp.ones((t // tm_ffn,), jnp.int32)
    f0 = _ffn_grouped(z1.reshape(t, d), ffn_gate.astype(BF16), ffn_up.astype(BF16), ffn_down.astype(BF16),
                      jnp.ones((t, 1), F32), jnp.zeros((t // tm_ffn,), jnp.int32), ones_tiles, tm_ffn, 1408)
    sh1, sc1, g1b, sh2b, sc2b, g2b = chunks(1, slice(0, b))
    h2, u2 = _resid_ln(h1, [f0.reshape(b, n, d)], g2, ln_g[0, 1], ln_b[0, 1], mod=(sc1, sh1))

    x0, vx = _l1a(u2, hy_in_w[0], hy_conv_w[0], hy_conv_b[0])
    h_f, h_b = _hyena_filters(n, hy_fw1[0], hy_fb1[0], hy_fw2[0], hy_fb2[0], hy_fw3[0], hy_fb3[0],
                              hy_fout[0], hy_freq[0])
    yl = _hyena_long_conv(x0, vx, h_f, h_b, hy_skip[0])
    h3, z3, logits = _l1c(yl, h2, g1b, sc2b, sh2b, hy_out_w[0], ln_g[1, 0], ln_b[1, 0], router_w[0])

    tm_moe = 512
    row_token, row_gate, tile_expert, tile_valid, pos = _route(logits.reshape(t, LANE)[:, :N_EXPERTS], tm_moe)
    zs = jnp.take(z3.reshape(t, d), row_token, axis=0)
    ys = _ffn_grouped(zs, moe_gate[0].astype(BF16), moe_up[0].astype(BF16), moe_down[0].astype(BF16),
                      row_gate, tile_expert, tile_valid, tm_moe, 896)
    y_a = jnp.take(ys, pos[:, 0], axis=0).reshape(b, n, d)
    y_b = jnp.take(ys, pos[:, 1], axis=0).reshape(b, n, d)
    (h4,) = _resid_ln(h3, [y_a, y_b], g2b, ln_g[1, 1], ln_b[1, 1])
    return h4
```

```python
import functools
import math

import numpy as np
import jax
import jax.numpy as jnp
from jax import lax
from jax.experimental import pallas as pl
from jax.experimental.pallas import tpu as pltpu

F32 = jnp.float32
BF16 = jnp.bfloat16
HIGHEST = lax.Precision.HIGHEST

D_MODEL = 1024
GRID_W = 64
POOL_W = 512
POOL_WINDOWS = (2, 4, 8, 16)
POOL_GC = 128
MLA_HEADS = 8
QK_NOPE = 64
QK_ROPE = 32
V_HEAD = 64
Q_LORA = 256
KV_LORA = 256
ROPE_AXIS = 16
ROPE_BASE = 10000.0
N_EXPERTS = 8
TOP_K = 2
HY_BANDS = 16
HY_EMB = 1 + 2 * HY_BANDS
HY_FAST_DECAY = 0.3
HY_SLOW_DECAY = 1.5
HY_TARGET = 1e-2
HY_MIN_DECAY = math.log(HY_TARGET) / HY_SLOW_DECAY
HY_MAX_DECAY = math.log(HY_TARGET) / HY_FAST_DECAY
LN_EPS = 1e-5
RMS_EPS = 1e-6
DEPTH = 2
ALPHA = (2.0 * DEPTH) ** 0.25

LANE = 128
HEAD_SLOT = 128
DFT_N = 128
DFT_HALF = DFT_N // 2 + 1
VMEM_LIMIT = 56 * 1024 * 1024


def _cparams(*sem):
    return pltpu.CompilerParams(dimension_semantics=sem, vmem_limit_bytes=VMEM_LIMIT)


def _layer_norm(v, g, b):
    mu = jnp.mean(v, axis=-1, keepdims=True)
    c = v - mu
    var = jnp.mean(c * c, axis=-1, keepdims=True)
    return c * lax.rsqrt(var + LN_EPS) * g + b


def _rms_norm(v, g):
    return v * lax.rsqrt(jnp.mean(v * v, axis=-1, keepdims=True) + RMS_EPS) * g


def _ada_kernel(s_ref, w_ref, b_ref, o_ref):
    s = s_ref[...]
    s = s * jax.nn.sigmoid(s)
    o_ref[0] = jnp.dot(s, w_ref[0], preferred_element_type=F32, precision=HIGHEST) + b_ref[0]


def _ada(s_rows, ada_w, ada_b):
    depth, d, n6 = ada_w.shape
    tn = 768
    return pl.pallas_call(
        _ada_kernel,
        out_shape=jax.ShapeDtypeStruct((depth, 8, n6), F32),
        grid=(depth, n6 // tn),
        in_specs=[
            pl.BlockSpec((8, d), lambda l, j: (0, 0)),
            pl.BlockSpec((1, d, tn), lambda l, j: (l, 0, j)),
            pl.BlockSpec((1, 1, tn), lambda l, j: (l, 0, j)),
        ],
        out_specs=pl.BlockSpec((1, 8, tn), lambda l, j: (l, 0, j)),
        compiler_params=_cparams("parallel", "parallel"),
        name="ada_mod",
    )(s_rows, ada_w, ada_b.reshape(depth, 1, n6))


def _l0a_kernel(x_ref, c_ref, sc_ref, sh_ref, scc_ref, shc_ref, win_ref, qn_ref, kvn_ref,
                wqa_ref, wqb_ref, wk_ref, wv_ref, cos_ref, sin_ref,
                pu_ref, q_ref, k_ref, v_ref, u_scr, *, n_lat_tiles):
    i = pl.program_id(1)

    @pl.when(i < n_lat_tiles)
    def _():
        u_scr[...] = (x_ref[0] * (1.0 + sc_ref[0]) + sh_ref[0]).astype(BF16)

    @pl.when(i >= n_lat_tiles)
    def _():
        u_scr[...] = (c_ref[0] * (1.0 + scc_ref[0]) + shc_ref[0]).astype(BF16)

    proj = jnp.dot(u_scr[...], win_ref[...], preferred_element_type=F32)
    pu_ref[0] = proj[:, :POOL_W]
    cos = cos_ref[...]
    sin = sin_ref[...]
    q0 = POOL_W
    kv0 = POOL_W + Q_LORA
    r0 = kv0 + KV_LORA
    qn = _rms_norm(proj[:, q0:kv0], qn_ref[...]).astype(BF16)
    kvn = _rms_norm(proj[:, kv0:r0], kvn_ref[...]).astype(BF16)
    qa = jnp.dot(qn, wqa_ref[...], preferred_element_type=F32)
    qb = jnp.dot(qn, wqb_ref[...], preferred_element_type=F32)
    kn = jnp.dot(kvn, wk_ref[...], preferred_element_type=F32)
    vt = lax.dot_general(wv_ref[...], kvn, (((1,), (1,)), ((), ())), preferred_element_type=F32)
    krot = proj[:, r0:r0 + HEAD_SLOT] * cos + proj[:, r0 + HEAD_SLOT:r0 + 2 * HEAD_SLOT] * sin
    for h in range(MLA_HEADS):
        sl = slice(h * HEAD_SLOT, (h + 1) * HEAD_SLOT)
        q_ref[0, h] = (qa[:, sl] * cos + qb[:, sl] * sin).astype(BF16)
        k_ref[0, h] = (kn[:, sl] + krot).astype(BF16)
        v_ref[0, h] = vt[h * V_HEAD:(h + 1) * V_HEAD].astype(BF16)


def _rope_swap_index():
    half = ROPE_AXIS // 2
    idx = []
    for a in range(2):
        base = a * ROPE_AXIS
        idx += list(range(base + half, base + ROPE_AXIS)) + list(range(base, base + half))
    return np.array(idx)


def _rope_tables(n, n_ctx):
    rows = n // GRID_W
    r = jnp.repeat(jnp.arange(rows, dtype=F32), GRID_W)
    col = jnp.tile(jnp.arange(GRID_W, dtype=F32), rows)
    inv = ROPE_BASE ** (-jnp.arange(0, ROPE_AXIS, 2, dtype=F32) / ROPE_AXIS)
    ang_r = r[:, None] * inv
    ang_c = col[:, None] * inv
    cos32 = jnp.concatenate([jnp.cos(ang_r), jnp.cos(ang_r), jnp.cos(ang_c), jnp.cos(ang_c)], axis=-1)
    sin32 = jnp.concatenate([-jnp.sin(ang_r), jnp.sin(ang_r), -jnp.sin(ang_c), jnp.sin(ang_c)], axis=-1)
    pad = HEAD_SLOT - QK_NOPE - QK_ROPE
    cos = jnp.concatenate([jnp.ones((n, QK_NOPE), F32), cos32, jnp.ones((n, pad), F32)], axis=-1)
    sin = jnp.concatenate([jnp.zeros((n, QK_NOPE), F32), sin32, jnp.zeros((n, pad), F32)], axis=-1)
    cos = jnp.concatenate([cos, jnp.ones((n_ctx, HEAD_SLOT), F32)], axis=0)
    sin = jnp.concatenate([sin, jnp.zeros((n_ctx, HEAD_SLOT), F32)], axis=0)
    return cos, sin


def _l0a_weights(in_w, q_up, kv_up):
    swap = _rope_swap_index()
    d = in_w.shape[0]
    r0 = POOL_W + Q_LORA + KV_LORA
    w_rope = in_w[:, r0:]
    pad_l = jnp.zeros((d, QK_NOPE), F32)
    pad_r = jnp.zeros((d, HEAD_SLOT - QK_NOPE - QK_ROPE), F32)
    kr_a = jnp.concatenate([pad_l, w_rope, pad_r], axis=1)
    kr_b = jnp.concatenate([pad_l, w_rope[:, swap], pad_r], axis=1)
    w_in = jnp.concatenate([in_w[:, :r0], kr_a, kr_b], axis=1).astype(BF16)

    scale = (QK_NOPE + QK_ROPE) ** -0.5 * math.log2(math.e)
    qu = q_up.reshape(Q_LORA, MLA_HEADS, QK_NOPE + QK_ROPE) * scale
    zpad = jnp.zeros((Q_LORA, MLA_HEADS, HEAD_SLOT - QK_NOPE - QK_ROPE), F32)
    wq_a = jnp.concatenate([qu, zpad], axis=-1).reshape(Q_LORA, MLA_HEADS * HEAD_SLOT).astype(BF16)
    wq_b = jnp.concatenate([jnp.zeros((Q_LORA, MLA_HEADS, QK_NOPE), F32), qu[..., QK_NOPE:][..., swap], zpad],
                           axis=-1).reshape(Q_LORA, MLA_HEADS * HEAD_SLOT).astype(BF16)
    kvu = kv_up.reshape(KV_LORA, MLA_HEADS, QK_NOPE + V_HEAD)
    wk = jnp.concatenate([kvu[..., :QK_NOPE], jnp.zeros((KV_LORA, MLA_HEADS, HEAD_SLOT - QK_NOPE), F32)],
                         axis=-1).reshape(KV_LORA, MLA_HEADS * HEAD_SLOT).astype(BF16)
    wv_t = kvu[..., QK_NOPE:].reshape(KV_LORA, MLA_HEADS * V_HEAD).T.astype(BF16)
    return w_in, wq_a, wq_b, wk, wv_t


def _l0a(x, ctx, sc, sh, scc, shc, in_w, q_norm, q_up, kv_norm, kv_up):
    b, n, d = x.shape
    n_ctx = ctx.shape[1]
    tm = 256
    nl = n // tm
    nt = (n + n_ctx) // tm
    w_in, wq_a, wq_b, wk, wv_t = _l0a_weights(in_w, q_up, kv_up)
    cos, sin = _rope_tables(n, n_ctx)
    hw = MLA_HEADS * HEAD_SLOT
    full = lambda shape: pl.BlockSpec(shape, lambda bi, i: (0,) * len(shape))
    vec = pl.BlockSpec((1, 1, d), lambda bi, i: (bi, 0, 0))
    return pl.pallas_call(
        functools.partial(_l0a_kernel, n_lat_tiles=nl),
        out_shape=[
            jax.ShapeDtypeStruct((b, n + n_ctx, POOL_W), F32),
            jax.ShapeDtypeStruct((b, MLA_HEADS, n + n_ctx, HEAD_SLOT), BF16),
            jax.ShapeDtypeStruct((b, MLA_HEADS, n + n_ctx, HEAD_SLOT), BF16),
            jax.ShapeDtypeStruct((b, MLA_HEADS, V_HEAD, n + n_ctx), BF16),
        ],
        grid=(b, nt),
        in_specs=[
            pl.BlockSpec((1, tm, d), lambda bi, i: (bi, jnp.minimum(i, nl - 1), 0)),
            pl.BlockSpec((1, tm, d), lambda bi, i: (bi, jnp.maximum(i - nl, 0), 0)),
            vec, vec, full((1, 1, d)), full((1, 1, d)),
            full(w_in.shape), full((1, Q_LORA)), full((1, KV_LORA)),
            full((Q_LORA, hw)), full((Q_LORA, hw)), full((KV_LORA, hw)), full((MLA_HEADS * V_HEAD, KV_LORA)),
            pl.BlockSpec((tm, HEAD_SLOT), lambda bi, i: (i, 0)),
            pl.BlockSpec((tm, HEAD_SLOT), lambda bi, i: (i, 0)),
        ],
        out_specs=[
            pl.BlockSpec((1, tm, POOL_W), lambda bi, i: (bi, i, 0)),
            pl.BlockSpec((1, MLA_HEADS, tm, HEAD_SLOT), lambda bi, i: (bi, 0, i, 0)),
            pl.BlockSpec((1, MLA_HEADS, tm, HEAD_SLOT), lambda bi, i: (bi, 0, i, 0)),
            pl.BlockSpec((1, MLA_HEADS, V_HEAD, tm), lambda bi, i: (bi, 0, 0, i)),
        ],
        scratch_shapes=[pltpu.VMEM((tm, d), BF16)],
        compiler_params=_cparams("parallel", "arbitrary"),
        name="l0_in_proj",
    )(x, ctx, sc, sh, scc, shc, w_in, q_norm.reshape(1, -1), kv_norm.reshape(1, -1),
      wq_a, wq_b, wk, wv_t, cos, sin)


def _attn_kernel(q_ref, k_ref, v_ref, o_ref, m_ref, l_ref, acc_ref, *, nk):
    ki = pl.program_id(2)

    @pl.when(ki == 0)
    def _():
        m_ref[...] = jnp.full(m_ref.shape, -jnp.inf, F32)
        l_ref[...] = jnp.zeros(l_ref.shape, F32)
        acc_ref[...] = jnp.zeros(acc_ref.shape, F32)

    for h in range(MLA_HEADS):
        st = lax.dot_general(k_ref[0, h], q_ref[0, h], (((1,), (1,)), ((), ())),
                             preferred_element_type=F32)
        m_prev = m_ref[h]
        m_new = jnp.maximum(m_prev, jnp.max(st, axis=0, keepdims=True))
        a = jnp.exp2(m_prev - m_new)
        p = jnp.exp2(st - m_new)
        l_ref[h] = a * l_ref[h] + jnp.sum(p, axis=0, keepdims=True)
        pv = jnp.dot(v_ref[0, h], p.astype(BF16), preferred_element_type=F32)
        rows = slice(h * V_HEAD, (h + 1) * V_HEAD)
        acc_ref[rows, :] = a * acc_ref[rows, :] + pv
        m_ref[h] = m_new

    @pl.when(ki == nk - 1)
    def _():
        for h in range(MLA_HEADS):
            rows = slice(h * V_HEAD, (h + 1) * V_HEAD)
            acc_ref[rows, :] = acc_ref[rows, :] / l_ref[h]
        o_ref[0] = acc_ref[...].T.astype(o_ref.dtype)


def _attention(q, k, vt, n):
    b, h, n_all, _ = k.shape
    tq, tk = 512, 768
    nk = n_all // tk
    return pl.pallas_call(
        functools.partial(_attn_kernel, nk=nk),
        out_shape=jax.ShapeDtypeStruct((b, n, h * V_HEAD), BF16),
        grid=(b, n // tq, nk),
        in_specs=[
            pl.BlockSpec((1, h, tq, HEAD_SLOT), lambda bi, qi, ki: (bi, 0, qi, 0)),
            pl.BlockSpec((1, h, tk, HEAD_SLOT), lambda bi, qi, ki: (bi, 0, ki, 0)),
            pl.BlockSpec((1, h, V_HEAD, tk), lambda bi, qi, ki: (bi, 0, 0, ki)),
        ],
        out_specs=pl.BlockSpec((1, tq, h * V_HEAD), lambda bi, qi, ki: (bi, qi, 0)),
        scratch_shapes=[
            pltpu.VMEM((h, 1, tq), F32),
            pltpu.VMEM((h, 1, tq), F32),
            pltpu.VMEM((h * V_HEAD, tq), F32),
        ],
        compiler_params=_cparams("parallel", "parallel", "arbitrary"),
        name="mla_attention",
    )(q, k, vt)


def _l0b_kernel(pp_ref, pc_ref, pn_ref, o_ref, x_ref, g1_ref, sc2_ref, sh2_ref, pw_ref, ps_ref, ow_ref,
                lg_ref, lb_ref, h_ref, z_ref, *, tm, n):
    i = pl.program_id(1)
    halo = 8
    ext = jnp.concatenate([pp_ref[0], pc_ref[0], pn_ref[0]], axis=0)
    pos = i * tm - halo + lax.broadcasted_iota(jnp.int32, (tm + 2 * halo, 1), 0)
    ext = jnp.where((pos >= 0) & (pos < n), ext, 0.0)
    t = i * tm + lax.broadcasted_iota(jnp.int32, (tm, 1), 0)
    y = jnp.dot(o_ref[0], ow_ref[POOL_W:, :], preferred_element_type=F32)
    for g, w in enumerate(POOL_WINDOWS):
        hw = w // 2
        e = ext[:, g * POOL_GC:(g + 1) * POOL_GC]
        s = e
        width = 1
        while width < w:
            s = s[:s.shape[0] - width] + s[width:]
            width *= 2
        win = s[halo - hw:halo - hw + tm]
        cnt = (jnp.minimum(t + hw, n) - jnp.maximum(t - hw, 0)).astype(F32)
        dd = (win / cnt - e[halo:halo + tm]).astype(BF16)
        yg = jnp.dot(dd, pw_ref[g], preferred_element_type=F32) * ps_ref[:, g * POOL_GC:(g + 1) * POOL_GC]
        y = y + jnp.dot(yg.astype(BF16), ow_ref[g * POOL_GC:(g + 1) * POOL_GC, :], preferred_element_type=F32)
    hh = _layer_norm(ALPHA * x_ref[0] + g1_ref[0] * y, lg_ref[...], lb_ref[...])
    h_ref[0] = hh
    z_ref[0] = (hh * (1.0 + sc2_ref[0]) + sh2_ref[0]).astype(BF16)


def _l0b(pool_u, attn_o, x, g1, sc2, sh2, pool_w, pool_scale, out_w, ln_g, ln_b):
    b, n, d = x.shape
    tm = 512
    hb = tm // 8
    vec = pl.BlockSpec((1, 1, d), lambda bi, i: (bi, 0, 0))
    full = lambda shape: pl.BlockSpec(shape, lambda bi, i: (0,) * len(shape))
    return pl.pallas_call(
        functools.partial(_l0b_kernel, tm=tm, n=n),
        out_shape=[jax.ShapeDtypeStruct((b, n, d), F32), jax.ShapeDtypeStruct((b, n, d), BF16)],
        grid=(b, n // tm),
        in_specs=[
            pl.BlockSpec((1, 8, POOL_W), lambda bi, i: (bi, jnp.maximum(i * hb - 1, 0), 0)),
            pl.BlockSpec((1, tm, POOL_W), lambda bi, i: (bi, i, 0)),
            pl.BlockSpec((1, 8, POOL_W), lambda bi, i: (bi, (i + 1) * hb, 0)),
            pl.BlockSpec((1, tm, POOL_W), lambda bi, i: (bi, i, 0)),
            pl.BlockSpec((1, tm, d), lambda bi, i: (bi, i, 0)),
            vec, vec, vec,
            full(pool_w.shape), full((1, POOL_W)), full(out_w.shape), full((1, d)), full((1, d)),
        ],
        out_specs=[pl.BlockSpec((1, tm, d), lambda bi, i: (bi, i, 0)),
                   pl.BlockSpec((1, tm, d), lambda bi, i: (bi, i, 0))],
        compiler_params=_cparams("parallel", "parallel"),
        name="l0_pool_out_ln",
    )(pool_u, pool_u, pool_u, attn_o, x, g1, sc2, sh2, pool_w.astype(BF16), pool_scale.reshape(1, -1),
      out_w.astype(BF16), ln_g.reshape(1, -1), ln_b.reshape(1, -1))


def _ffn_kernel(te_ref, tv_ref, z_ref, wg_ref, wu_ref, wd_ref, gate_ref, o_ref, acc_ref, *, nj):
    i = pl.program_id(0)
    j = pl.program_id(1)

    @pl.when(j == 0)
    def _():
        acc_ref[...] = jnp.zeros(acc_ref.shape, F32)

    @pl.when(tv_ref[i] > 0)
    def _():
        z = z_ref[...].astype(BF16)
        g = jnp.dot(z, wg_ref[0], preferred_element_type=F32)
        u = jnp.dot(z, wu_ref[0], preferred_element_type=F32)
        a = (g * jax.nn.sigmoid(g) * u).astype(BF16)
        acc_ref[...] += jnp.dot(a, wd_ref[0], preferred_element_type=F32)

    @pl.when(j == nj - 1)
    def _():
        o_ref[...] = acc_ref[...] * gate_ref[...]


def _ffn_grouped(z, wg, wu, wd, gate, tile_expert, tile_valid, tm, tf):
    p, d = z.shape
    f = wg.shape[-1]
    nj = f // tf
    return pl.pallas_call(
        functools.partial(_ffn_kernel, nj=nj),
        out_shape=jax.ShapeDtypeStruct((p, d), F32),
        grid_spec=pltpu.PrefetchScalarGridSpec(
            num_scalar_prefetch=2,
            grid=(p // tm, nj),
            in_specs=[
                pl.BlockSpec((tm, d), lambda i, j, te, tv: (i, 0)),
                pl.BlockSpec((1, d, tf), lambda i, j, te, tv: (te[i], 0, j)),
                pl.BlockSpec((1, d, tf), lambda i, j, te, tv: (te[i], 0, j)),
                pl.BlockSpec((1, tf, d), lambda i, j, te, tv: (te[i], j, 0)),
                pl.BlockSpec((tm, 1), lambda i, j, te, tv: (i, 0)),
            ],
            out_specs=pl.BlockSpec((tm, d), lambda i, j, te, tv: (i, 0)),
            scratch_shapes=[pltpu.VMEM((tm, d), F32)],
        ),
        compiler_params=_cparams("parallel", "arbitrary"),
        name="swiglu_grouped",
    )(tile_expert, tile_valid, z, wg, wu, wd, gate)


def _resid_ln_kernel(*refs, n_y, with_mod):
    h_ref = refs[0]
    y_refs = refs[1:1 + n_y]
    g_ref, lg_ref, lb_ref = refs[1 + n_y:4 + n_y]
    rest = refs[4 + n_y:]
    y = y_refs[0][0]
    for r in y_refs[1:]:
        y = y + r[0]
    hh = _layer_norm(ALPHA * h_ref[0] + g_ref[0] * y, lg_ref[...], lb_ref[...])
    if with_mod:
        sc_ref, sh_ref, ho_ref, u_ref = rest
        ho_ref[0] = hh
        u_ref[0] = (hh * (1.0 + sc_ref[0]) + sh_ref[0]).astype(BF16)
    else:
        rest[0][0] = hh


def _resid_ln(h, ys, g, ln_g, ln_b, mod=None):
    b, n, d = h.shape
    tm = 1024
    blk = pl.BlockSpec((1, tm, d), lambda bi, i: (bi, i, 0))
    vec = pl.BlockSpec((1, 1, d), lambda bi, i: (bi, 0, 0))
    row = pl.BlockSpec((1, d), lambda bi, i: (0, 0))
    in_specs = [blk] + [blk] * len(ys) + [vec, row, row]
    args = [h, *ys, g, ln_g.reshape(1, -1), ln_b.reshape(1, -1)]
    out_shape = [jax.ShapeDtypeStruct((b, n, d), F32)]
    out_specs = [blk]
    if mod is not None:
        in_specs += [vec, vec]
        args += list(mod)
        out_shape.append(jax.ShapeDtypeStruct((b, n, d), BF16))
        out_specs.append(blk)
    return pl.pallas_call(
        functools.partial(_resid_ln_kernel, n_y=len(ys), with_mod=mod is not None),
        out_shape=out_shape,
        grid=(b, n // tm),
        in_specs=in_specs,
        out_specs=out_specs,
        compiler_params=_cparams("parallel", "parallel"),
        name="resid_ln",
    )(*args)


def _l1a_kernel(up_ref, uc_ref, un_ref, w_ref, cw_ref, cb_ref, x0_ref, vx_ref, *, tm, n):
    i = pl.program_id(1)
    halo = 16
    ext = jnp.concatenate([up_ref[0], uc_ref[0], un_ref[0]], axis=0)
    pos = i * tm - halo + lax.broadcasted_iota(jnp.int32, (tm + 2 * halo, 1), 0)
    ext = jnp.where((pos >= 0) & (pos < n), ext, jnp.zeros_like(ext))
    z = jnp.dot(ext, w_ref[...], preferred_element_type=F32)
    cw = cw_ref[...]
    zc = (cw[0:1] * z[halo - 1:halo - 1 + tm] + cw[1:2] * z[halo:halo + tm]
          + cw[2:3] * z[halo + 1:halo + 1 + tm] + cb_ref[...])
    c = zc.shape[1] // 3
    x0_ref[0] = zc[:, :c]
    vx_ref[0] = zc[:, 2 * c:] * zc[:, c:2 * c]


def _l1a(u, hy_in_w, conv_w, conv_b):
    b, n, d = u.shape
    c3 = hy_in_w.shape[1]
    c = c3 // 3
    tm = 512
    hb = tm // 16
    nb16 = n // 16
    full = lambda shape: pl.BlockSpec(shape, lambda bi, i: (0,) * len(shape))
    return pl.pallas_call(
        functools.partial(_l1a_kernel, tm=tm, n=n),
        out_shape=[jax.ShapeDtypeStruct((b, n, c), F32), jax.ShapeDtypeStruct((b, n, c), F32)],
        grid=(b, n // tm),
        in_specs=[
            pl.BlockSpec((1, 16, d), lambda bi, i: (bi, jnp.maximum(i * hb - 1, 0), 0)),
            pl.BlockSpec((1, tm, d), lambda bi, i: (bi, i, 0)),
            pl.BlockSpec((1, 16, d), lambda bi, i: (bi, jnp.minimum((i + 1) * hb, nb16 - 1), 0)),
            full((d, c3)), full((3, c3)), full((1, c3)),
        ],
        out_specs=[pl.BlockSpec((1, tm, c), lambda bi, i: (bi, i, 0)),
                   pl.BlockSpec((1, tm, c), lambda bi, i: (bi, i, 0))],
        compiler_params=_cparams("parallel", "parallel"),
        name="hyena_in_conv",
    )(u, u, u, hy_in_w.astype(BF16), conv_w, conv_b.reshape(1, -1))


def _filter_kernel(z_ref, w1_ref, b1_ref, w2_ref, b2_ref, w3_ref, b3_ref, wo_ref, fr_ref, dl_ref, k_ref, *, tm, n):
    z = z_ref[...]
    fr = fr_ref[...]
    dot = functools.partial(jnp.dot, preferred_element_type=F32, precision=HIGHEST)
    h = jnp.sin(fr * (dot(z, w1_ref[...]) + b1_ref[...]))
    h = jnp.sin(fr * (dot(h, w2_ref[...]) + b2_ref[...]))
    h = jnp.sin(fr * (dot(h, w3_ref[...]) + b3_ref[...]))
    o = dot(h, wo_ref[...]) * jnp.exp(-z[:, 0:1] * dl_ref[...])
    row = pl.program_id(0) * tm + lax.broadcasted_iota(jnp.int32, (tm, 1), 0)
    k_ref[...] = jnp.where(row == n, 0.0, o)


def _hyena_filters(n, fw1, fb1, fw2, fb2, fw3, fb3, fout, freq):
    c = fout.shape[1] // 2
    m = jnp.arange(n, dtype=F32)
    p = jnp.concatenate([m, n - m])[:, None]
    t = p / (n - 1.0)
    w_ang = (2.0 * math.pi / n) * p
    bands = jnp.linspace(1e-4, HY_BANDS - 1, HY_BANDS, dtype=F32)[None, :]
    z = jnp.concatenate([t, jnp.cos(bands * w_ang), -jnp.sin(bands * w_ang),
                         jnp.zeros((2 * n, LANE - HY_EMB), F32)], axis=-1)
    deltas = jnp.abs(jnp.linspace(HY_MIN_DECAY, HY_MAX_DECAY, c, dtype=F32))[None, :]
    padc = lambda a: jnp.pad(a, ((0, 0), (0, LANE - a.shape[1])))
    padr = lambda a: jnp.pad(a, ((0, LANE - a.shape[0]), (0, 0)))
    row = lambda a: padc(a.reshape(1, -1))
    tm = 1024
    half = n // tm
    full = lambda shape: pl.BlockSpec(shape, lambda i: (0,) * len(shape))
    return pl.pallas_call(
        functools.partial(_filter_kernel, tm=tm, n=n),
        out_shape=jax.ShapeDtypeStruct((2 * n, c), F32),
        grid=(2 * half,),
        in_specs=[pl.BlockSpec((tm, LANE), lambda i: (i, 0)),
                  full((LANE, LANE)), full((1, LANE)), full((LANE, LANE)), full((1, LANE)),
                  full((LANE, LANE)), full((1, LANE)), pl.BlockSpec((LANE, c), lambda i: (0, i // half)),
                  full((1, LANE)), full((1, c))],
        out_specs=pl.BlockSpec((tm, c), lambda i: (i, 0)),
        compiler_params=_cparams("parallel"),
        name="hyena_filters",
    )(z, padc(padr(fw1)), row(fb1), padc(padr(fw2)), row(fb2), padc(padr(fw3)), row(fb3),
      padr(fout), row(freq), deltas)


def _dft_consts():
    k = np.arange(DFT_HALF)[:, None]
    n1 = np.arange(DFT_N)[None, :]
    th = 2.0 * np.pi * k * n1 / DFT_N
    fa_r, fa_i = np.cos(th), -np.sin(th)
    kk = np.arange(DFT_N)[:, None] * np.arange(DFT_N)[None, :]
    c, s = np.cos(2.0 * np.pi * kk / DFT_N), np.sin(2.0 * np.pi * kk / DFT_N)
    g_fwd = np.block([[c, s], [-s, c]])
    g_inv = np.block([[c, -s], [s, c]])
    ph = 2.0 * np.pi * np.arange(DFT_HALF)[:, None] * np.arange(DFT_N)[None, :] / (DFT_N * DFT_N)
    tw_c, tw_s = np.cos(ph)[:, :, None], np.sin(ph)[:, :, None]
    n_out = np.arange(DFT_N // 2)[:, None]
    ps = 2.0 * np.pi * n_out * np.arange(DFT_HALF)[None, :] / DFT_N
    wgt = np.full((1, DFT_HALF), 2.0)
    wgt[0, 0] = wgt[0, -1] = 1.0
    length = DFT_N * DFT_N
    fo_r = wgt * np.cos(ps) / length
    fo_i = -wgt * np.sin(ps) / length
    return dict(fa_r=fa_r, fa_i=fa_i, g_fwd=g_fwd, g_inv=g_inv, tw_c=tw_c, tw_s=tw_s, fo_r=fo_r, fo_i=fo_i)


DFT_ROW_PAD = 72
DFT_NB = 16


def _dft_a_kernel(f_ref, x_ref, or_ref, oi_ref):
    f = f_ref[...]
    for j in range(x_ref.shape[2]):
        y = jnp.dot(f, x_ref[0, :, j, :].astype(BF16), preferred_element_type=F32)
        or_ref[0, :, j, :] = y[:DFT_HALF].astype(or_ref.dtype)
        oi_ref[0, :, j, :] = y[DFT_ROW_PAD:DFT_ROW_PAD + DFT_HALF].astype(oi_ref.dtype)


def _dft_a(f_stack, x, out_dtype):
    b, k, n2, c = x.shape
    blk = lambda rows: pl.BlockSpec((1, rows, DFT_NB, c), lambda bi, j: (bi, 0, j, 0))
    return pl.pallas_call(
        _dft_a_kernel,
        out_shape=[jax.ShapeDtypeStruct((b, DFT_HALF, n2, c), out_dtype)] * 2,
        grid=(b, n2 // DFT_NB),
        in_specs=[pl.BlockSpec(f_stack.shape, lambda bi, j: (0, 0)), blk(k)],
        out_specs=[blk(DFT_HALF)] * 2,
        compiler_params=_cparams("parallel", "parallel"),
        name="dft_stage_a",
    )(f_stack, x)


def _spectrum_kernel(ar_ref, ai_ref, tc_ref, ts_ref, g_ref, hr_ref, hi_ref):
    tc = tc_ref[0]
    ts = ts_ref[0]
    ar = ar_ref[0, 0]
    ai = ai_ref[0, 0]
    a2 = jnp.concatenate([ar * tc + ai * ts, ai * tc - ar * ts], axis=0).astype(BF16)
    x = jnp.dot(g_ref[...], a2, preferred_element_type=F32)
    hr_ref[0] = x[:DFT_N]
    hi_ref[0] = x[DFT_N:]


def _filter_spectrum(kr, ki, consts):
    c = kr.shape[-1]
    plane = pl.BlockSpec((1, 1, DFT_N, c), lambda k1: (0, k1, 0, 0))
    tw = pl.BlockSpec((1, DFT_N, 1), lambda k1: (k1, 0, 0))
    out = pl.BlockSpec((1, DFT_N, c), lambda k1: (k1, 0, 0))
    return pl.pallas_call(
        _spectrum_kernel,
        out_shape=[jax.ShapeDtypeStruct((DFT_HALF, DFT_N, c), F32)] * 2,
        grid=(DFT_HALF,),
        in_specs=[plane, plane, tw, tw, pl.BlockSpec((2 * DFT_N, 2 * DFT_N), lambda k1: (0, 0))],
        out_specs=[out, out],
        compiler_params=_cparams("parallel"),
        name="filter_spectrum",
    )(kr, ki, consts["tw_c"], consts["tw_s"], consts["g_fwd"])


def _fft_mid_kernel(ar_ref, ai_ref, hr_ref, hi_ref, tc_ref, ts_ref, gf_ref, gi_ref, br_ref, bi_ref):
    tc = tc_ref[0]
    ts = ts_ref[0]
    hr = hr_ref[0]
    hi = hi_ref[0]
    for b in range(ar_ref.shape[0]):
        ar = ar_ref[b, 0].astype(F32)
        ai = ai_ref[b, 0].astype(F32)
        a2 = jnp.concatenate([ar * tc + ai * ts, ai * tc - ar * ts], axis=0).astype(BF16)
        x = jnp.dot(gf_ref[...], a2, preferred_element_type=F32)
        xr, xi = x[:DFT_N], x[DFT_N:]
        z = jnp.concatenate([xr * hr - xi * hi, xr * hi + xi * hr], axis=0).astype(BF16)
        y = jnp.dot(gi_ref[...], z, preferred_element_type=F32)
        yr, yi = y[:DFT_N], y[DFT_N:]
        br_ref[b, 0] = (yr * tc - yi * ts).astype(br_ref.dtype)
        bi_ref[b, 0] = (yr * ts + yi * tc).astype(bi_ref.dtype)


def _fft_mid(ar, ai, hr, hi, consts):
    b, _, _, c = ar.shape
    plane = pl.BlockSpec((b, 1, DFT_N, c), lambda k1: (0, k1, 0, 0))
    hspec = pl.BlockSpec((1, DFT_N, c), lambda k1: (k1, 0, 0))
    tw = pl.BlockSpec((1, DFT_N, 1), lambda k1: (k1, 0, 0))
    gspec = pl.BlockSpec((2 * DFT_N, 2 * DFT_N), lambda k1: (0, 0))
    return pl.pallas_call(
        _fft_mid_kernel,
        out_shape=[jax.ShapeDtypeStruct(ar.shape, BF16)] * 2,
        grid=(DFT_HALF,),
        in_specs=[plane, plane, hspec, hspec, tw, tw, gspec, gspec],
        out_specs=[plane, plane],
        compiler_params=_cparams("parallel"),
        name="fft_mid",
    )(ar, ai, hr, hi, consts["tw_c"], consts["tw_s"], consts["g_fwd"], consts["g_inv"])


def _idft_gate_kernel(fr_ref, fi_ref, br_ref, bi_ref, x0_ref, vx_ref, sk_ref, o_ref):
    fr = fr_ref[...]
    fi = fi_ref[...]
    sk = sk_ref[...]
    for j in range(br_ref.shape[2]):
        y = (jnp.dot(fr, br_ref[0, :, j, :], preferred_element_type=F32)
             + jnp.dot(fi, bi_ref[0, :, j, :], preferred_element_type=F32))
        o_ref[0, :, j, :] = (x0_ref[0, :, j, :] * (y + vx_ref[0, :, j, :] * sk)).astype(o_ref.dtype)


def _idft_gate(fo_r, fo_i, br, bi, x0, vx, skip_row):
    b, r, n2, c = br.shape
    m = fo_r.shape[0]
    fspec = pl.BlockSpec((m, r), lambda bi_, j: (0, 0))
    blk = lambda rows: pl.BlockSpec((1, rows, DFT_NB, c), lambda bi_, j: (bi_, 0, j, 0))
    return pl.pallas_call(
        _idft_gate_kernel,
        out_shape=jax.ShapeDtypeStruct((b, m, n2, c), BF16),
        grid=(b, n2 // DFT_NB),
        in_specs=[fspec, fspec, blk(r), blk(r), blk(m), blk(m), pl.BlockSpec((1, c), lambda bi_, j: (0, 0))],
        out_specs=blk(m),
        compiler_params=_cparams("parallel", "parallel"),
        name="idft_gate",
    )(fo_r, fo_i, br, bi, x0, vx, skip_row)


def _stack_rows(fr, fi):
    pad = np.zeros((DFT_ROW_PAD - DFT_HALF, fr.shape[1]))
    return jnp.asarray(np.concatenate([fr, pad, fi, pad], axis=0), BF16)


def _hyena_long_conv(x0, vx, kfull, skip):
    b, n, c = vx.shape
    n1 = n // DFT_N
    cn = _dft_consts()
    bf = lambda a: jnp.asarray(a, BF16)
    f32 = lambda a: jnp.asarray(a, F32)
    consts = dict(tw_c=f32(cn["tw_c"]), tw_s=f32(cn["tw_s"]), g_fwd=bf(cn["g_fwd"]), g_inv=bf(cn["g_inv"]))
    kr, ki = _dft_a(_stack_rows(cn["fa_r"], cn["fa_i"]), kfull.reshape(1, DFT_N, DFT_N, c), F32)
    hr, hi = _filter_spectrum(kr, ki, consts)
    vx4 = vx.reshape(b, n1, DFT_N, c)
    ar, ai = _dft_a(_stack_rows(cn["fa_r"][:, :n1], cn["fa_i"][:, :n1]), vx4, BF16)
    br, bi = _fft_mid(ar, ai, hr, hi, consts)
    y = _idft_gate(bf(cn["fo_r"]), bf(cn["fo_i"]), br, bi, x0.reshape(b, n1, DFT_N, c), vx4, skip.reshape(1, c))
    return y.reshape(b, n, c)


def _l1c_kernel(y_ref, h_ref, g1_ref, sc2_ref, sh2_ref, w_ref, lg_ref, lb_ref, rw_ref, ho_ref, z_ref, lo_ref):
    y = jnp.dot(y_ref[0], w_ref[...], preferred_element_type=F32)
    hh = _layer_norm(ALPHA * h_ref[0] + g1_ref[0] * y, lg_ref[...], lb_ref[...])
    ho_ref[0] = hh
    z = hh * (1.0 + sc2_ref[0]) + sh2_ref[0]
    z_ref[0] = z
    lo_ref[0] = jnp.dot(z, rw_ref[...], preferred_element_type=F32, precision=HIGHEST)


def _l1c(y, h, g1, sc2, sh2, out_w, ln_g, ln_b, router_w):
    b, n, d = h.shape
    tm = 512
    blk = pl.BlockSpec((1, tm, d), lambda bi, i: (bi, i, 0))
    vec = pl.BlockSpec((1, 1, d), lambda bi, i: (bi, 0, 0))
    full = lambda shape: pl.BlockSpec(shape, lambda bi, i: (0,) * len(shape))
    rw = jnp.pad(router_w, ((0, 0), (0, LANE - router_w.shape[1])))
    return pl.pallas_call(
        _l1c_kernel,
        out_shape=[jax.ShapeDtypeStruct((b, n, d), F32), jax.ShapeDtypeStruct((b, n, d), F32),
                   jax.ShapeDtypeStruct((b, n, LANE), F32)],
        grid=(b, n // tm),
        in_specs=[blk, blk, vec, vec, vec, full((d, d)), full((1, d)), full((1, d)), full((d, LANE))],
        out_specs=[blk, blk, pl.BlockSpec((1, tm, LANE), lambda bi, i: (bi, i, 0))],
        compiler_params=_cparams("parallel", "parallel"),
        name="hyena_out_ln_router",
    )(y, h, g1, sc2, sh2, out_w.astype(BF16), ln_g.reshape(1, -1), ln_b.reshape(1, -1), rw)


def _route(logits, tm):
    t = logits.shape[0]
    top_v, top_i = lax.top_k(logits, TOP_K)
    gates = jax.nn.softmax(top_v, axis=-1)
    flat_e = top_i.reshape(-1).astype(jnp.int32)
    flat_g = gates.reshape(-1)
    n_sel = t * TOP_K
    p = n_sel + N_EXPERTS * tm
    eids = jnp.arange(N_EXPERTS, dtype=jnp.int32)[None, :]
    onehot = (flat_e[:, None] == eids).astype(jnp.int32)
    csum = jnp.cumsum(onehot, axis=0)
    counts = csum[-1]
    rank = jnp.sum((csum - onehot) * onehot, axis=1)
    padded = ((counts + tm - 1) // tm) * tm
    end_p = jnp.cumsum(padded)
    start_p = end_p - padded
    start = jnp.cumsum(counts) - counts
    pos = (jnp.sum(onehot * start_p[None, :], axis=1) + rank).reshape(t, TOP_K)
    order = jnp.argsort(flat_e, stable=True).astype(jnp.int32)
    r = jnp.arange(p, dtype=jnp.int32)
    e_row = jnp.sum(r[:, None] >= end_p[None, :], axis=1).astype(jnp.int32)
    oh_r = (jnp.minimum(e_row, N_EXPERTS - 1)[:, None] == eids).astype(jnp.int32)
    j = r - jnp.sum(oh_r * start_p[None, :], axis=1)
    valid = (e_row < N_EXPERTS) & (j < jnp.sum(oh_r * counts[None, :], axis=1))
    src = jnp.clip(jnp.sum(oh_r * start[None, :], axis=1) + j, 0, n_sel - 1)
    flat_idx = jnp.take(order, src)
    row_token = jnp.where(valid, flat_idx // TOP_K, 0)
    row_gate = jnp.where(valid, jnp.take(flat_g, flat_idx), 0.0)
    tile_start = jnp.arange(p // tm, dtype=jnp.int32) * tm
    tile_expert = jnp.sum(tile_start[:, None] >= end_p[None, :], axis=1).astype(jnp.int32)
    tile_valid = (tile_expert < N_EXPERTS).astype(jnp.int32)
    tile_expert = jnp.minimum(tile_expert, N_EXPERTS - 1)
    return row_token, row_gate[:, None], tile_expert, tile_valid, pos


def kernel(x, c, ctx, c_ctx, ada_w, ada_b, ln_g, ln_b, mix_in_w, pool_w, pool_scale, q_norm, q_up, kv_norm, kv_up, mix_out_w, ffn_gate, ffn_up, ffn_down, hy_in_w, hy_conv_w, hy_conv_b, hy_fw1, hy_fb1, hy_fw2, hy_fb2, hy_fw3, hy_fb3, hy_fout, hy_freq, hy_skip, hy_out_w, router_w, moe_gate, moe_up, moe_down):
    b, n, d = x.shape
    t = b * n
    assert b + 1 <= 8
    s_rows = jnp.concatenate([c, c_ctx[None, :], jnp.zeros((8 - b - 1, d), F32)], axis=0)
    mod = _ada(s_rows, ada_w, ada_b)

    def chunks(l, rows):
        m = mod[l, rows].reshape(-1, 6, d)
        return [m[:, k][:, None, :] for k in range(6)]

    sh1, sc1, g1, sh2, sc2, g2 = chunks(0, slice(0, b))
    shc, scc = chunks(0, slice(b, b + 1))[:2]
    pool_u, q, k, v = _l0a(x, ctx, sc1, sh1, scc, shc, mix_in_w[0], q_norm[0], q_up[0], kv_norm[0], kv_up[0])
    attn_o = _attention(q, k, v, n)
    h1, z1 = _l0b(pool_u, attn_o, x, g1, sc2, sh2, pool_w[0], pool_scale[0], mix_out_w[0], ln_g[0, 0], ln_b[0, 0])
    tm_ffn = 512
    ones_tiles = jnp.ones((t // tm_ffn,), jnp.int32)
    f0 = _ffn_grouped(z1.reshape(t, d), ffn_gate.astype(BF16), ffn_up.astype(BF16), ffn_down.astype(BF16),
                      jnp.ones((t, 1), F32), jnp.zeros((t // tm_ffn,), jnp.int32), ones_tiles, tm_ffn, 1408)
    sh1, sc1, g1b, sh2b, sc2b, g2b = chunks(1, slice(0, b))
    h2, u2 = _resid_ln(h1, [f0.reshape(b, n, d)], g2, ln_g[0, 1], ln_b[0, 1], mod=(sc1, sh1))

    x0, vx = _l1a(u2, hy_in_w[0], hy_conv_w[0], hy_conv_b[0])
    kfull = _hyena_filters(n, hy_fw1[0], hy_fb1[0], hy_fw2[0], hy_fb2[0], hy_fw3[0], hy_fb3[0],
                           hy_fout[0], hy_freq[0])
    yl = _hyena_long_conv(x0, vx, kfull, hy_skip[0])
    h3, z3, logits = _l1c(yl, h2, g1b, sc2b, sh2b, hy_out_w[0], ln_g[1, 0], ln_b[1, 0], router_w[0])

    tm_moe = 512
    row_token, row_gate, tile_expert, tile_valid, pos = _route(logits.reshape(t, LANE)[:, :N_EXPERTS], tm_moe)
    zs = jnp.take(z3.reshape(t, d), row_token, axis=0)
    ys = _ffn_grouped(zs, moe_gate[0].astype(BF16), moe_up[0].astype(BF16), moe_down[0].astype(BF16),
                      row_gate, tile_expert, tile_valid, tm_moe, 896)
    y_a = jnp.take(ys, pos[:, 0], axis=0).reshape(b, n, d)
    y_b = jnp.take(ys, pos[:, 1], axis=0).reshape(b, n, d)
    (h4,) = _resid_ln(h3, [y_a, y_b], g2b, ln_g[1, 1], ln_b[1, 1])
    return h4
```

```python
import functools
import math

import numpy as np
import jax
import jax.numpy as jnp
from jax import lax
from jax.experimental import pallas as pl
from jax.experimental.pallas import tpu as pltpu

F32 = jnp.float32
BF16 = jnp.bfloat16
HIGHEST = lax.Precision.HIGHEST

D_MODEL = 1024
GRID_W = 64
POOL_W = 512
POOL_WINDOWS = (2, 4, 8, 16)
POOL_GC = 128
MLA_HEADS = 8
QK_NOPE = 64
QK_ROPE = 32
V_HEAD = 64
Q_LORA = 256
KV_LORA = 256
ROPE_AXIS = 16
ROPE_BASE = 10000.0
N_EXPERTS = 8
TOP_K = 2
HY_BANDS = 16
HY_EMB = 1 + 2 * HY_BANDS
HY_FAST_DECAY = 0.3
HY_SLOW_DECAY = 1.5
HY_TARGET = 1e-2
HY_MIN_DECAY = math.log(HY_TARGET) / HY_SLOW_DECAY
HY_MAX_DECAY = math.log(HY_TARGET) / HY_FAST_DECAY
LN_EPS = 1e-5
RMS_EPS = 1e-6
DEPTH = 2
ALPHA = (2.0 * DEPTH) ** 0.25

LANE = 128
HEAD_SLOT = 128
DFT_N = 128
DFT_HALF = DFT_N // 2 + 1
VMEM_LIMIT = 56 * 1024 * 1024


def _cparams(*sem):
    return pltpu.CompilerParams(dimension_semantics=sem, vmem_limit_bytes=VMEM_LIMIT)


def _layer_norm(v, g, b):
    mu = jnp.mean(v, axis=-1, keepdims=True)
    c = v - mu
    var = jnp.mean(c * c, axis=-1, keepdims=True)
    return c * lax.rsqrt(var + LN_EPS) * g + b


def _rms_norm(v, g):
    return v * lax.rsqrt(jnp.mean(v * v, axis=-1, keepdims=True) + RMS_EPS) * g


def _ada_kernel(s_ref, w_ref, b_ref, o_ref):
    s = s_ref[...]
    s = s * jax.nn.sigmoid(s)
    o_ref[0] = jnp.dot(s, w_ref[0], preferred_element_type=F32, precision=HIGHEST) + b_ref[0]


def _ada(s_rows, ada_w, ada_b):
    depth, d, n6 = ada_w.shape
    tn = 768
    return pl.pallas_call(
        _ada_kernel,
        out_shape=jax.ShapeDtypeStruct((depth, 8, n6), F32),
        grid=(depth, n6 // tn),
        in_specs=[
            pl.BlockSpec((8, d), lambda l, j: (0, 0)),
            pl.BlockSpec((1, d, tn), lambda l, j: (l, 0, j)),
            pl.BlockSpec((1, 1, tn), lambda l, j: (l, 0, j)),
        ],
        out_specs=pl.BlockSpec((1, 8, tn), lambda l, j: (l, 0, j)),
        compiler_params=_cparams("parallel", "parallel"),
        name="ada_mod",
    )(s_rows, ada_w, ada_b.reshape(depth, 1, n6))


def _l0a_kernel(x_ref, c_ref, sc_ref, sh_ref, scc_ref, shc_ref, win_ref, qn_ref, kvn_ref,
                wqa_ref, wqb_ref, wk_ref, wv_ref, cos_ref, sin_ref,
                pu_ref, q_ref, k_ref, v_ref, u_scr, *, n_lat_tiles):
    i = pl.program_id(1)

    @pl.when(i < n_lat_tiles)
    def _():
        u_scr[...] = (x_ref[0] * (1.0 + sc_ref[0]) + sh_ref[0]).astype(BF16)

    @pl.when(i >= n_lat_tiles)
    def _():
        u_scr[...] = (c_ref[0] * (1.0 + scc_ref[0]) + shc_ref[0]).astype(BF16)

    proj = jnp.dot(u_scr[...], win_ref[...], preferred_element_type=F32)
    pu_ref[0] = proj[:, :POOL_W]
    cos = cos_ref[...]
    sin = sin_ref[...]
    q0 = POOL_W
    kv0 = POOL_W + Q_LORA
    r0 = kv0 + KV_LORA
    qn = _rms_norm(proj[:, q0:kv0], qn_ref[...]).astype(BF16)
    kvn = _rms_norm(proj[:, kv0:r0], kvn_ref[...]).astype(BF16)
    qa = jnp.dot(qn, wqa_ref[...], preferred_element_type=F32)
    qb = jnp.dot(qn, wqb_ref[...], preferred_element_type=F32)
    kn = jnp.dot(kvn, wk_ref[...], preferred_element_type=F32)
    vt = lax.dot_general(wv_ref[...], kvn, (((1,), (1,)), ((), ())), preferred_element_type=F32)
    krot = proj[:, r0:r0 + HEAD_SLOT] * cos + proj[:, r0 + HEAD_SLOT:r0 + 2 * HEAD_SLOT] * sin
    for h in range(MLA_HEADS):
        sl = slice(h * HEAD_SLOT, (h + 1) * HEAD_SLOT)
        q_ref[0, h] = (qa[:, sl] * cos + qb[:, sl] * sin).astype(BF16)
        k_ref[0, h] = (kn[:, sl] + krot).astype(BF16)
        v_ref[0, h] = vt[h * V_HEAD:(h + 1) * V_HEAD].astype(BF16)


def _rope_swap_index():
    half = ROPE_AXIS // 2
    idx = []
    for a in range(2):
        base = a * ROPE_AXIS
        idx += list(range(base + half, base + ROPE_AXIS)) + list(range(base, base + half))
    return np.array(idx)


def _rope_tables(n, n_ctx):
    rows = n // GRID_W
    r = jnp.repeat(jnp.arange(rows, dtype=F32), GRID_W)
    col = jnp.tile(jnp.arange(GRID_W, dtype=F32), rows)
    inv = ROPE_BASE ** (-jnp.arange(0, ROPE_AXIS, 2, dtype=F32) / ROPE_AXIS)
    ang_r = r[:, None] * inv
    ang_c = col[:, None] * inv
    cos32 = jnp.concatenate([jnp.cos(ang_r), jnp.cos(ang_r), jnp.cos(ang_c), jnp.cos(ang_c)], axis=-1)
    sin32 = jnp.concatenate([-jnp.sin(ang_r), jnp.sin(ang_r), -jnp.sin(ang_c), jnp.sin(ang_c)], axis=-1)
    pad = HEAD_SLOT - QK_NOPE - QK_ROPE
    cos = jnp.concatenate([jnp.ones((n, QK_NOPE), F32), cos32, jnp.ones((n, pad), F32)], axis=-1)
    sin = jnp.concatenate([jnp.zeros((n, QK_NOPE), F32), sin32, jnp.zeros((n, pad), F32)], axis=-1)
    cos = jnp.concatenate([cos, jnp.ones((n_ctx, HEAD_SLOT), F32)], axis=0)
    sin = jnp.concatenate([sin, jnp.zeros((n_ctx, HEAD_SLOT), F32)], axis=0)
    return cos, sin


def _l0a_weights(in_w, q_up, kv_up):
    swap = _rope_swap_index()
    d = in_w.shape[0]
    r0 = POOL_W + Q_LORA + KV_LORA
    w_rope = in_w[:, r0:]
    pad_l = jnp.zeros((d, QK_NOPE), F32)
    pad_r = jnp.zeros((d, HEAD_SLOT - QK_NOPE - QK_ROPE), F32)
    kr_a = jnp.concatenate([pad_l, w_rope, pad_r], axis=1)
    kr_b = jnp.concatenate([pad_l, w_rope[:, swap], pad_r], axis=1)
    w_in = jnp.concatenate([in_w[:, :r0], kr_a, kr_b], axis=1).astype(BF16)

    scale = (QK_NOPE + QK_ROPE) ** -0.5 * math.log2(math.e)
    qu = q_up.reshape(Q_LORA, MLA_HEADS, QK_NOPE + QK_ROPE) * scale
    zpad = jnp.zeros((Q_LORA, MLA_HEADS, HEAD_SLOT - QK_NOPE - QK_ROPE), F32)
    wq_a = jnp.concatenate([qu, zpad], axis=-1).reshape(Q_LORA, MLA_HEADS * HEAD_SLOT).astype(BF16)
    wq_b = jnp.concatenate([jnp.zeros((Q_LORA, MLA_HEADS, QK_NOPE), F32), qu[..., QK_NOPE:][..., swap], zpad],
                           axis=-1).reshape(Q_LORA, MLA_HEADS * HEAD_SLOT).astype(BF16)
    kvu = kv_up.reshape(KV_LORA, MLA_HEADS, QK_NOPE + V_HEAD)
    wk = jnp.concatenate([kvu[..., :QK_NOPE], jnp.zeros((KV_LORA, MLA_HEADS, HEAD_SLOT - QK_NOPE), F32)],
                         axis=-1).reshape(KV_LORA, MLA_HEADS * HEAD_SLOT).astype(BF16)
    wv_t = kvu[..., QK_NOPE:].reshape(KV_LORA, MLA_HEADS * V_HEAD).T.astype(BF16)
    return w_in, wq_a, wq_b, wk, wv_t


def _l0a(x, ctx, sc, sh, scc, shc, in_w, q_norm, q_up, kv_norm, kv_up):
    b, n, d = x.shape
    n_ctx = ctx.shape[1]
    tm = 256
    nl = n // tm
    nt = (n + n_ctx) // tm
    w_in, wq_a, wq_b, wk, wv_t = _l0a_weights(in_w, q_up, kv_up)
    cos, sin = _rope_tables(n, n_ctx)
    hw = MLA_HEADS * HEAD_SLOT
    full = lambda shape: pl.BlockSpec(shape, lambda bi, i: (0,) * len(shape))
    vec = pl.BlockSpec((1, 1, d), lambda bi, i: (bi, 0, 0))
    return pl.pallas_call(
        functools.partial(_l0a_kernel, n_lat_tiles=nl),
        out_shape=[
            jax.ShapeDtypeStruct((b, n + n_ctx, POOL_W), F32),
            jax.ShapeDtypeStruct((b, MLA_HEADS, n + n_ctx, HEAD_SLOT), BF16),
            jax.ShapeDtypeStruct((b, MLA_HEADS, n + n_ctx, HEAD_SLOT), BF16),
            jax.ShapeDtypeStruct((b, MLA_HEADS, V_HEAD, n + n_ctx), BF16),
        ],
        grid=(b, nt),
        in_specs=[
            pl.BlockSpec((1, tm, d), lambda bi, i: (bi, jnp.minimum(i, nl - 1), 0)),
            pl.BlockSpec((1, tm, d), lambda bi, i: (bi, jnp.maximum(i - nl, 0), 0)),
            vec, vec, full((1, 1, d)), full((1, 1, d)),
            full(w_in.shape), full((1, Q_LORA)), full((1, KV_LORA)),
            full((Q_LORA, hw)), full((Q_LORA, hw)), full((KV_LORA, hw)), full((MLA_HEADS * V_HEAD, KV_LORA)),
            pl.BlockSpec((tm, HEAD_SLOT), lambda bi, i: (i, 0)),
            pl.BlockSpec((tm, HEAD_SLOT), lambda bi, i: (i, 0)),
        ],
        out_specs=[
            pl.BlockSpec((1, tm, POOL_W), lambda bi, i: (bi, i, 0)),
            pl.BlockSpec((1, MLA_HEADS, tm, HEAD_SLOT), lambda bi, i: (bi, 0, i, 0)),
            pl.BlockSpec((1, MLA_HEADS, tm, HEAD_SLOT), lambda bi, i: (bi, 0, i, 0)),
            pl.BlockSpec((1, MLA_HEADS, V_HEAD, tm), lambda bi, i: (bi, 0, 0, i)),
        ],
        scratch_shapes=[pltpu.VMEM((tm, d), BF16)],
        compiler_params=_cparams("parallel", "arbitrary"),
        name="l0_in_proj",
    )(x, ctx, sc, sh, scc, shc, w_in, q_norm.reshape(1, -1), kv_norm.reshape(1, -1),
      wq_a, wq_b, wk, wv_t, cos, sin)


def _attn_kernel(q_ref, k_ref, v_ref, o_ref, m_ref, l_ref, acc_ref, *, nk):
    ki = pl.program_id(2)

    @pl.when(ki == 0)
    def _():
        m_ref[...] = jnp.full(m_ref.shape, -jnp.inf, F32)
        l_ref[...] = jnp.zeros(l_ref.shape, F32)
        acc_ref[...] = jnp.zeros(acc_ref.shape, F32)

    for h in range(MLA_HEADS):
        st = lax.dot_general(k_ref[0, h], q_ref[0, h], (((1,), (1,)), ((), ())),
                             preferred_element_type=F32)
        m_prev = m_ref[h]
        m_new = jnp.maximum(m_prev, jnp.max(st, axis=0, keepdims=True))
        a = jnp.exp2(m_prev - m_new)
        p = jnp.exp2(st - m_new)
        l_ref[h] = a * l_ref[h] + jnp.sum(p, axis=0, keepdims=True)
        pv = jnp.dot(v_ref[0, h], p.astype(BF16), preferred_element_type=F32)
        rows = slice(h * V_HEAD, (h + 1) * V_HEAD)
        acc_ref[rows, :] = a * acc_ref[rows, :] + pv
        m_ref[h] = m_new

    @pl.when(ki == nk - 1)
    def _():
        for h in range(MLA_HEADS):
            rows = slice(h * V_HEAD, (h + 1) * V_HEAD)
            acc_ref[rows, :] = acc_ref[rows, :] / l_ref[h]
        o_ref[0] = acc_ref[...].T.astype(o_ref.dtype)


def _attention(q, k, vt, n):
    b, h, n_all, _ = k.shape
    tq, tk = 512, 768
    nk = n_all // tk
    return pl.pallas_call(
        functools.partial(_attn_kernel, nk=nk),
        out_shape=jax.ShapeDtypeStruct((b, n, h * V_HEAD), BF16),
        grid=(b, n // tq, nk),
        in_specs=[
            pl.BlockSpec((1, h, tq, HEAD_SLOT), lambda bi, qi, ki: (bi, 0, qi, 0)),
            pl.BlockSpec((1, h, tk, HEAD_SLOT), lambda bi, qi, ki: (bi, 0, ki, 0)),
            pl.BlockSpec((1, h, V_HEAD, tk), lambda bi, qi, ki: (bi, 0, 0, ki)),
        ],
        out_specs=pl.BlockSpec((1, tq, h * V_HEAD), lambda bi, qi, ki: (bi, qi, 0)),
        scratch_shapes=[
            pltpu.VMEM((h, 1, tq), F32),
            pltpu.VMEM((h, 1, tq), F32),
            pltpu.VMEM((h * V_HEAD, tq), F32),
        ],
        compiler_params=_cparams("parallel", "parallel", "arbitrary"),
        name="mla_attention",
    )(q, k, vt)


def _l0b_kernel(pp_ref, pc_ref, pn_ref, o_ref, x_ref, g1_ref, sc2_ref, sh2_ref, pw_ref, ps_ref, ow_ref,
                lg_ref, lb_ref, h_ref, z_ref, *, tm, n):
    i = pl.program_id(1)
    halo = 8
    ext = jnp.concatenate([pp_ref[0], pc_ref[0], pn_ref[0]], axis=0)
    pos = i * tm - halo + lax.broadcasted_iota(jnp.int32, (tm + 2 * halo, 1), 0)
    ext = jnp.where((pos >= 0) & (pos < n), ext, 0.0)
    t = i * tm + lax.broadcasted_iota(jnp.int32, (tm, 1), 0)
    y = jnp.dot(o_ref[0], ow_ref[POOL_W:, :], preferred_element_type=F32)
    for g, w in enumerate(POOL_WINDOWS):
        hw = w // 2
        e = ext[:, g * POOL_GC:(g + 1) * POOL_GC]
        s = e
        width = 1
        while width < w:
            s = s[:s.shape[0] - width] + s[width:]
            width *= 2
        win = s[halo - hw:halo - hw + tm]
        cnt = (jnp.minimum(t + hw, n) - jnp.maximum(t - hw, 0)).astype(F32)
        dd = (win / cnt - e[halo:halo + tm]).astype(BF16)
        yg = jnp.dot(dd, pw_ref[g], preferred_element_type=F32) * ps_ref[:, g * POOL_GC:(g + 1) * POOL_GC]
        y = y + jnp.dot(yg.astype(BF16), ow_ref[g * POOL_GC:(g + 1) * POOL_GC, :], preferred_element_type=F32)
    hh = _layer_norm(ALPHA * x_ref[0] + g1_ref[0] * y, lg_ref[...], lb_ref[...])
    h_ref[0] = hh
    z_ref[0] = (hh * (1.0 + sc2_ref[0]) + sh2_ref[0]).astype(BF16)


def _l0b(pool_u, attn_o, x, g1, sc2, sh2, pool_w, pool_scale, out_w, ln_g, ln_b):
    b, n, d = x.shape
    tm = 512
    hb = tm // 8
    vec = pl.BlockSpec((1, 1, d), lambda bi, i: (bi, 0, 0))
    full = lambda shape: pl.BlockSpec(shape, lambda bi, i: (0,) * len(shape))
    return pl.pallas_call(
        functools.partial(_l0b_kernel, tm=tm, n=n),
        out_shape=[jax.ShapeDtypeStruct((b, n, d), F32), jax.ShapeDtypeStruct((b, n, d), BF16)],
        grid=(b, n // tm),
        in_specs=[
            pl.BlockSpec((1, 8, POOL_W), lambda bi, i: (bi, jnp.maximum(i * hb - 1, 0), 0)),
            pl.BlockSpec((1, tm, POOL_W), lambda bi, i: (bi, i, 0)),
            pl.BlockSpec((1, 8, POOL_W), lambda bi, i: (bi, (i + 1) * hb, 0)),
            pl.BlockSpec((1, tm, POOL_W), lambda bi, i: (bi, i, 0)),
            pl.BlockSpec((1, tm, d), lambda bi, i: (bi, i, 0)),
            vec, vec, vec,
            full(pool_w.shape), full((1, POOL_W)), full(out_w.shape), full((1, d)), full((1, d)),
        ],
        out_specs=[pl.BlockSpec((1, tm, d), lambda bi, i: (bi, i, 0)),
                   pl.BlockSpec((1, tm, d), lambda bi, i: (bi, i, 0))],
        compiler_params=_cparams("parallel", "parallel"),
        name="l0_pool_out_ln",
    )(pool_u, pool_u, pool_u, attn_o, x, g1, sc2, sh2, pool_w.astype(BF16), pool_scale.reshape(1, -1),
      out_w.astype(BF16), ln_g.reshape(1, -1), ln_b.reshape(1, -1))


def _ffn_kernel(te_ref, tv_ref, z_ref, wg_ref, wu_ref, wd_ref, gate_ref, *rest, nj):
    o_ref, acc_ref = rest[-2:]
    i = pl.program_id(0)
    j = pl.program_id(1)

    @pl.when(j == 0)
    def _():
        acc_ref[...] = jnp.zeros(acc_ref.shape, F32)

    @pl.when(tv_ref[i] > 0)
    def _():
        z = z_ref[...].astype(BF16)
        g = jnp.dot(z, wg_ref[0].astype(BF16), preferred_element_type=F32)
        u = jnp.dot(z, wu_ref[0].astype(BF16), preferred_element_type=F32)
        a = (g * jax.nn.sigmoid(g) * u).astype(BF16)
        acc_ref[...] += jnp.dot(a, wd_ref[0].astype(BF16), preferred_element_type=F32)

    @pl.when(j == nj - 1)
    def _():
        o_ref[...] = acc_ref[...] * gate_ref[...]


def _ffn_grouped(z, wg, wu, wd, gate, tile_expert, tile_valid, tm, tf, out_rows=None, tile0=0, prev=None):
    p, d = z.shape
    out_rows = p if out_rows is None else out_rows
    f = wg.shape[-1]
    nj = f // tf
    in_specs = [
        pl.BlockSpec((tm, d), lambda i, j, te, tv: (i, 0)),
        pl.BlockSpec((1, d, tf), lambda i, j, te, tv: (te[i], 0, j)),
        pl.BlockSpec((1, d, tf), lambda i, j, te, tv: (te[i], 0, j)),
        pl.BlockSpec((1, tf, d), lambda i, j, te, tv: (te[i], j, 0)),
        pl.BlockSpec((tm, 1), lambda i, j, te, tv: (i, 0)),
    ]
    args = [tile_expert, tile_valid, z, wg, wu, wd, gate]
    aliases = {}
    if prev is not None:
        in_specs.append(pl.BlockSpec(memory_space=pl.ANY))
        aliases = {len(args): 0}
        args.append(prev)
    return pl.pallas_call(
        functools.partial(_ffn_kernel, nj=nj),
        out_shape=jax.ShapeDtypeStruct((out_rows, d), F32),
        grid_spec=pltpu.PrefetchScalarGridSpec(
            num_scalar_prefetch=2,
            grid=(p // tm, nj),
            in_specs=in_specs,
            out_specs=pl.BlockSpec((tm, d), lambda i, j, te, tv: (tile0 + i, 0)),
            scratch_shapes=[pltpu.VMEM((tm, d), F32)],
        ),
        input_output_aliases=aliases,
        compiler_params=_cparams("parallel", "arbitrary"),
        name="swiglu_grouped",
    )(*args)


def _resid_ln_kernel(*refs, n_y, with_mod):
    h_ref = refs[0]
    y_refs = refs[1:1 + n_y]
    g_ref, lg_ref, lb_ref = refs[1 + n_y:4 + n_y]
    rest = refs[4 + n_y:]
    y = y_refs[0][0]
    for r in y_refs[1:]:
        y = y + r[0]
    hh = _layer_norm(ALPHA * h_ref[0] + g_ref[0] * y, lg_ref[...], lb_ref[...])
    if with_mod:
        sc_ref, sh_ref, ho_ref, u_ref = rest
        ho_ref[0] = hh
        u_ref[0] = (hh * (1.0 + sc_ref[0]) + sh_ref[0]).astype(BF16)
    else:
        rest[0][0] = hh


def _resid_ln(h, ys, g, ln_g, ln_b, mod=None):
    b, n, d = h.shape
    tm = 1024
    blk = pl.BlockSpec((1, tm, d), lambda bi, i: (bi, i, 0))
    vec = pl.BlockSpec((1, 1, d), lambda bi, i: (bi, 0, 0))
    row = pl.BlockSpec((1, d), lambda bi, i: (0, 0))
    in_specs = [blk] + [blk] * len(ys) + [vec, row, row]
    args = [h, *ys, g, ln_g.reshape(1, -1), ln_b.reshape(1, -1)]
    out_shape = [jax.ShapeDtypeStruct((b, n, d), F32)]
    out_specs = [blk]
    if mod is not None:
        in_specs += [vec, vec]
        args += list(mod)
        out_shape.append(jax.ShapeDtypeStruct((b, n, d), BF16))
        out_specs.append(blk)
    return pl.pallas_call(
        functools.partial(_resid_ln_kernel, n_y=len(ys), with_mod=mod is not None),
        out_shape=out_shape,
        grid=(b, n // tm),
        in_specs=in_specs,
        out_specs=out_specs,
        compiler_params=_cparams("parallel", "parallel"),
        name="resid_ln",
    )(*args)


def _l1a_kernel(up_ref, uc_ref, un_ref, w_ref, cw_ref, cb_ref, x0_ref, vx_ref, *, tm, n):
    i = pl.program_id(1)
    halo = 16
    ext = jnp.concatenate([up_ref[0], uc_ref[0], un_ref[0]], axis=0)
    pos = i * tm - halo + lax.broadcasted_iota(jnp.int32, (tm + 2 * halo, 1), 0)
    ext = jnp.where((pos >= 0) & (pos < n), ext, jnp.zeros_like(ext))
    z = jnp.dot(ext, w_ref[...], preferred_element_type=F32)
    cw = cw_ref[...]
    zc = (cw[0:1] * z[halo - 1:halo - 1 + tm] + cw[1:2] * z[halo:halo + tm]
          + cw[2:3] * z[halo + 1:halo + 1 + tm] + cb_ref[...])
    c = zc.shape[1] // 3
    x0_ref[0] = zc[:, :c]
    vx_ref[0] = zc[:, 2 * c:] * zc[:, c:2 * c]


def _l1a(u, hy_in_w, conv_w, conv_b):
    b, n, d = u.shape
    c3 = hy_in_w.shape[1]
    c = c3 // 3
    tm = 512
    hb = tm // 16
    nb16 = n // 16
    full = lambda shape: pl.BlockSpec(shape, lambda bi, i: (0,) * len(shape))
    return pl.pallas_call(
        functools.partial(_l1a_kernel, tm=tm, n=n),
        out_shape=[jax.ShapeDtypeStruct((b, n, c), F32), jax.ShapeDtypeStruct((b, n, c), F32)],
        grid=(b, n // tm),
        in_specs=[
            pl.BlockSpec((1, 16, d), lambda bi, i: (bi, jnp.maximum(i * hb - 1, 0), 0)),
            pl.BlockSpec((1, tm, d), lambda bi, i: (bi, i, 0)),
            pl.BlockSpec((1, 16, d), lambda bi, i: (bi, jnp.minimum((i + 1) * hb, nb16 - 1), 0)),
            full((d, c3)), full((3, c3)), full((1, c3)),
        ],
        out_specs=[pl.BlockSpec((1, tm, c), lambda bi, i: (bi, i, 0)),
                   pl.BlockSpec((1, tm, c), lambda bi, i: (bi, i, 0))],
        compiler_params=_cparams("parallel", "parallel"),
        name="hyena_in_conv",
    )(u, u, u, hy_in_w.astype(BF16), conv_w, conv_b.reshape(1, -1))


def _filter_kernel(z_ref, w1_ref, b1_ref, w2_ref, b2_ref, w3_ref, b3_ref, wo_ref, fr_ref, dl_ref, k_ref, *, tm, n):
    z = z_ref[...]
    fr = fr_ref[...]
    dot = functools.partial(jnp.dot, preferred_element_type=F32, precision=HIGHEST)
    h = jnp.sin(fr * (dot(z, w1_ref[...]) + b1_ref[...]))
    h = jnp.sin(fr * (dot(h, w2_ref[...]) + b2_ref[...]))
    h = jnp.sin(fr * (dot(h, w3_ref[...]) + b3_ref[...]))
    o = dot(h, wo_ref[...]) * jnp.exp(-z[:, 0:1] * dl_ref[...])
    row = pl.program_id(0) * tm + lax.broadcasted_iota(jnp.int32, (tm, 1), 0)
    k_ref[...] = jnp.where(row == n, 0.0, o)


def _hyena_filters(n, fw1, fb1, fw2, fb2, fw3, fb3, fout, freq):
    c = fout.shape[1] // 2
    m = jnp.arange(n, dtype=F32)
    p = jnp.concatenate([m, n - m])[:, None]
    t = p / (n - 1.0)
    w_ang = (2.0 * math.pi / n) * p
    bands = jnp.linspace(1e-4, HY_BANDS - 1, HY_BANDS, dtype=F32)[None, :]
    z = jnp.concatenate([t, jnp.cos(bands * w_ang), -jnp.sin(bands * w_ang),
                         jnp.zeros((2 * n, LANE - HY_EMB), F32)], axis=-1)
    deltas = jnp.abs(jnp.linspace(HY_MIN_DECAY, HY_MAX_DECAY, c, dtype=F32))[None, :]
    padc = lambda a: jnp.pad(a, ((0, 0), (0, LANE - a.shape[1])))
    padr = lambda a: jnp.pad(a, ((0, LANE - a.shape[0]), (0, 0)))
    row = lambda a: padc(a.reshape(1, -1))
    tm = 1024
    half = n // tm
    full = lambda shape: pl.BlockSpec(shape, lambda i: (0,) * len(shape))
    return pl.pallas_call(
        functools.partial(_filter_kernel, tm=tm, n=n),
        out_shape=jax.ShapeDtypeStruct((2 * n, c), F32),
        grid=(2 * half,),
        in_specs=[pl.BlockSpec((tm, LANE), lambda i: (i, 0)),
                  full((LANE, LANE)), full((1, LANE)), full((LANE, LANE)), full((1, LANE)),
                  full((LANE, LANE)), full((1, LANE)), pl.BlockSpec((LANE, c), lambda i: (0, i // half)),
                  full((1, LANE)), full((1, c))],
        out_specs=pl.BlockSpec((tm, c), lambda i: (i, 0)),
        compiler_params=_cparams("parallel"),
        name="hyena_filters",
    )(z, padc(padr(fw1)), row(fb1), padc(padr(fw2)), row(fb2), padc(padr(fw3)), row(fb3),
      padr(fout), row(freq), deltas)


def _dft_consts():
    k = np.arange(DFT_HALF)[:, None]
    n1 = np.arange(DFT_N)[None, :]
    th = 2.0 * np.pi * k * n1 / DFT_N
    fa_r, fa_i = np.cos(th), -np.sin(th)
    kk = np.arange(DFT_N)[:, None] * np.arange(DFT_N)[None, :]
    c, s = np.cos(2.0 * np.pi * kk / DFT_N), np.sin(2.0 * np.pi * kk / DFT_N)
    g_fwd = np.block([[c, s], [-s, c]])
    g_inv = np.block([[c, -s], [s, c]])
    ph = 2.0 * np.pi * np.arange(DFT_HALF)[:, None] * np.arange(DFT_N)[None, :] / (DFT_N * DFT_N)
    tw_c, tw_s = np.cos(ph)[:, :, None], np.sin(ph)[:, :, None]
    n_out = np.arange(DFT_N // 2)[:, None]
    ps = 2.0 * np.pi * n_out * np.arange(DFT_HALF)[None, :] / DFT_N
    wgt = np.full((1, DFT_HALF), 2.0)
    wgt[0, 0] = wgt[0, -1] = 1.0
    length = DFT_N * DFT_N
    fo_r = wgt * np.cos(ps) / length
    fo_i = -wgt * np.sin(ps) / length
    return dict(fa_r=fa_r, fa_i=fa_i, g_fwd=g_fwd, g_inv=g_inv, tw_c=tw_c, tw_s=tw_s, fo_r=fo_r, fo_i=fo_i)


DFT_ROW_PAD = 72
DFT_NB = 16


def _dft_a_kernel(f_ref, x_ref, or_ref, oi_ref):
    f = f_ref[...]
    for j in range(x_ref.shape[2]):
        y = jnp.dot(f, x_ref[0, :, j, :].astype(BF16), preferred_element_type=F32)
        or_ref[0, :, j, :] = y[:DFT_HALF].astype(or_ref.dtype)
        oi_ref[0, :, j, :] = y[DFT_ROW_PAD:DFT_ROW_PAD + DFT_HALF].astype(oi_ref.dtype)


def _dft_a(f_stack, x, out_dtype):
    b, k, n2, c = x.shape
    blk = lambda rows: pl.BlockSpec((1, rows, DFT_NB, c), lambda bi, j: (bi, 0, j, 0))
    return pl.pallas_call(
        _dft_a_kernel,
        out_shape=[jax.ShapeDtypeStruct((b, DFT_HALF, n2, c), out_dtype)] * 2,
        grid=(b, n2 // DFT_NB),
        in_specs=[pl.BlockSpec(f_stack.shape, lambda bi, j: (0, 0)), blk(k)],
        out_specs=[blk(DFT_HALF)] * 2,
        compiler_params=_cparams("parallel", "parallel"),
        name="dft_stage_a",
    )(f_stack, x)


def _spectrum_kernel(ar_ref, ai_ref, tc_ref, ts_ref, g_ref, hr_ref, hi_ref):
    tc = tc_ref[0]
    ts = ts_ref[0]
    ar = ar_ref[0, 0]
    ai = ai_ref[0, 0]
    a2 = jnp.concatenate([ar * tc + ai * ts, ai * tc - ar * ts], axis=0).astype(BF16)
    x = jnp.dot(g_ref[...], a2, preferred_element_type=F32)
    hr_ref[0] = x[:DFT_N]
    hi_ref[0] = x[DFT_N:]


def _filter_spectrum(kr, ki, consts):
    c = kr.shape[-1]
    plane = pl.BlockSpec((1, 1, DFT_N, c), lambda k1: (0, k1, 0, 0))
    tw = pl.BlockSpec((1, DFT_N, 1), lambda k1: (k1, 0, 0))
    out = pl.BlockSpec((1, DFT_N, c), lambda k1: (k1, 0, 0))
    return pl.pallas_call(
        _spectrum_kernel,
        out_shape=[jax.ShapeDtypeStruct((DFT_HALF, DFT_N, c), F32)] * 2,
        grid=(DFT_HALF,),
        in_specs=[plane, plane, tw, tw, pl.BlockSpec((2 * DFT_N, 2 * DFT_N), lambda k1: (0, 0))],
        out_specs=[out, out],
        compiler_params=_cparams("parallel"),
        name="filter_spectrum",
    )(kr, ki, consts["tw_c"], consts["tw_s"], consts["g_fwd"])


def _fft_mid_kernel(ar_ref, ai_ref, hr_ref, hi_ref, tc_ref, ts_ref, gf_ref, gi_ref, br_ref, bi_ref):
    tc = tc_ref[0]
    ts = ts_ref[0]
    hr = hr_ref[0]
    hi = hi_ref[0]
    for b in range(ar_ref.shape[0]):
        ar = ar_ref[b, 0].astype(F32)
        ai = ai_ref[b, 0].astype(F32)
        a2 = jnp.concatenate([ar * tc + ai * ts, ai * tc - ar * ts], axis=0).astype(BF16)
        x = jnp.dot(gf_ref[...], a2, preferred_element_type=F32)
        xr, xi = x[:DFT_N], x[DFT_N:]
        z = jnp.concatenate([xr * hr - xi * hi, xr * hi + xi * hr], axis=0).astype(BF16)
        y = jnp.dot(gi_ref[...], z, preferred_element_type=F32)
        yr, yi = y[:DFT_N], y[DFT_N:]
        br_ref[b, 0] = (yr * tc - yi * ts).astype(br_ref.dtype)
        bi_ref[b, 0] = (yr * ts + yi * tc).astype(bi_ref.dtype)


def _fft_mid(ar, ai, hr, hi, consts):
    b, _, _, c = ar.shape
    plane = pl.BlockSpec((b, 1, DFT_N, c), lambda k1: (0, k1, 0, 0))
    hspec = pl.BlockSpec((1, DFT_N, c), lambda k1: (k1, 0, 0))
    tw = pl.BlockSpec((1, DFT_N, 1), lambda k1: (k1, 0, 0))
    gspec = pl.BlockSpec((2 * DFT_N, 2 * DFT_N), lambda k1: (0, 0))
    return pl.pallas_call(
        _fft_mid_kernel,
        out_shape=[jax.ShapeDtypeStruct(ar.shape, BF16)] * 2,
        grid=(DFT_HALF,),
        in_specs=[plane, plane, hspec, hspec, tw, tw, gspec, gspec],
        out_specs=[plane, plane],
        compiler_params=_cparams("parallel"),
        name="fft_mid",
    )(ar, ai, hr, hi, consts["tw_c"], consts["tw_s"], consts["g_fwd"], consts["g_inv"])


def _idft_gate_kernel(fr_ref, fi_ref, br_ref, bi_ref, x0_ref, vx_ref, sk_ref, o_ref):
    fr = fr_ref[...]
    fi = fi_ref[...]
    sk = sk_ref[...]
    for j in range(br_ref.shape[2]):
        y = (jnp.dot(fr, br_ref[0, :, j, :], preferred_element_type=F32)
             + jnp.dot(fi, bi_ref[0, :, j, :], preferred_element_type=F32))
        o_ref[0, :, j, :] = (x0_ref[0, :, j, :] * (y + vx_ref[0, :, j, :] * sk)).astype(o_ref.dtype)


def _idft_gate(fo_r, fo_i, br, bi, x0, vx, skip_row):
    b, r, n2, c = br.shape
    m = fo_r.shape[0]
    fspec = pl.BlockSpec((m, r), lambda bi_, j: (0, 0))
    blk = lambda rows: pl.BlockSpec((1, rows, DFT_NB, c), lambda bi_, j: (bi_, 0, j, 0))
    return pl.pallas_call(
        _idft_gate_kernel,
        out_shape=jax.ShapeDtypeStruct((b, m, n2, c), BF16),
        grid=(b, n2 // DFT_NB),
        in_specs=[fspec, fspec, blk(r), blk(r), blk(m), blk(m), pl.BlockSpec((1, c), lambda bi_, j: (0, 0))],
        out_specs=blk(m),
        compiler_params=_cparams("parallel", "parallel"),
        name="idft_gate",
    )(fo_r, fo_i, br, bi, x0, vx, skip_row)


def _stack_rows(fr, fi):
    pad = np.zeros((DFT_ROW_PAD - DFT_HALF, fr.shape[1]))
    return jnp.asarray(np.concatenate([fr, pad, fi, pad], axis=0), BF16)


def _hyena_long_conv(x0, vx, kfull, skip):
    b, n, c = vx.shape
    n1 = n // DFT_N
    cn = _dft_consts()
    bf = lambda a: jnp.asarray(a, BF16)
    f32 = lambda a: jnp.asarray(a, F32)
    consts = dict(tw_c=f32(cn["tw_c"]), tw_s=f32(cn["tw_s"]), g_fwd=bf(cn["g_fwd"]), g_inv=bf(cn["g_inv"]))
    kr, ki = _dft_a(_stack_rows(cn["fa_r"], cn["fa_i"]), kfull.reshape(1, DFT_N, DFT_N, c), F32)
    hr, hi = _filter_spectrum(kr, ki, consts)
    vx4 = vx.reshape(b, n1, DFT_N, c)
    ar, ai = _dft_a(_stack_rows(cn["fa_r"][:, :n1], cn["fa_i"][:, :n1]), vx4, BF16)
    br, bi = _fft_mid(ar, ai, hr, hi, consts)
    y = _idft_gate(bf(cn["fo_r"]), bf(cn["fo_i"]), br, bi, x0.reshape(b, n1, DFT_N, c), vx4, skip.reshape(1, c))
    return y.reshape(b, n, c)


def _l1c_kernel(y_ref, h_ref, g1_ref, sc2_ref, sh2_ref, w_ref, lg_ref, lb_ref, rw_ref, ho_ref, z_ref, lo_ref):
    y = jnp.dot(y_ref[0], w_ref[...], preferred_element_type=F32)
    hh = _layer_norm(ALPHA * h_ref[0] + g1_ref[0] * y, lg_ref[...], lb_ref[...])
    ho_ref[0] = hh
    z = hh * (1.0 + sc2_ref[0]) + sh2_ref[0]
    z_ref[0] = z
    lo_ref[0] = jnp.dot(z, rw_ref[...], preferred_element_type=F32, precision=HIGHEST)


def _l1c(y, h, g1, sc2, sh2, out_w, ln_g, ln_b, router_w):
    b, n, d = h.shape
    tm = 512
    blk = pl.BlockSpec((1, tm, d), lambda bi, i: (bi, i, 0))
    vec = pl.BlockSpec((1, 1, d), lambda bi, i: (bi, 0, 0))
    full = lambda shape: pl.BlockSpec(shape, lambda bi, i: (0,) * len(shape))
    rw = jnp.pad(router_w, ((0, 0), (0, LANE - router_w.shape[1])))
    return pl.pallas_call(
        _l1c_kernel,
        out_shape=[jax.ShapeDtypeStruct((b, n, d), F32), jax.ShapeDtypeStruct((b, n, d), F32),
                   jax.ShapeDtypeStruct((b, n, LANE), F32)],
        grid=(b, n // tm),
        in_specs=[blk, blk, vec, vec, vec, full((d, d)), full((1, d)), full((1, d)), full((d, LANE))],
        out_specs=[blk, blk, pl.BlockSpec((1, tm, LANE), lambda bi, i: (bi, i, 0))],
        compiler_params=_cparams("parallel", "parallel"),
        name="hyena_out_ln_router",
    )(y, h, g1, sc2, sh2, out_w.astype(BF16), ln_g.reshape(1, -1), ln_b.reshape(1, -1), rw)


def _take(a, idx):
    return a.at[idx].get(mode="promise_in_bounds")


def _route(logits, tm):
    t = logits.shape[0]
    top_v, top_i = lax.top_k(logits, TOP_K)
    gates = jax.nn.softmax(top_v, axis=-1)
    flat_e = top_i.reshape(-1).astype(jnp.int32)
    flat_g = gates.reshape(-1)
    n_sel = t * TOP_K
    p = n_sel + N_EXPERTS * tm
    eids = jnp.arange(N_EXPERTS, dtype=jnp.int32)[None, :]
    onehot = (flat_e[:, None] == eids).astype(jnp.int32)
    csum = jnp.cumsum(onehot, axis=0)
    counts = csum[-1]
    rank = jnp.sum((csum - onehot) * onehot, axis=1)
    padded = ((counts + tm - 1) // tm) * tm
    end_p = jnp.cumsum(padded)
    start_p = end_p - padded
    start = jnp.cumsum(counts) - counts
    pos = (jnp.sum(onehot * start_p[None, :], axis=1) + rank).reshape(t, TOP_K)
    order = jnp.argsort(flat_e, stable=True).astype(jnp.int32)
    r = jnp.arange(p, dtype=jnp.int32)
    e_row = jnp.sum(r[:, None] >= end_p[None, :], axis=1).astype(jnp.int32)
    oh_r = (jnp.minimum(e_row, N_EXPERTS - 1)[:, None] == eids).astype(jnp.int32)
    j = r - jnp.sum(oh_r * start_p[None, :], axis=1)
    valid = (e_row < N_EXPERTS) & (j < jnp.sum(oh_r * counts[None, :], axis=1))
    src = jnp.clip(jnp.sum(oh_r * start[None, :], axis=1) + j, 0, n_sel - 1)
    flat_idx = _take(order, src)
    row_token = jnp.where(valid, flat_idx // TOP_K, 0)
    row_gate = jnp.where(valid, _take(flat_g, flat_idx), 0.0)
    tile_start = jnp.arange(p // tm, dtype=jnp.int32) * tm
    tile_expert = jnp.sum(tile_start[:, None] >= end_p[None, :], axis=1).astype(jnp.int32)
    tile_valid = (tile_expert < N_EXPERTS).astype(jnp.int32)
    tile_expert = jnp.minimum(tile_expert, N_EXPERTS - 1)
    return row_token, row_gate[:, None], tile_expert, tile_valid, pos


def kernel(x, c, ctx, c_ctx, ada_w, ada_b, ln_g, ln_b, mix_in_w, pool_w, pool_scale, q_norm, q_up, kv_norm, kv_up, mix_out_w, ffn_gate, ffn_up, ffn_down, hy_in_w, hy_conv_w, hy_conv_b, hy_fw1, hy_fb1, hy_fw2, hy_fb2, hy_fw3, hy_fb3, hy_fout, hy_freq, hy_skip, hy_out_w, router_w, moe_gate, moe_up, moe_down):
    b, n, d = x.shape
    t = b * n
    assert b + 1 <= 8
    s_rows = jnp.concatenate([c, c_ctx[None, :], jnp.zeros((8 - b - 1, d), F32)], axis=0)
    mod = _ada(s_rows, ada_w, ada_b)

    def chunks(l, rows):
        m = mod[l, rows].reshape(-1, 6, d)
        return [m[:, k][:, None, :] for k in range(6)]

    sh1, sc1, g1, sh2, sc2, g2 = chunks(0, slice(0, b))
    shc, scc = chunks(0, slice(b, b + 1))[:2]
    pool_u, q, k, v = _l0a(x, ctx, sc1, sh1, scc, shc, mix_in_w[0], q_norm[0], q_up[0], kv_norm[0], kv_up[0])
    attn_o = _attention(q, k, v, n)
    h1, z1 = _l0b(pool_u, attn_o, x, g1, sc2, sh2, pool_w[0], pool_scale[0], mix_out_w[0], ln_g[0, 0], ln_b[0, 0])
    tm_ffn = 512
    ones_tiles = jnp.ones((t // tm_ffn,), jnp.int32)
    f0 = _ffn_grouped(z1.reshape(t, d), ffn_gate.astype(BF16), ffn_up.astype(BF16), ffn_down.astype(BF16),
                      jnp.ones((t, 1), F32), jnp.zeros((t // tm_ffn,), jnp.int32), ones_tiles, tm_ffn, 1408)
    sh1, sc1, g1b, sh2b, sc2b, g2b = chunks(1, slice(0, b))
    h2, u2 = _resid_ln(h1, [f0.reshape(b, n, d)], g2, ln_g[0, 1], ln_b[0, 1], mod=(sc1, sh1))

    x0, vx = _l1a(u2, hy_in_w[0], hy_conv_w[0], hy_conv_b[0])
    kfull = _hyena_filters(n, hy_fw1[0], hy_fb1[0], hy_fw2[0], hy_fb2[0], hy_fw3[0], hy_fb3[0],
                           hy_fout[0], hy_freq[0])
    yl = _hyena_long_conv(x0, vx, kfull, hy_skip[0])
    h3, z3, logits = _l1c(yl, h2, g1b, sc2b, sh2b, hy_out_w[0], ln_g[1, 0], ln_b[1, 0], router_w[0])

    tm_moe = 1024
    row_token, row_gate, tile_expert, tile_valid, pos = _route(logits.reshape(t, LANE)[:, :N_EXPERTS], tm_moe)
    p_rows = row_token.shape[0]
    n_tiles = p_rows // tm_moe
    half_t = n_tiles // 2
    ys = None
    for t0, t1 in ((0, half_t), (half_t, n_tiles)):
        r0, r1 = t0 * tm_moe, t1 * tm_moe
        zs = _take(z3.reshape(t, d), row_token[r0:r1])
        ys = _ffn_grouped(zs, moe_gate[0], moe_up[0], moe_down[0], row_gate[r0:r1], tile_expert[t0:t1],
                          tile_valid[t0:t1], tm_moe, 512, out_rows=p_rows, tile0=t0, prev=ys)
    y_a = _take(ys, pos[:, 0]).reshape(b, n, d)
    y_b = _take(ys, pos[:, 1]).reshape(b, n, d)
    (h4,) = _resid_ln(h3, [y_a, y_b], g2b, ln_g[1, 1], ln_b[1, 1])
    return h4
```

```python
import functools
import math

import numpy as np
import jax
import jax.numpy as jnp
from jax import lax
from jax.experimental import pallas as pl
from jax.experimental.pallas import tpu as pltpu

F32 = jnp.float32
BF16 = jnp.bfloat16
HIGHEST = lax.Precision.HIGHEST

D_MODEL = 1024
GRID_W = 64
POOL_W = 512
POOL_WINDOWS = (2, 4, 8, 16)
POOL_GC = 128
MLA_HEADS = 8
QK_NOPE = 64
QK_ROPE = 32
V_HEAD = 64
Q_LORA = 256
KV_LORA = 256
ROPE_AXIS = 16
ROPE_BASE = 10000.0
N_EXPERTS = 8
TOP_K = 2
HY_BANDS = 16
HY_EMB = 1 + 2 * HY_BANDS
HY_FAST_DECAY = 0.3
HY_SLOW_DECAY = 1.5
HY_TARGET = 1e-2
HY_MIN_DECAY = math.log(HY_TARGET) / HY_SLOW_DECAY
HY_MAX_DECAY = math.log(HY_TARGET) / HY_FAST_DECAY
LN_EPS = 1e-5
RMS_EPS = 1e-6
DEPTH = 2
ALPHA = (2.0 * DEPTH) ** 0.25

LANE = 128
HEAD_SLOT = 128
DFT_N = 128
DFT_HALF = DFT_N // 2 + 1
VMEM_LIMIT = 56 * 1024 * 1024
ATTN_LOOKAHEAD = 2


def _cparams(*sem):
    return pltpu.CompilerParams(dimension_semantics=sem, vmem_limit_bytes=VMEM_LIMIT)


def _layer_norm(v, g, b):
    mu = jnp.mean(v, axis=-1, keepdims=True)
    c = v - mu
    var = jnp.mean(c * c, axis=-1, keepdims=True)
    return c * lax.rsqrt(var + LN_EPS) * g + b


def _rms_norm(v, g):
    return v * lax.rsqrt(jnp.mean(v * v, axis=-1, keepdims=True) + RMS_EPS) * g


def _ada_kernel(s_ref, w_ref, b_ref, o_ref):
    s = s_ref[...]
    s = s * jax.nn.sigmoid(s)
    o_ref[0] = jnp.dot(s, w_ref[0], preferred_element_type=F32, precision=HIGHEST) + b_ref[0]


def _ada(s_rows, ada_w, ada_b):
    depth, d, n6 = ada_w.shape
    tn = 768
    return pl.pallas_call(
        _ada_kernel,
        out_shape=jax.ShapeDtypeStruct((depth, 8, n6), F32),
        grid=(depth, n6 // tn),
        in_specs=[
            pl.BlockSpec((8, d), lambda l, j: (0, 0)),
            pl.BlockSpec((1, d, tn), lambda l, j: (l, 0, j)),
            pl.BlockSpec((1, 1, tn), lambda l, j: (l, 0, j)),
        ],
        out_specs=pl.BlockSpec((1, 8, tn), lambda l, j: (l, 0, j)),
        compiler_params=_cparams("parallel", "parallel"),
        name="ada_mod",
    )(s_rows, ada_w, ada_b.reshape(depth, 1, n6))


def _l0a_kernel(x_ref, c_ref, sc_ref, sh_ref, scc_ref, shc_ref, win_ref, qn_ref, kvn_ref,
                wqa_ref, wqb_ref, wk_ref, wv_ref, cos_ref, sin_ref, cost_ref, sint_ref,
                pu_ref, q_ref, k_ref, v_ref, u_scr, *, n_lat_tiles):
    i = pl.program_id(1)

    @pl.when(i < n_lat_tiles)
    def _():
        u_scr[...] = (x_ref[0] * (1.0 + sc_ref[0]) + sh_ref[0]).astype(BF16)

    @pl.when(i >= n_lat_tiles)
    def _():
        u_scr[...] = (c_ref[0] * (1.0 + scc_ref[0]) + shc_ref[0]).astype(BF16)

    proj = jnp.dot(u_scr[...], win_ref[...], preferred_element_type=F32)
    pu_ref[0] = proj[:, :POOL_W]
    cos = cos_ref[...]
    sin = sin_ref[...]
    q0 = POOL_W
    kv0 = POOL_W + Q_LORA
    r0 = kv0 + KV_LORA
    qn = _rms_norm(proj[:, q0:kv0], qn_ref[...]).astype(BF16)
    kvn = _rms_norm(proj[:, kv0:r0], kvn_ref[...]).astype(BF16)
    nt_dims = (((1,), (1,)), ((), ()))
    qa = lax.dot_general(wqa_ref[...], qn, nt_dims, preferred_element_type=F32)
    qb = lax.dot_general(wqb_ref[...], qn, nt_dims, preferred_element_type=F32)
    vt = lax.dot_general(wv_ref[...], kvn, nt_dims, preferred_element_type=F32)
    kn = jnp.dot(kvn, wk_ref[...], preferred_element_type=F32)
    krot = proj[:, r0:r0 + HEAD_SLOT] * cos + proj[:, r0 + HEAD_SLOT:r0 + 2 * HEAD_SLOT] * sin
    cos_t = cost_ref[...]
    sin_t = sint_ref[...]
    for h in range(MLA_HEADS):
        sl = slice(h * HEAD_SLOT, (h + 1) * HEAD_SLOT)
        q_ref[0, h] = (qa[sl] * cos_t + qb[sl] * sin_t).astype(BF16)
        k_ref[0, h] = (kn[:, sl] + krot).astype(BF16)
        v_ref[0, h] = vt[h * V_HEAD:(h + 1) * V_HEAD].astype(BF16)


def _rope_swap_index():
    half = ROPE_AXIS // 2
    idx = []
    for a in range(2):
        base = a * ROPE_AXIS
        idx += list(range(base + half, base + ROPE_AXIS)) + list(range(base, base + half))
    return np.array(idx)


def _rope_tables(n, n_ctx):
    rows = n // GRID_W
    r = jnp.repeat(jnp.arange(rows, dtype=F32), GRID_W)
    col = jnp.tile(jnp.arange(GRID_W, dtype=F32), rows)
    inv = ROPE_BASE ** (-jnp.arange(0, ROPE_AXIS, 2, dtype=F32) / ROPE_AXIS)
    ang_r = r[:, None] * inv
    ang_c = col[:, None] * inv
    cos32 = jnp.concatenate([jnp.cos(ang_r), jnp.cos(ang_r), jnp.cos(ang_c), jnp.cos(ang_c)], axis=-1)
    sin32 = jnp.concatenate([-jnp.sin(ang_r), jnp.sin(ang_r), -jnp.sin(ang_c), jnp.sin(ang_c)], axis=-1)
    pad = HEAD_SLOT - QK_NOPE - QK_ROPE
    cos = jnp.concatenate([jnp.ones((n, QK_NOPE), F32), cos32, jnp.ones((n, pad), F32)], axis=-1)
    sin = jnp.concatenate([jnp.zeros((n, QK_NOPE), F32), sin32, jnp.zeros((n, pad), F32)], axis=-1)
    cos = jnp.concatenate([cos, jnp.ones((n_ctx, HEAD_SLOT), F32)], axis=0)
    sin = jnp.concatenate([sin, jnp.zeros((n_ctx, HEAD_SLOT), F32)], axis=0)
    return cos, sin


def _l0a_weights(in_w, q_up, kv_up):
    swap = _rope_swap_index()
    d = in_w.shape[0]
    r0 = POOL_W + Q_LORA + KV_LORA
    w_rope = in_w[:, r0:]
    pad_l = jnp.zeros((d, QK_NOPE), F32)
    pad_r = jnp.zeros((d, HEAD_SLOT - QK_NOPE - QK_ROPE), F32)
    kr_a = jnp.concatenate([pad_l, w_rope, pad_r], axis=1)
    kr_b = jnp.concatenate([pad_l, w_rope[:, swap], pad_r], axis=1)
    w_in = jnp.concatenate([in_w[:, :r0], kr_a, kr_b], axis=1).astype(BF16)

    scale = (QK_NOPE + QK_ROPE) ** -0.5 * math.log2(math.e)
    qu = q_up.reshape(Q_LORA, MLA_HEADS, QK_NOPE + QK_ROPE) * scale
    zpad = jnp.zeros((Q_LORA, MLA_HEADS, HEAD_SLOT - QK_NOPE - QK_ROPE), F32)
    wq_a = jnp.concatenate([qu, zpad], axis=-1).reshape(Q_LORA, MLA_HEADS * HEAD_SLOT).T.astype(BF16)
    wq_b = jnp.concatenate([jnp.zeros((Q_LORA, MLA_HEADS, QK_NOPE), F32), qu[..., QK_NOPE:][..., swap], zpad],
                           axis=-1).reshape(Q_LORA, MLA_HEADS * HEAD_SLOT).T.astype(BF16)
    kvu = kv_up.reshape(KV_LORA, MLA_HEADS, QK_NOPE + V_HEAD)
    wk = jnp.concatenate([kvu[..., :QK_NOPE], jnp.zeros((KV_LORA, MLA_HEADS, HEAD_SLOT - QK_NOPE), F32)],
                         axis=-1).reshape(KV_LORA, MLA_HEADS * HEAD_SLOT).astype(BF16)
    wv_t = kvu[..., QK_NOPE:].reshape(KV_LORA, MLA_HEADS * V_HEAD).T.astype(BF16)
    return w_in, wq_a, wq_b, wk, wv_t


def _l0a(x, ctx, sc, sh, scc, shc, in_w, q_norm, q_up, kv_norm, kv_up):
    b, n, d = x.shape
    n_ctx = ctx.shape[1]
    tm = 256
    nl = n // tm
    nt = (n + n_ctx) // tm
    w_in, wq_a, wq_b, wk, wv_t = _l0a_weights(in_w, q_up, kv_up)
    cos, sin = _rope_tables(n, n_ctx)
    hw = MLA_HEADS * HEAD_SLOT
    full = lambda shape: pl.BlockSpec(shape, lambda bi, i: (0,) * len(shape))
    vec = pl.BlockSpec((1, 1, d), lambda bi, i: (bi, 0, 0))
    return pl.pallas_call(
        functools.partial(_l0a_kernel, n_lat_tiles=nl),
        out_shape=[
            jax.ShapeDtypeStruct((b, n + n_ctx, POOL_W), F32),
            jax.ShapeDtypeStruct((b, MLA_HEADS, HEAD_SLOT, n + n_ctx), BF16),
            jax.ShapeDtypeStruct((b, MLA_HEADS, n + n_ctx, HEAD_SLOT), BF16),
            jax.ShapeDtypeStruct((b, MLA_HEADS, V_HEAD, n + n_ctx), BF16),
        ],
        grid=(b, nt),
        in_specs=[
            pl.BlockSpec((1, tm, d), lambda bi, i: (bi, jnp.minimum(i, nl - 1), 0)),
            pl.BlockSpec((1, tm, d), lambda bi, i: (bi, jnp.maximum(i - nl, 0), 0)),
            vec, vec, full((1, 1, d)), full((1, 1, d)),
            full(w_in.shape), full((1, Q_LORA)), full((1, KV_LORA)),
            full((hw, Q_LORA)), full((hw, Q_LORA)), full((KV_LORA, hw)), full((MLA_HEADS * V_HEAD, KV_LORA)),
            pl.BlockSpec((tm, HEAD_SLOT), lambda bi, i: (i, 0)),
            pl.BlockSpec((tm, HEAD_SLOT), lambda bi, i: (i, 0)),
            pl.BlockSpec((HEAD_SLOT, tm), lambda bi, i: (0, i)),
            pl.BlockSpec((HEAD_SLOT, tm), lambda bi, i: (0, i)),
        ],
        out_specs=[
            pl.BlockSpec((1, tm, POOL_W), lambda bi, i: (bi, i, 0)),
            pl.BlockSpec((1, MLA_HEADS, HEAD_SLOT, tm), lambda bi, i: (bi, 0, 0, i)),
            pl.BlockSpec((1, MLA_HEADS, tm, HEAD_SLOT), lambda bi, i: (bi, 0, i, 0)),
            pl.BlockSpec((1, MLA_HEADS, V_HEAD, tm), lambda bi, i: (bi, 0, 0, i)),
        ],
        scratch_shapes=[pltpu.VMEM((tm, d), BF16)],
        compiler_params=_cparams("parallel", "arbitrary"),
        name="l0_in_proj",
    )(x, ctx, sc, sh, scc, shc, w_in, q_norm.reshape(1, -1), kv_norm.reshape(1, -1),
      wq_a, wq_b, wk, wv_t, cos, sin, cos.T, sin.T)


def _attn_kernel(q_ref, k_ref, v_ref, o_ref, m_ref, l_ref, acc_ref, *, nk):
    ki = pl.program_id(2)

    @pl.when(ki == 0)
    def _():
        m_ref[...] = jnp.full(m_ref.shape, -jnp.inf, F32)
        l_ref[...] = jnp.zeros(l_ref.shape, F32)
        acc_ref[...] = jnp.zeros(acc_ref.shape, F32)

    def scores(h):
        return jnp.dot(k_ref[0, h], q_ref[0, h], preferred_element_type=F32)

    pending = [scores(h) for h in range(ATTN_LOOKAHEAD)]
    for h in range(MLA_HEADS):
        if h + ATTN_LOOKAHEAD < MLA_HEADS:
            pending.append(scores(h + ATTN_LOOKAHEAD))
        st = pending.pop(0)
        m_prev = m_ref[h]
        m_new = jnp.maximum(m_prev, jnp.max(st, axis=0, keepdims=True))
        a = jnp.exp2(m_prev - m_new)
        p = jnp.exp2(st - m_new)
        l_ref[h] = a * l_ref[h] + jnp.sum(p, axis=0, keepdims=True)
        pv = jnp.dot(v_ref[0, h], p.astype(BF16), preferred_element_type=F32)
        rows = slice(h * V_HEAD, (h + 1) * V_HEAD)
        acc_ref[rows, :] = a * acc_ref[rows, :] + pv
        m_ref[h] = m_new

    @pl.when(ki == nk - 1)
    def _():
        for h in range(MLA_HEADS):
            rows = slice(h * V_HEAD, (h + 1) * V_HEAD)
            acc_ref[rows, :] = acc_ref[rows, :] / l_ref[h]
        o_ref[0] = acc_ref[...].T.astype(o_ref.dtype)


def _attention(q, k, vt, n):
    b, h, n_all, _ = k.shape
    tq, tk = 512, 1408
    nk = n_all // tk
    return pl.pallas_call(
        functools.partial(_attn_kernel, nk=nk),
        out_shape=jax.ShapeDtypeStruct((b, n, h * V_HEAD), BF16),
        grid=(b, n // tq, nk),
        in_specs=[
            pl.BlockSpec((1, h, HEAD_SLOT, tq), lambda bi, qi, ki: (bi, 0, 0, qi)),
            pl.BlockSpec((1, h, tk, HEAD_SLOT), lambda bi, qi, ki: (bi, 0, ki, 0)),
            pl.BlockSpec((1, h, V_HEAD, tk), lambda bi, qi, ki: (bi, 0, 0, ki)),
        ],
        out_specs=pl.BlockSpec((1, tq, h * V_HEAD), lambda bi, qi, ki: (bi, qi, 0)),
        scratch_shapes=[
            pltpu.VMEM((h, 1, tq), F32),
            pltpu.VMEM((h, 1, tq), F32),
            pltpu.VMEM((h * V_HEAD, tq), F32),
        ],
        compiler_params=_cparams("parallel", "parallel", "arbitrary"),
        name="mla_attention",
    )(q, k, vt)


def _l0b_kernel(pp_ref, pc_ref, pn_ref, o_ref, x_ref, g1_ref, sc2_ref, sh2_ref, pw_ref, ps_ref, ow_ref,
                lg_ref, lb_ref, h_ref, z_ref, *, tm, n):
    i = pl.program_id(1)
    halo = 8
    ext = jnp.concatenate([pp_ref[0], pc_ref[0], pn_ref[0]], axis=0)
    pos = i * tm - halo + lax.broadcasted_iota(jnp.int32, (tm + 2 * halo, 1), 0)
    ext = jnp.where((pos >= 0) & (pos < n), ext, 0.0)
    t = i * tm + lax.broadcasted_iota(jnp.int32, (tm, 1), 0)
    y = jnp.dot(o_ref[0], ow_ref[POOL_W:, :], preferred_element_type=F32)
    for g, w in enumerate(POOL_WINDOWS):
        hw = w // 2
        e = ext[:, g * POOL_GC:(g + 1) * POOL_GC]
        s = e
        width = 1
        while width < w:
            s = s[:s.shape[0] - width] + s[width:]
            width *= 2
        win = s[halo - hw:halo - hw + tm]
        cnt = (jnp.minimum(t + hw, n) - jnp.maximum(t - hw, 0)).astype(F32)
        dd = (win / cnt - e[halo:halo + tm]).astype(BF16)
        yg = jnp.dot(dd, pw_ref[g], preferred_element_type=F32) * ps_ref[:, g * POOL_GC:(g + 1) * POOL_GC]
        y = y + jnp.dot(yg.astype(BF16), ow_ref[g * POOL_GC:(g + 1) * POOL_GC, :], preferred_element_type=F32)
    hh = _layer_norm(ALPHA * x_ref[0] + g1_ref[0] * y, lg_ref[...], lb_ref[...])
    h_ref[0] = hh
    z_ref[0] = (hh * (1.0 + sc2_ref[0]) + sh2_ref[0]).astype(BF16)


def _l0b(pool_u, attn_o, x, g1, sc2, sh2, pool_w, pool_scale, out_w, ln_g, ln_b):
    b, n, d = x.shape
    tm = 512
    hb = tm // 8
    vec = pl.BlockSpec((1, 1, d), lambda bi, i: (bi, 0, 0))
    full = lambda shape: pl.BlockSpec(shape, lambda bi, i: (0,) * len(shape))
    return pl.pallas_call(
        functools.partial(_l0b_kernel, tm=tm, n=n),
        out_shape=[jax.ShapeDtypeStruct((b, n, d), F32), jax.ShapeDtypeStruct((b, n, d), BF16)],
        grid=(b, n // tm),
        in_specs=[
            pl.BlockSpec((1, 8, POOL_W), lambda bi, i: (bi, jnp.maximum(i * hb - 1, 0), 0)),
            pl.BlockSpec((1, tm, POOL_W), lambda bi, i: (bi, i, 0)),
            pl.BlockSpec((1, 8, POOL_W), lambda bi, i: (bi, (i + 1) * hb, 0)),
            pl.BlockSpec((1, tm, POOL_W), lambda bi, i: (bi, i, 0)),
            pl.BlockSpec((1, tm, d), lambda bi, i: (bi, i, 0)),
            vec, vec, vec,
            full(pool_w.shape), full((1, POOL_W)), full(out_w.shape), full((1, d)), full((1, d)),
        ],
        out_specs=[pl.BlockSpec((1, tm, d), lambda bi, i: (bi, i, 0)),
                   pl.BlockSpec((1, tm, d), lambda bi, i: (bi, i, 0))],
        compiler_params=_cparams("parallel", "parallel"),
        name="l0_pool_out_ln",
    )(pool_u, pool_u, pool_u, attn_o, x, g1, sc2, sh2, pool_w.astype(BF16), pool_scale.reshape(1, -1),
      out_w.astype(BF16), ln_g.reshape(1, -1), ln_b.reshape(1, -1))


def _swiglu_step(z, wg_ref, wu_ref, wd_ref, acc_ref):
    g = jnp.dot(z, wg_ref[0].astype(BF16), preferred_element_type=F32)
    u = jnp.dot(z, wu_ref[0].astype(BF16), preferred_element_type=F32)
    a = (g * jax.nn.sigmoid(g) * u).astype(BF16)
    acc_ref[...] += jnp.dot(a, wd_ref[0].astype(BF16), preferred_element_type=F32)


def _ffn_kernel(z_ref, wg_ref, wu_ref, wd_ref, o_ref, acc_ref, *, nj):
    j = pl.program_id(1)

    @pl.when(j == 0)
    def _():
        acc_ref[...] = jnp.zeros(acc_ref.shape, F32)

    _swiglu_step(z_ref[...], wg_ref, wu_ref, wd_ref, acc_ref)

    @pl.when(j == nj - 1)
    def _():
        o_ref[...] = acc_ref[...]


def _ffn_dense(z, wg, wu, wd, tm, tf):
    t, d = z.shape
    f = wg.shape[-1]
    nj = f // tf
    return pl.pallas_call(
        functools.partial(_ffn_kernel, nj=nj),
        out_shape=jax.ShapeDtypeStruct((t, d), F32),
        grid=(t // tm, nj),
        in_specs=[
            pl.BlockSpec((tm, d), lambda i, j: (i, 0)),
            pl.BlockSpec((1, d, tf), lambda i, j: (0, 0, j)),
            pl.BlockSpec((1, d, tf), lambda i, j: (0, 0, j)),
            pl.BlockSpec((1, tf, d), lambda i, j: (0, j, 0)),
        ],
        out_specs=pl.BlockSpec((tm, d), lambda i, j: (i, 0)),
        scratch_shapes=[pltpu.VMEM((tm, d), F32)],
        compiler_params=_cparams("parallel", "arbitrary"),
        name="swiglu_dense",
    )(z, wg, wu, wd)


def _moe_kernel(te_ref, tv_ref, tok0_ref, tokn_ref, z_hbm, wg_ref, wu_ref, wd_ref, gate_ref, o_ref,
                zbuf, zb16, sem, acc_ref, *, nj, tm, n_tiles):
    i = pl.program_id(0)
    j = pl.program_id(1)
    slot = lax.rem(i, 2)
    nxt = 1 - slot
    per_step = zbuf.shape[1] // nj

    def row_copy(tok_ref, row, s):
        tok = tok_ref[0, 0, jnp.minimum(row, tm - 1)]
        return pltpu.make_async_copy(z_hbm.at[pl.ds(tok, 1), :], zbuf.at[s, pl.ds(row, 1), :], sem.at[s])

    def wait_slot(s):
        pltpu.make_async_copy(zbuf.at[s], zbuf.at[s], sem.at[s]).wait()

    def issue_next():
        for r in range(per_step):
            row_copy(tokn_ref, j * per_step + r, nxt).start()

    @pl.when((i == 0) & (j == 0))
    def _():
        def body(r, carry):
            row_copy(tok0_ref, r, 0).start()
            return carry
        lax.fori_loop(0, per_step * nj, body, 0)

    @pl.when(j == 0)
    def _():
        wait_slot(slot)
        zb16[...] = zbuf[slot, :tm, :].astype(BF16)
        acc_ref[...] = jnp.zeros(acc_ref.shape, F32)

    @pl.when(tv_ref[i] > 0)
    def _():
        issue_next()
        _swiglu_step(zb16[...], wg_ref, wu_ref, wd_ref, acc_ref)

    @pl.when(tv_ref[i] == 0)
    def _():
        issue_next()

    @pl.when(j == nj - 1)
    def _():
        o_ref[...] = acc_ref[...] * gate_ref[...]

    @pl.when((i == n_tiles - 1) & (j == nj - 1))
    def _():
        wait_slot(nxt)


def _moe_grouped(z, row_token, wg, wu, wd, gate, tile_expert, tile_valid, tm, tf):
    t, d = z.shape
    p = row_token.shape[0]
    n_tiles = p // tm
    f = wg.shape[-1]
    nj = f // tf
    tok = row_token.reshape(n_tiles, 1, tm)
    buf_rows = nj * (-(-tm // (8 * nj)) * 8)
    smem_blk = lambda imap: pl.BlockSpec((1, 1, tm), imap, memory_space=pltpu.SMEM)
    return pl.pallas_call(
        functools.partial(_moe_kernel, nj=nj, tm=tm, n_tiles=n_tiles),
        out_shape=jax.ShapeDtypeStruct((p, d), F32),
        grid_spec=pltpu.PrefetchScalarGridSpec(
            num_scalar_prefetch=2,
            grid=(n_tiles, nj),
            in_specs=[
                smem_blk(lambda i, j, te, tv: (0, 0, 0)),
                smem_blk(lambda i, j, te, tv: (jnp.minimum(i + 1, n_tiles - 1), 0, 0)),
                pl.BlockSpec(memory_space=pl.ANY),
                pl.BlockSpec((1, d, tf), lambda i, j, te, tv: (te[i], 0, j)),
                pl.BlockSpec((1, d, tf), lambda i, j, te, tv: (te[i], 0, j)),
                pl.BlockSpec((1, tf, d), lambda i, j, te, tv: (te[i], j, 0)),
                pl.BlockSpec((tm, 1), lambda i, j, te, tv: (i, 0)),
            ],
            out_specs=pl.BlockSpec((tm, d), lambda i, j, te, tv: (i, 0)),
            scratch_shapes=[
                pltpu.VMEM((2, buf_rows, d), F32),
                pltpu.VMEM((tm, d), BF16),
                pltpu.SemaphoreType.DMA((2,)),
                pltpu.VMEM((tm, d), F32),
            ],
        ),
        compiler_params=_cparams("arbitrary", "arbitrary"),
        name="moe_swiglu_gather",
    )(tile_expert, tile_valid, tok, tok, z, wg, wu, wd, gate)


def _resid_ln_kernel(*refs, n_y, with_mod):
    h_ref = refs[0]
    y_refs = refs[1:1 + n_y]
    g_ref, lg_ref, lb_ref = refs[1 + n_y:4 + n_y]
    rest = refs[4 + n_y:]
    y = y_refs[0][0]
    for r in y_refs[1:]:
        y = y + r[0]
    hh = _layer_norm(ALPHA * h_ref[0] + g_ref[0] * y, lg_ref[...], lb_ref[...])
    if with_mod:
        sc_ref, sh_ref, ho_ref, u_ref = rest
        ho_ref[0] = hh
        u_ref[0] = (hh * (1.0 + sc_ref[0]) + sh_ref[0]).astype(BF16)
    else:
        rest[0][0] = hh


def _resid_ln(h, ys, g, ln_g, ln_b, mod=None):
    b, n, d = h.shape
    tm = 1024
    blk = pl.BlockSpec((1, tm, d), lambda bi, i: (bi, i, 0))
    vec = pl.BlockSpec((1, 1, d), lambda bi, i: (bi, 0, 0))
    row = pl.BlockSpec((1, d), lambda bi, i: (0, 0))
    in_specs = [blk] + [blk] * len(ys) + [vec, row, row]
    args = [h, *ys, g, ln_g.reshape(1, -1), ln_b.reshape(1, -1)]
    out_shape = [jax.ShapeDtypeStruct((b, n, d), F32)]
    out_specs = [blk]
    if mod is not None:
        in_specs += [vec, vec]
        args += list(mod)
        out_shape.append(jax.ShapeDtypeStruct((b, n, d), BF16))
        out_specs.append(blk)
    return pl.pallas_call(
        functools.partial(_resid_ln_kernel, n_y=len(ys), with_mod=mod is not None),
        out_shape=out_shape,
        grid=(b, n // tm),
        in_specs=in_specs,
        out_specs=out_specs,
        compiler_params=_cparams("parallel", "parallel"),
        name="resid_ln",
    )(*args)


def _l1a_kernel(up_ref, uc_ref, un_ref, w_ref, cw_ref, cb_ref, x0_ref, vx_ref, *, tm, n):
    i = pl.program_id(1)
    halo = 16
    ext = jnp.concatenate([up_ref[0], uc_ref[0], un_ref[0]], axis=0)
    pos = i * tm - halo + lax.broadcasted_iota(jnp.int32, (tm + 2 * halo, 1), 0)
    ext = jnp.where((pos >= 0) & (pos < n), ext, jnp.zeros_like(ext))
    z = jnp.dot(ext, w_ref[...], preferred_element_type=F32)
    cw = cw_ref[...]
    zc = (cw[0:1] * z[halo - 1:halo - 1 + tm] + cw[1:2] * z[halo:halo + tm]
          + cw[2:3] * z[halo + 1:halo + 1 + tm] + cb_ref[...])
    c = zc.shape[1] // 3
    x0_ref[0] = zc[:, :c]
    vx_ref[0] = zc[:, 2 * c:] * zc[:, c:2 * c]


def _l1a(u, hy_in_w, conv_w, conv_b):
    b, n, d = u.shape
    c3 = hy_in_w.shape[1]
    c = c3 // 3
    tm = 512
    hb = tm // 16
    nb16 = n // 16
    full = lambda shape: pl.BlockSpec(shape, lambda bi, i: (0,) * len(shape))
    return pl.pallas_call(
        functools.partial(_l1a_kernel, tm=tm, n=n),
        out_shape=[jax.ShapeDtypeStruct((b, n, c), F32), jax.ShapeDtypeStruct((b, n, c), F32)],
        grid=(b, n // tm),
        in_specs=[
            pl.BlockSpec((1, 16, d), lambda bi, i: (bi, jnp.maximum(i * hb - 1, 0), 0)),
            pl.BlockSpec((1, tm, d), lambda bi, i: (bi, i, 0)),
            pl.BlockSpec((1, 16, d), lambda bi, i: (bi, jnp.minimum((i + 1) * hb, nb16 - 1), 0)),
            full((d, c3)), full((3, c3)), full((1, c3)),
        ],
        out_specs=[pl.BlockSpec((1, tm, c), lambda bi, i: (bi, i, 0)),
                   pl.BlockSpec((1, tm, c), lambda bi, i: (bi, i, 0))],
        compiler_params=_cparams("parallel", "parallel"),
        name="hyena_in_conv",
    )(u, u, u, hy_in_w.astype(BF16), conv_w, conv_b.reshape(1, -1))


def _filter_kernel(z_ref, w1_ref, b1_ref, w2_ref, b2_ref, w3_ref, b3_ref, wo_ref, fr_ref, dl_ref, k_ref, *, tm, n):
    z = z_ref[...]
    fr = fr_ref[...]
    dot = functools.partial(jnp.dot, preferred_element_type=F32, precision=HIGHEST)
    h = jnp.sin(fr * (dot(z, w1_ref[...]) + b1_ref[...]))
    h = jnp.sin(fr * (dot(h, w2_ref[...]) + b2_ref[...]))
    h = jnp.sin(fr * (dot(h, w3_ref[...]) + b3_ref[...]))
    o = dot(h, wo_ref[...]) * jnp.exp(-z[:, 0:1] * dl_ref[...])
    row = pl.program_id(0) * tm + lax.broadcasted_iota(jnp.int32, (tm, 1), 0)
    k_ref[...] = jnp.where(row == n, 0.0, o)


def _hyena_filters(n, fw1, fb1, fw2, fb2, fw3, fb3, fout, freq):
    c = fout.shape[1] // 2
    m = jnp.arange(n, dtype=F32)
    p = jnp.concatenate([m, n - m])[:, None]
    t = p / (n - 1.0)
    w_ang = (2.0 * math.pi / n) * p
    bands = jnp.linspace(1e-4, HY_BANDS - 1, HY_BANDS, dtype=F32)[None, :]
    z = jnp.concatenate([t, jnp.cos(bands * w_ang), -jnp.sin(bands * w_ang),
                         jnp.zeros((2 * n, LANE - HY_EMB), F32)], axis=-1)
    deltas = jnp.abs(jnp.linspace(HY_MIN_DECAY, HY_MAX_DECAY, c, dtype=F32))[None, :]
    padc = lambda a: jnp.pad(a, ((0, 0), (0, LANE - a.shape[1])))
    padr = lambda a: jnp.pad(a, ((0, LANE - a.shape[0]), (0, 0)))
    row = lambda a: padc(a.reshape(1, -1))
    tm = 1024
    half = n // tm
    full = lambda shape: pl.BlockSpec(shape, lambda i: (0,) * len(shape))
    return pl.pallas_call(
        functools.partial(_filter_kernel, tm=tm, n=n),
        out_shape=jax.ShapeDtypeStruct((2 * n, c), F32),
        grid=(2 * half,),
        in_specs=[pl.BlockSpec((tm, LANE), lambda i: (i, 0)),
                  full((LANE, LANE)), full((1, LANE)), full((LANE, LANE)), full((1, LANE)),
                  full((LANE, LANE)), full((1, LANE)), pl.BlockSpec((LANE, c), lambda i: (0, i // half)),
                  full((1, LANE)), full((1, c))],
        out_specs=pl.BlockSpec((tm, c), lambda i: (i, 0)),
        compiler_params=_cparams("parallel"),
        name="hyena_filters",
    )(z, padc(padr(fw1)), row(fb1), padc(padr(fw2)), row(fb2), padc(padr(fw3)), row(fb3),
      padr(fout), row(freq), deltas)


def _dft_consts():
    k = np.arange(DFT_HALF)[:, None]
    n1 = np.arange(DFT_N)[None, :]
    th = 2.0 * np.pi * k * n1 / DFT_N
    fa_r, fa_i = np.cos(th), -np.sin(th)
    kk = np.arange(DFT_N)[:, None] * np.arange(DFT_N)[None, :]
    c, s = np.cos(2.0 * np.pi * kk / DFT_N), np.sin(2.0 * np.pi * kk / DFT_N)
    g_fwd = np.block([[c, s], [-s, c]])
    g_inv = np.block([[c, -s], [s, c]])
    ph = 2.0 * np.pi * np.arange(DFT_HALF)[:, None] * np.arange(DFT_N)[None, :] / (DFT_N * DFT_N)
    tw_c, tw_s = np.cos(ph)[:, :, None], np.sin(ph)[:, :, None]
    n_out = np.arange(DFT_N // 2)[:, None]
    ps = 2.0 * np.pi * n_out * np.arange(DFT_HALF)[None, :] / DFT_N
    wgt = np.full((1, DFT_HALF), 2.0)
    wgt[0, 0] = wgt[0, -1] = 1.0
    length = DFT_N * DFT_N
    fo_r = wgt * np.cos(ps) / length
    fo_i = -wgt * np.sin(ps) / length
    return dict(fa_r=fa_r, fa_i=fa_i, g_fwd=g_fwd, g_inv=g_inv, tw_c=tw_c, tw_s=tw_s, fo_r=fo_r, fo_i=fo_i)


DFT_ROW_PAD = 72
DFT_NB = 16


def _dft_a_kernel(f_ref, x_ref, or_ref, oi_ref):
    f = f_ref[...]
    for j in range(x_ref.shape[2]):
        y = jnp.dot(f, x_ref[0, :, j, :].astype(BF16), preferred_element_type=F32)
        or_ref[0, :, j, :] = y[:DFT_HALF].astype(or_ref.dtype)
        oi_ref[0, :, j, :] = y[DFT_ROW_PAD:DFT_ROW_PAD + DFT_HALF].astype(oi_ref.dtype)


def _dft_a(f_stack, x, out_dtype):
    b, k, n2, c = x.shape
    blk = lambda rows: pl.BlockSpec((1, rows, DFT_NB, c), lambda bi, j: (bi, 0, j, 0))
    return pl.pallas_call(
        _dft_a_kernel,
        out_shape=[jax.ShapeDtypeStruct((b, DFT_HALF, n2, c), out_dtype)] * 2,
        grid=(b, n2 // DFT_NB),
        in_specs=[pl.BlockSpec(f_stack.shape, lambda bi, j: (0, 0)), blk(k)],
        out_specs=[blk(DFT_HALF)] * 2,
        compiler_params=_cparams("parallel", "parallel"),
        name="dft_stage_a",
    )(f_stack, x)


def _spectrum_kernel(ar_ref, ai_ref, tc_ref, ts_ref, g_ref, hr_ref, hi_ref):
    tc = tc_ref[0]
    ts = ts_ref[0]
    ar = ar_ref[0, 0]
    ai = ai_ref[0, 0]
    a2 = jnp.concatenate([ar * tc + ai * ts, ai * tc - ar * ts], axis=0).astype(BF16)
    x = jnp.dot(g_ref[...], a2, preferred_element_type=F32)
    hr_ref[0] = x[:DFT_N]
    hi_ref[0] = x[DFT_N:]


def _filter_spectrum(kr, ki, consts):
    c = kr.shape[-1]
    plane = pl.BlockSpec((1, 1, DFT_N, c), lambda k1: (0, k1, 0, 0))
    tw = pl.BlockSpec((1, DFT_N, 1), lambda k1: (k1, 0, 0))
    out = pl.BlockSpec((1, DFT_N, c), lambda k1: (k1, 0, 0))
    return pl.pallas_call(
        _spectrum_kernel,
        out_shape=[jax.ShapeDtypeStruct((DFT_HALF, DFT_N, c), F32)] * 2,
        grid=(DFT_HALF,),
        in_specs=[plane, plane, tw, tw, pl.BlockSpec((2 * DFT_N, 2 * DFT_N), lambda k1: (0, 0))],
        out_specs=[out, out],
        compiler_params=_cparams("parallel"),
        name="filter_spectrum",
    )(kr, ki, consts["tw_c"], consts["tw_s"], consts["g_fwd"])


def _fft_mid_kernel(ar_ref, ai_ref, hr_ref, hi_ref, tc_ref, ts_ref, gf_ref, gi_ref, br_ref, bi_ref):
    tc = tc_ref[0]
    ts = ts_ref[0]
    hr = hr_ref[0]
    hi = hi_ref[0]
    for b in range(ar_ref.shape[0]):
        ar = ar_ref[b, 0].astype(F32)
        ai = ai_ref[b, 0].astype(F32)
        a2 = jnp.concatenate([ar * tc + ai * ts, ai * tc - ar * ts], axis=0).astype(BF16)
        x = jnp.dot(gf_ref[...], a2, preferred_element_type=F32)
        xr, xi = x[:DFT_N], x[DFT_N:]
        z = jnp.concatenate([xr * hr - xi * hi, xr * hi + xi * hr], axis=0).astype(BF16)
        y = jnp.dot(gi_ref[...], z, preferred_element_type=F32)
        yr, yi = y[:DFT_N], y[DFT_N:]
        br_ref[b, 0] = (yr * tc - yi * ts).astype(br_ref.dtype)
        bi_ref[b, 0] = (yr * ts + yi * tc).astype(bi_ref.dtype)


def _fft_mid(ar, ai, hr, hi, consts):
    b, _, _, c = ar.shape
    plane = pl.BlockSpec((b, 1, DFT_N, c), lambda k1: (0, k1, 0, 0))
    hspec = pl.BlockSpec((1, DFT_N, c), lambda k1: (k1, 0, 0))
    tw = pl.BlockSpec((1, DFT_N, 1), lambda k1: (k1, 0, 0))
    gspec = pl.BlockSpec((2 * DFT_N, 2 * DFT_N), lambda k1: (0, 0))
    return pl.pallas_call(
        _fft_mid_kernel,
        out_shape=[jax.ShapeDtypeStruct(ar.shape, BF16)] * 2,
        grid=(DFT_HALF,),
        in_specs=[plane, plane, hspec, hspec, tw, tw, gspec, gspec],
        out_specs=[plane, plane],
        compiler_params=_cparams("parallel"),
        name="fft_mid",
    )(ar, ai, hr, hi, consts["tw_c"], consts["tw_s"], consts["g_fwd"], consts["g_inv"])


def _idft_gate_kernel(fr_ref, fi_ref, br_ref, bi_ref, x0_ref, vx_ref, sk_ref, o_ref):
    fr = fr_ref[...]
    fi = fi_ref[...]
    sk = sk_ref[...]
    for j in range(br_ref.shape[2]):
        y = (jnp.dot(fr, br_ref[0, :, j, :], preferred_element_type=F32)
             + jnp.dot(fi, bi_ref[0, :, j, :], preferred_element_type=F32))
        o_ref[0, :, j, :] = (x0_ref[0, :, j, :] * (y + vx_ref[0, :, j, :] * sk)).astype(o_ref.dtype)


def _idft_gate(fo_r, fo_i, br, bi, x0, vx, skip_row):
    b, r, n2, c = br.shape
    m = fo_r.shape[0]
    fspec = pl.BlockSpec((m, r), lambda bi_, j: (0, 0))
    blk = lambda rows: pl.BlockSpec((1, rows, DFT_NB, c), lambda bi_, j: (bi_, 0, j, 0))
    return pl.pallas_call(
        _idft_gate_kernel,
        out_shape=jax.ShapeDtypeStruct((b, m, n2, c), BF16),
        grid=(b, n2 // DFT_NB),
        in_specs=[fspec, fspec, blk(r), blk(r), blk(m), blk(m), pl.BlockSpec((1, c), lambda bi_, j: (0, 0))],
        out_specs=blk(m),
        compiler_params=_cparams("parallel", "parallel"),
        name="idft_gate",
    )(fo_r, fo_i, br, bi, x0, vx, skip_row)


def _stack_rows(fr, fi):
    pad = np.zeros((DFT_ROW_PAD - DFT_HALF, fr.shape[1]))
    return jnp.asarray(np.concatenate([fr, pad, fi, pad], axis=0), BF16)


def _hyena_long_conv(x0, vx, kfull, skip):
    b, n, c = vx.shape
    n1 = n // DFT_N
    cn = _dft_consts()
    bf = lambda a: jnp.asarray(a, BF16)
    f32 = lambda a: jnp.asarray(a, F32)
    consts = dict(tw_c=f32(cn["tw_c"]), tw_s=f32(cn["tw_s"]), g_fwd=bf(cn["g_fwd"]), g_inv=bf(cn["g_inv"]))
    kr, ki = _dft_a(_stack_rows(cn["fa_r"], cn["fa_i"]), kfull.reshape(1, DFT_N, DFT_N, c), F32)
    hr, hi = _filter_spectrum(kr, ki, consts)
    vx4 = vx.reshape(b, n1, DFT_N, c)
    ar, ai = _dft_a(_stack_rows(cn["fa_r"][:, :n1], cn["fa_i"][:, :n1]), vx4, BF16)
    br, bi = _fft_mid(ar, ai, hr, hi, consts)
    y = _idft_gate(bf(cn["fo_r"]), bf(cn["fo_i"]), br, bi, x0.reshape(b, n1, DFT_N, c), vx4, skip.reshape(1, c))
    return y.reshape(b, n, c)


def _l1c_kernel(y_ref, h_ref, g1_ref, sc2_ref, sh2_ref, w_ref, lg_ref, lb_ref, rw_ref, ho_ref, z_ref, lo_ref):
    y = jnp.dot(y_ref[0], w_ref[...], preferred_element_type=F32)
    hh = _layer_norm(ALPHA * h_ref[0] + g1_ref[0] * y, lg_ref[...], lb_ref[...])
    ho_ref[0] = hh
    z = hh * (1.0 + sc2_ref[0]) + sh2_ref[0]
    z_ref[0] = z
    lo_ref[0] = jnp.dot(z, rw_ref[...], preferred_element_type=F32, precision=HIGHEST)


def _l1c(y, h, g1, sc2, sh2, out_w, ln_g, ln_b, router_w):
    b, n, d = h.shape
    tm = 512
    blk = pl.BlockSpec((1, tm, d), lambda bi, i: (bi, i, 0))
    vec = pl.BlockSpec((1, 1, d), lambda bi, i: (bi, 0, 0))
    full = lambda shape: pl.BlockSpec(shape, lambda bi, i: (0,) * len(shape))
    rw = jnp.pad(router_w, ((0, 0), (0, LANE - router_w.shape[1])))
    return pl.pallas_call(
        _l1c_kernel,
        out_shape=[jax.ShapeDtypeStruct((b, n, d), F32), jax.ShapeDtypeStruct((b, n, d), F32),
                   jax.ShapeDtypeStruct((b, n, LANE), F32)],
        grid=(b, n // tm),
        in_specs=[blk, blk, vec, vec, vec, full((d, d)), full((1, d)), full((1, d)), full((d, LANE))],
        out_specs=[blk, blk, pl.BlockSpec((1, tm, LANE), lambda bi, i: (bi, i, 0))],
        compiler_params=_cparams("parallel", "parallel"),
        name="hyena_out_ln_router",
    )(y, h, g1, sc2, sh2, out_w.astype(BF16), ln_g.reshape(1, -1), ln_b.reshape(1, -1), rw)


def _take(a, idx):
    return a.at[idx].get(mode="promise_in_bounds")


def _route(logits, tm):
    t = logits.shape[0]
    top_v, top_i = lax.top_k(logits, TOP_K)
    gates = jax.nn.softmax(top_v, axis=-1)
    flat_e = top_i.reshape(-1).astype(jnp.int32)
    flat_g = gates.reshape(-1)
    n_sel = t * TOP_K
    p = n_sel + N_EXPERTS * tm
    eids = jnp.arange(N_EXPERTS, dtype=jnp.int32)[None, :]
    onehot = (flat_e[:, None] == eids).astype(jnp.int32)
    csum = jnp.cumsum(onehot, axis=0)
    counts = csum[-1]
    rank = jnp.sum((csum - onehot) * onehot, axis=1)
    padded = ((counts + tm - 1) // tm) * tm
    end_p = jnp.cumsum(padded)
    start_p = end_p - padded
    start = jnp.cumsum(counts) - counts
    pos = (jnp.sum(onehot * start_p[None, :], axis=1) + rank).reshape(t, TOP_K)
    order = jnp.argsort(flat_e, stable=True).astype(jnp.int32)
    r = jnp.arange(p, dtype=jnp.int32)
    e_row = jnp.sum(r[:, None] >= end_p[None, :], axis=1).astype(jnp.int32)
    oh_r = (jnp.minimum(e_row, N_EXPERTS - 1)[:, None] == eids).astype(jnp.int32)
    j = r - jnp.sum(oh_r * start_p[None, :], axis=1)
    valid = (e_row < N_EXPERTS) & (j < jnp.sum(oh_r * counts[None, :], axis=1))
    src = jnp.clip(jnp.sum(oh_r * start[None, :], axis=1) + j, 0, n_sel - 1)
    flat_idx = _take(order, src)
    row_token = jnp.where(valid, flat_idx // TOP_K, 0)
    row_gate = jnp.where(valid, _take(flat_g, flat_idx), 0.0)
    tile_start = jnp.arange(p // tm, dtype=jnp.int32) * tm
    tile_expert = jnp.sum(tile_start[:, None] >= end_p[None, :], axis=1).astype(jnp.int32)
    tile_valid = (tile_expert < N_EXPERTS).astype(jnp.int32)
    tile_expert = jnp.minimum(tile_expert, N_EXPERTS - 1)
    return row_token, row_gate[:, None], tile_expert, tile_valid, pos


def kernel(x, c, ctx, c_ctx, ada_w, ada_b, ln_g, ln_b, mix_in_w, pool_w, pool_scale, q_norm, q_up, kv_norm, kv_up, mix_out_w, ffn_gate, ffn_up, ffn_down, hy_in_w, hy_conv_w, hy_conv_b, hy_fw1, hy_fb1, hy_fw2, hy_fb2, hy_fw3, hy_fb3, hy_fout, hy_freq, hy_skip, hy_out_w, router_w, moe_gate, moe_up, moe_down):
    b, n, d = x.shape
    t = b * n
    assert b + 1 <= 8
    s_rows = jnp.concatenate([c, c_ctx[None, :], jnp.zeros((8 - b - 1, d), F32)], axis=0)
    mod = _ada(s_rows, ada_w, ada_b)

    def chunks(l, rows):
        m = mod[l, rows].reshape(-1, 6, d)
        return [m[:, k][:, None, :] for k in range(6)]

    sh1, sc1, g1, sh2, sc2, g2 = chunks(0, slice(0, b))
    shc, scc = chunks(0, slice(b, b + 1))[:2]
    pool_u, q, k, v = _l0a(x, ctx, sc1, sh1, scc, shc, mix_in_w[0], q_norm[0], q_up[0], kv_norm[0], kv_up[0])
    attn_o = _attention(q, k, v, n)
    h1, z1 = _l0b(pool_u, attn_o, x, g1, sc2, sh2, pool_w[0], pool_scale[0], mix_out_w[0], ln_g[0, 0], ln_b[0, 0])
    f0 = _ffn_dense(z1.reshape(t, d), ffn_gate.astype(BF16), ffn_up.astype(BF16), ffn_down.astype(BF16), 512, 1408)
    sh1, sc1, g1b, sh2b, sc2b, g2b = chunks(1, slice(0, b))
    h2, u2 = _resid_ln(h1, [f0.reshape(b, n, d)], g2, ln_g[0, 1], ln_b[0, 1], mod=(sc1, sh1))

    x0, vx = _l1a(u2, hy_in_w[0], hy_conv_w[0], hy_conv_b[0])
    kfull = _hyena_filters(n, hy_fw1[0], hy_fb1[0], hy_fw2[0], hy_fb2[0], hy_fw3[0], hy_fb3[0],
                           hy_fout[0], hy_freq[0])
    yl = _hyena_long_conv(x0, vx, kfull, hy_skip[0])
    h3, z3, logits = _l1c(yl, h2, g1b, sc2b, sh2b, hy_out_w[0], ln_g[1, 0], ln_b[1, 0], router_w[0])

    tm_moe = 1024
    row_token, row_gate, tile_expert, tile_valid, pos = _route(logits.reshape(t, LANE)[:, :N_EXPERTS], tm_moe)
    ys = _moe_grouped(z3.reshape(t, d), row_token, moe_gate[0], moe_up[0], moe_down[0], row_gate, tile_expert,
                      tile_valid, tm_moe, 512)
    y_a = _take(ys, pos[:, 0]).reshape(b, n, d)
    y_b = _take(ys, pos[:, 1]).reshape(b, n, d)
    (h4,) = _resid_ln(h3, [y_a, y_b], g2b, ln_g[1, 1], ln_b[1, 1])
    return h4
```

```python
import functools
import math

import numpy as np
import jax
import jax.numpy as jnp
from jax import lax
from jax.experimental import pallas as pl
from jax.experimental.pallas import tpu as pltpu

F32 = jnp.float32
BF16 = jnp.bfloat16
HIGHEST = lax.Precision.HIGHEST

D_MODEL = 1024
GRID_W = 64
POOL_W = 512
POOL_WINDOWS = (2, 4, 8, 16)
POOL_GC = 128
MLA_HEADS = 8
QK_NOPE = 64
QK_ROPE = 32
V_HEAD = 64
Q_LORA = 256
KV_LORA = 256
ROPE_AXIS = 16
ROPE_BASE = 10000.0
N_EXPERTS = 8
TOP_K = 2
HY_BANDS = 16
HY_EMB = 1 + 2 * HY_BANDS
HY_FAST_DECAY = 0.3
HY_SLOW_DECAY = 1.5
HY_TARGET = 1e-2
HY_MIN_DECAY = math.log(HY_TARGET) / HY_SLOW_DECAY
HY_MAX_DECAY = math.log(HY_TARGET) / HY_FAST_DECAY
LN_EPS = 1e-5
RMS_EPS = 1e-6
DEPTH = 2
ALPHA = (2.0 * DEPTH) ** 0.25

LANE = 128
HEAD_SLOT = 128
DFT_N = 128
DFT_HALF = DFT_N // 2 + 1
VMEM_LIMIT = 56 * 1024 * 1024
ATTN_LOOKAHEAD = 2


def _cparams(*sem):
    return pltpu.CompilerParams(dimension_semantics=sem, vmem_limit_bytes=VMEM_LIMIT)


def _layer_norm(v, g, b):
    mu = jnp.mean(v, axis=-1, keepdims=True)
    c = v - mu
    var = jnp.mean(c * c, axis=-1, keepdims=True)
    return c * lax.rsqrt(var + LN_EPS) * g + b


def _rms_norm(v, g):
    return v * lax.rsqrt(jnp.mean(v * v, axis=-1, keepdims=True) + RMS_EPS) * g


def _ada_kernel(s_ref, w_ref, b_ref, o_ref):
    s = s_ref[...]
    s = s * jax.nn.sigmoid(s)
    o_ref[0] = jnp.dot(s, w_ref[0], preferred_element_type=F32, precision=HIGHEST) + b_ref[0]


def _ada(s_rows, ada_w, ada_b):
    depth, d, n6 = ada_w.shape
    tn = 768
    return pl.pallas_call(
        _ada_kernel,
        out_shape=jax.ShapeDtypeStruct((depth, 8, n6), F32),
        grid=(depth, n6 // tn),
        in_specs=[
            pl.BlockSpec((8, d), lambda l, j: (0, 0)),
            pl.BlockSpec((1, d, tn), lambda l, j: (l, 0, j)),
            pl.BlockSpec((1, 1, tn), lambda l, j: (l, 0, j)),
        ],
        out_specs=pl.BlockSpec((1, 8, tn), lambda l, j: (l, 0, j)),
        compiler_params=_cparams("parallel", "parallel"),
        name="ada_mod",
    )(s_rows, ada_w, ada_b.reshape(depth, 1, n6))


def _l0a_kernel(x_ref, c_ref, sc_ref, sh_ref, scc_ref, shc_ref, win_ref, qn_ref, kvn_ref,
                wqa_ref, wqb_ref, wk_ref, wv_ref, cos_ref, sin_ref, cost_ref, sint_ref,
                pu_ref, q_ref, k_ref, v_ref, u_scr, *, n_lat_tiles):
    i = pl.program_id(1)

    @pl.when(i < n_lat_tiles)
    def _():
        u_scr[...] = (x_ref[0] * (1.0 + sc_ref[0]) + sh_ref[0]).astype(BF16)

    @pl.when(i >= n_lat_tiles)
    def _():
        u_scr[...] = (c_ref[0] * (1.0 + scc_ref[0]) + shc_ref[0]).astype(BF16)

    proj = jnp.dot(u_scr[...], win_ref[...], preferred_element_type=F32)
    pu_ref[0] = proj[:, :POOL_W]
    cos = cos_ref[...]
    sin = sin_ref[...]
    q0 = POOL_W
    kv0 = POOL_W + Q_LORA
    r0 = kv0 + KV_LORA
    qn = _rms_norm(proj[:, q0:kv0], qn_ref[...]).astype(BF16)
    kvn = _rms_norm(proj[:, kv0:r0], kvn_ref[...]).astype(BF16)
    nt_dims = (((1,), (1,)), ((), ()))
    qa = lax.dot_general(wqa_ref[...], qn, nt_dims, preferred_element_type=F32)
    qb = lax.dot_general(wqb_ref[...], qn, nt_dims, preferred_element_type=F32)
    vt = lax.dot_general(wv_ref[...], kvn, nt_dims, preferred_element_type=F32)
    kn = jnp.dot(kvn, wk_ref[...], preferred_element_type=F32)
    krot = proj[:, r0:r0 + HEAD_SLOT] * cos + proj[:, r0 + HEAD_SLOT:r0 + 2 * HEAD_SLOT] * sin
    cos_t = cost_ref[...]
    sin_t = sint_ref[...]
    for h in range(MLA_HEADS):
        sl = slice(h * HEAD_SLOT, (h + 1) * HEAD_SLOT)
        q_ref[0, h] = (qa[sl] * cos_t + qb[sl] * sin_t).astype(BF16)
        k_ref[0, h] = (kn[:, sl] + krot).astype(BF16)
        v_ref[0, h] = vt[h * V_HEAD:(h + 1) * V_HEAD].astype(BF16)


def _rope_swap_index():
    half = ROPE_AXIS // 2
    idx = []
    for a in range(2):
        base = a * ROPE_AXIS
        idx += list(range(base + half, base + ROPE_AXIS)) + list(range(base, base + half))
    return np.array(idx)


def _rope_tables(n, n_ctx):
    rows = n // GRID_W
    r = jnp.repeat(jnp.arange(rows, dtype=F32), GRID_W)
    col = jnp.tile(jnp.arange(GRID_W, dtype=F32), rows)
    inv = ROPE_BASE ** (-jnp.arange(0, ROPE_AXIS, 2, dtype=F32) / ROPE_AXIS)
    ang_r = r[:, None] * inv
    ang_c = col[:, None] * inv
    cos32 = jnp.concatenate([jnp.cos(ang_r), jnp.cos(ang_r), jnp.cos(ang_c), jnp.cos(ang_c)], axis=-1)
    sin32 = jnp.concatenate([-jnp.sin(ang_r), jnp.sin(ang_r), -jnp.sin(ang_c), jnp.sin(ang_c)], axis=-1)
    pad = HEAD_SLOT - QK_NOPE - QK_ROPE
    cos = jnp.concatenate([jnp.ones((n, QK_NOPE), F32), cos32, jnp.ones((n, pad), F32)], axis=-1)
    sin = jnp.concatenate([jnp.zeros((n, QK_NOPE), F32), sin32, jnp.zeros((n, pad), F32)], axis=-1)
    cos = jnp.concatenate([cos, jnp.ones((n_ctx, HEAD_SLOT), F32)], axis=0)
    sin = jnp.concatenate([sin, jnp.zeros((n_ctx, HEAD_SLOT), F32)], axis=0)
    return cos, sin


def _l0a_weights(in_w, q_up, kv_up):
    swap = _rope_swap_index()
    d = in_w.shape[0]
    r0 = POOL_W + Q_LORA + KV_LORA
    w_rope = in_w[:, r0:]
    pad_l = jnp.zeros((d, QK_NOPE), F32)
    pad_r = jnp.zeros((d, HEAD_SLOT - QK_NOPE - QK_ROPE), F32)
    kr_a = jnp.concatenate([pad_l, w_rope, pad_r], axis=1)
    kr_b = jnp.concatenate([pad_l, w_rope[:, swap], pad_r], axis=1)
    w_in = jnp.concatenate([in_w[:, :r0], kr_a, kr_b], axis=1).astype(BF16)

    scale = (QK_NOPE + QK_ROPE) ** -0.5 * math.log2(math.e)
    qu = q_up.reshape(Q_LORA, MLA_HEADS, QK_NOPE + QK_ROPE) * scale
    zpad = jnp.zeros((Q_LORA, MLA_HEADS, HEAD_SLOT - QK_NOPE - QK_ROPE), F32)
    wq_a = jnp.concatenate([qu, zpad], axis=-1).reshape(Q_LORA, MLA_HEADS * HEAD_SLOT).T.astype(BF16)
    wq_b = jnp.concatenate([jnp.zeros((Q_LORA, MLA_HEADS, QK_NOPE), F32), qu[..., QK_NOPE:][..., swap], zpad],
                           axis=-1).reshape(Q_LORA, MLA_HEADS * HEAD_SLOT).T.astype(BF16)
    kvu = kv_up.reshape(KV_LORA, MLA_HEADS, QK_NOPE + V_HEAD)
    wk = jnp.concatenate([kvu[..., :QK_NOPE], jnp.zeros((KV_LORA, MLA_HEADS, HEAD_SLOT - QK_NOPE), F32)],
                         axis=-1).reshape(KV_LORA, MLA_HEADS * HEAD_SLOT).astype(BF16)
    wv_t = kvu[..., QK_NOPE:].reshape(KV_LORA, MLA_HEADS * V_HEAD).T.astype(BF16)
    return w_in, wq_a, wq_b, wk, wv_t


def _l0a(x, ctx, sc, sh, scc, shc, in_w, q_norm, q_up, kv_norm, kv_up):
    b, n, d = x.shape
    n_ctx = ctx.shape[1]
    tm = 256
    nl = n // tm
    nt = (n + n_ctx) // tm
    w_in, wq_a, wq_b, wk, wv_t = _l0a_weights(in_w, q_up, kv_up)
    cos, sin = _rope_tables(n, n_ctx)
    hw = MLA_HEADS * HEAD_SLOT
    full = lambda shape: pl.BlockSpec(shape, lambda bi, i: (0,) * len(shape))
    vec = pl.BlockSpec((1, 1, d), lambda bi, i: (bi, 0, 0))
    return pl.pallas_call(
        functools.partial(_l0a_kernel, n_lat_tiles=nl),
        out_shape=[
            jax.ShapeDtypeStruct((b, n + n_ctx, POOL_W), F32),
            jax.ShapeDtypeStruct((b, MLA_HEADS, HEAD_SLOT, n + n_ctx), BF16),
            jax.ShapeDtypeStruct((b, MLA_HEADS, n + n_ctx, HEAD_SLOT), BF16),
            jax.ShapeDtypeStruct((b, MLA_HEADS, V_HEAD, n + n_ctx), BF16),
        ],
        grid=(b, nt),
        in_specs=[
            pl.BlockSpec((1, tm, d), lambda bi, i: (bi, jnp.minimum(i, nl - 1), 0)),
            pl.BlockSpec((1, tm, d), lambda bi, i: (bi, jnp.maximum(i - nl, 0), 0)),
            vec, vec, full((1, 1, d)), full((1, 1, d)),
            full(w_in.shape), full((1, Q_LORA)), full((1, KV_LORA)),
            full((hw, Q_LORA)), full((hw, Q_LORA)), full((KV_LORA, hw)), full((MLA_HEADS * V_HEAD, KV_LORA)),
            pl.BlockSpec((tm, HEAD_SLOT), lambda bi, i: (i, 0)),
            pl.BlockSpec((tm, HEAD_SLOT), lambda bi, i: (i, 0)),
            pl.BlockSpec((HEAD_SLOT, tm), lambda bi, i: (0, i)),
            pl.BlockSpec((HEAD_SLOT, tm), lambda bi, i: (0, i)),
        ],
        out_specs=[
            pl.BlockSpec((1, tm, POOL_W), lambda bi, i: (bi, i, 0)),
            pl.BlockSpec((1, MLA_HEADS, HEAD_SLOT, tm), lambda bi, i: (bi, 0, 0, i)),
            pl.BlockSpec((1, MLA_HEADS, tm, HEAD_SLOT), lambda bi, i: (bi, 0, i, 0)),
            pl.BlockSpec((1, MLA_HEADS, V_HEAD, tm), lambda bi, i: (bi, 0, 0, i)),
        ],
        scratch_shapes=[pltpu.VMEM((tm, d), BF16)],
        compiler_params=_cparams("parallel", "arbitrary"),
        name="l0_in_proj",
    )(x, ctx, sc, sh, scc, shc, w_in, q_norm.reshape(1, -1), kv_norm.reshape(1, -1),
      wq_a, wq_b, wk, wv_t, cos, sin, cos.T, sin.T)


def _attn_kernel(q_ref, k_ref, v_ref, o_ref, m_ref, l_ref, acc_ref, *, nk):
    ki = pl.program_id(2)

    @pl.when(ki == 0)
    def _():
        m_ref[...] = jnp.full(m_ref.shape, -jnp.inf, F32)
        l_ref[...] = jnp.zeros(l_ref.shape, F32)
        acc_ref[...] = jnp.zeros(acc_ref.shape, F32)

    def scores(h):
        return jnp.dot(k_ref[0, h], q_ref[0, h], preferred_element_type=F32)

    pending = [scores(h) for h in range(ATTN_LOOKAHEAD)]
    for h in range(MLA_HEADS):
        if h + ATTN_LOOKAHEAD < MLA_HEADS:
            pending.append(scores(h + ATTN_LOOKAHEAD))
        st = pending.pop(0)
        m_prev = m_ref[h]
        m_new = jnp.maximum(m_prev, jnp.max(st, axis=0, keepdims=True))
        a = jnp.exp2(m_prev - m_new)
        p = jnp.exp2(st - m_new)
        l_ref[h] = a * l_ref[h] + jnp.sum(p, axis=0, keepdims=True)
        pv = jnp.dot(v_ref[0, h], p.astype(BF16), preferred_element_type=F32)
        rows = slice(h * V_HEAD, (h + 1) * V_HEAD)
        acc_ref[rows, :] = a * acc_ref[rows, :] + pv
        m_ref[h] = m_new

    @pl.when(ki == nk - 1)
    def _():
        for h in range(MLA_HEADS):
            rows = slice(h * V_HEAD, (h + 1) * V_HEAD)
            acc_ref[rows, :] = acc_ref[rows, :] / l_ref[h]
        o_ref[0] = acc_ref[...].T.astype(o_ref.dtype)


def _attention(q, k, vt, n):
    b, h, n_all, _ = k.shape
    tq, tk = 512, 1408
    nk = n_all // tk
    return pl.pallas_call(
        functools.partial(_attn_kernel, nk=nk),
        out_shape=jax.ShapeDtypeStruct((b, n, h * V_HEAD), BF16),
        grid=(b, n // tq, nk),
        in_specs=[
            pl.BlockSpec((1, h, HEAD_SLOT, tq), lambda bi, qi, ki: (bi, 0, 0, qi)),
            pl.BlockSpec((1, h, tk, HEAD_SLOT), lambda bi, qi, ki: (bi, 0, ki, 0)),
            pl.BlockSpec((1, h, V_HEAD, tk), lambda bi, qi, ki: (bi, 0, 0, ki)),
        ],
        out_specs=pl.BlockSpec((1, tq, h * V_HEAD), lambda bi, qi, ki: (bi, qi, 0)),
        scratch_shapes=[
            pltpu.VMEM((h, 1, tq), F32),
            pltpu.VMEM((h, 1, tq), F32),
            pltpu.VMEM((h * V_HEAD, tq), F32),
        ],
        compiler_params=_cparams("parallel", "parallel", "arbitrary"),
        name="mla_attention",
    )(q, k, vt)


def _l0b_kernel(pp_ref, pc_ref, pn_ref, o_ref, x_ref, g1_ref, sc2_ref, sh2_ref, pw_ref, ps_ref, ow_ref,
                lg_ref, lb_ref, h_ref, z_ref, *, tm, n):
    i = pl.program_id(1)
    halo = 8
    ext = jnp.concatenate([pp_ref[0], pc_ref[0], pn_ref[0]], axis=0)
    pos = i * tm - halo + lax.broadcasted_iota(jnp.int32, (tm + 2 * halo, 1), 0)
    ext = jnp.where((pos >= 0) & (pos < n), ext, 0.0)
    t = i * tm + lax.broadcasted_iota(jnp.int32, (tm, 1), 0)
    y = jnp.dot(o_ref[0], ow_ref[POOL_W:, :], preferred_element_type=F32)
    for g, w in enumerate(POOL_WINDOWS):
        hw = w // 2
        e = ext[:, g * POOL_GC:(g + 1) * POOL_GC]
        s = e
        width = 1
        while width < w:
            s = s[:s.shape[0] - width] + s[width:]
            width *= 2
        win = s[halo - hw:halo - hw + tm]
        cnt = (jnp.minimum(t + hw, n) - jnp.maximum(t - hw, 0)).astype(F32)
        dd = (win / cnt - e[halo:halo + tm]).astype(BF16)
        yg = jnp.dot(dd, pw_ref[g], preferred_element_type=F32) * ps_ref[:, g * POOL_GC:(g + 1) * POOL_GC]
        y = y + jnp.dot(yg.astype(BF16), ow_ref[g * POOL_GC:(g + 1) * POOL_GC, :], preferred_element_type=F32)
    hh = _layer_norm(ALPHA * x_ref[0] + g1_ref[0] * y, lg_ref[...], lb_ref[...])
    h_ref[0] = hh
    z_ref[0] = (hh * (1.0 + sc2_ref[0]) + sh2_ref[0]).astype(BF16)


def _l0b(pool_u, attn_o, x, g1, sc2, sh2, pool_w, pool_scale, out_w, ln_g, ln_b):
    b, n, d = x.shape
    tm = 512
    hb = tm // 8
    vec = pl.BlockSpec((1, 1, d), lambda bi, i: (bi, 0, 0))
    full = lambda shape: pl.BlockSpec(shape, lambda bi, i: (0,) * len(shape))
    return pl.pallas_call(
        functools.partial(_l0b_kernel, tm=tm, n=n),
        out_shape=[jax.ShapeDtypeStruct((b, n, d), F32), jax.ShapeDtypeStruct((b, n, d), BF16)],
        grid=(b, n // tm),
        in_specs=[
            pl.BlockSpec((1, 8, POOL_W), lambda bi, i: (bi, jnp.maximum(i * hb - 1, 0), 0)),
            pl.BlockSpec((1, tm, POOL_W), lambda bi, i: (bi, i, 0)),
            pl.BlockSpec((1, 8, POOL_W), lambda bi, i: (bi, (i + 1) * hb, 0)),
            pl.BlockSpec((1, tm, POOL_W), lambda bi, i: (bi, i, 0)),
            pl.BlockSpec((1, tm, d), lambda bi, i: (bi, i, 0)),
            vec, vec, vec,
            full(pool_w.shape), full((1, POOL_W)), full(out_w.shape), full((1, d)), full((1, d)),
        ],
        out_specs=[pl.BlockSpec((1, tm, d), lambda bi, i: (bi, i, 0)),
                   pl.BlockSpec((1, tm, d), lambda bi, i: (bi, i, 0))],
        compiler_params=_cparams("parallel", "parallel"),
        name="l0_pool_out_ln",
    )(pool_u, pool_u, pool_u, attn_o, x, g1, sc2, sh2, pool_w.astype(BF16), pool_scale.reshape(1, -1),
      out_w.astype(BF16), ln_g.reshape(1, -1), ln_b.reshape(1, -1))


def _swiglu_step(z, wg_ref, wu_ref, wd_ref, acc_ref):
    g = jnp.dot(z, wg_ref[0].astype(BF16), preferred_element_type=F32)
    u = jnp.dot(z, wu_ref[0].astype(BF16), preferred_element_type=F32)
    a = (g * jax.nn.sigmoid(g) * u).astype(BF16)
    acc_ref[...] += jnp.dot(a, wd_ref[0].astype(BF16), preferred_element_type=F32)


def _ffn_kernel(z_ref, wg_ref, wu_ref, wd_ref, o_ref, acc_ref, *, nj):
    j = pl.program_id(1)

    @pl.when(j == 0)
    def _():
        acc_ref[...] = jnp.zeros(acc_ref.shape, F32)

    _swiglu_step(z_ref[...], wg_ref, wu_ref, wd_ref, acc_ref)

    @pl.when(j == nj - 1)
    def _():
        o_ref[...] = acc_ref[...]


def _ffn_dense(z, wg, wu, wd, tm, tf):
    t, d = z.shape
    f = wg.shape[-1]
    nj = f // tf
    return pl.pallas_call(
        functools.partial(_ffn_kernel, nj=nj),
        out_shape=jax.ShapeDtypeStruct((t, d), F32),
        grid=(t // tm, nj),
        in_specs=[
            pl.BlockSpec((tm, d), lambda i, j: (i, 0)),
            pl.BlockSpec((1, d, tf), lambda i, j: (0, 0, j)),
            pl.BlockSpec((1, d, tf), lambda i, j: (0, 0, j)),
            pl.BlockSpec((1, tf, d), lambda i, j: (0, j, 0)),
        ],
        out_specs=pl.BlockSpec((tm, d), lambda i, j: (i, 0)),
        scratch_shapes=[pltpu.VMEM((tm, d), F32)],
        compiler_params=_cparams("parallel", "arbitrary"),
        name="swiglu_dense",
    )(z, wg, wu, wd)


def _moe_kernel(te_ref, tv_ref, tok0_ref, tokn_ref, z_hbm, wg_ref, wu_ref, wd_ref, gate_ref, o_ref,
                zbuf, zb16, sem, acc_ref, *, nj, tm, n_tiles):
    i = pl.program_id(0)
    j = pl.program_id(1)
    slot = lax.rem(i, 2)
    nxt = 1 - slot
    segs = z_hbm.shape[1]
    per_step = zbuf.shape[1] // (segs * nj)

    def row_copy(tok_ref, row, s):
        tok = tok_ref[0, 0, jnp.minimum(row, tm - 1)]
        return pltpu.make_async_copy(z_hbm.at[tok], zbuf.at[s, pl.ds(row * segs, segs), :], sem.at[s])

    def wait_slot(s):
        pltpu.make_async_copy(zbuf.at[s], zbuf.at[s], sem.at[s]).wait()

    def issue_next():
        for r in range(per_step):
            row_copy(tokn_ref, j * per_step + r, nxt).start()

    @pl.when((i == 0) & (j == 0))
    def _():
        def body(r, carry):
            row_copy(tok0_ref, r, 0).start()
            return carry
        lax.fori_loop(0, per_step * nj, body, 0)

    @pl.when(j == 0)
    def _():
        wait_slot(slot)
        for g in range(segs):
            zb16[:, g * LANE:(g + 1) * LANE] = zbuf.at[slot][pl.ds(g, tm, stride=segs), :].astype(BF16)
        acc_ref[...] = jnp.zeros(acc_ref.shape, F32)

    @pl.when(tv_ref[i] > 0)
    def _():
        issue_next()
        _swiglu_step(zb16[...], wg_ref, wu_ref, wd_ref, acc_ref)

    @pl.when(tv_ref[i] == 0)
    def _():
        issue_next()

    @pl.when(j == nj - 1)
    def _():
        o_ref[...] = acc_ref[...] * gate_ref[...]

    @pl.when((i == n_tiles - 1) & (j == nj - 1))
    def _():
        wait_slot(nxt)


def _moe_grouped(z, row_token, wg, wu, wd, gate, tile_expert, tile_valid, tm, tf):
    t, segs, _ = z.shape
    d = segs * LANE
    p = row_token.shape[0]
    n_tiles = p // tm
    f = wg.shape[-1]
    nj = f // tf
    tok = row_token.reshape(n_tiles, 1, tm)
    buf_rows = nj * (-(-tm // (8 * nj)) * 8)
    smem_blk = lambda imap: pl.BlockSpec((1, 1, tm), imap, memory_space=pltpu.SMEM)
    return pl.pallas_call(
        functools.partial(_moe_kernel, nj=nj, tm=tm, n_tiles=n_tiles),
        out_shape=jax.ShapeDtypeStruct((p, d), F32),
        grid_spec=pltpu.PrefetchScalarGridSpec(
            num_scalar_prefetch=2,
            grid=(n_tiles, nj),
            in_specs=[
                smem_blk(lambda i, j, te, tv: (0, 0, 0)),
                smem_blk(lambda i, j, te, tv: (jnp.minimum(i + 1, n_tiles - 1), 0, 0)),
                pl.BlockSpec(memory_space=pl.ANY),
                pl.BlockSpec((1, d, tf), lambda i, j, te, tv: (te[i], 0, j)),
                pl.BlockSpec((1, d, tf), lambda i, j, te, tv: (te[i], 0, j)),
                pl.BlockSpec((1, tf, d), lambda i, j, te, tv: (te[i], j, 0)),
                pl.BlockSpec((tm, 1), lambda i, j, te, tv: (i, 0)),
            ],
            out_specs=pl.BlockSpec((tm, d), lambda i, j, te, tv: (i, 0)),
            scratch_shapes=[
                pltpu.VMEM((2, buf_rows * segs, LANE), F32),
                pltpu.VMEM((tm, d), BF16),
                pltpu.SemaphoreType.DMA((2,)),
                pltpu.VMEM((tm, d), F32),
            ],
        ),
        compiler_params=_cparams("arbitrary", "arbitrary"),
        name="moe_swiglu_gather",
    )(tile_expert, tile_valid, tok, tok, z, wg, wu, wd, gate)


def _resid_ln_kernel(*refs, n_y, with_mod):
    h_ref = refs[0]
    y_refs = refs[1:1 + n_y]
    g_ref, lg_ref, lb_ref = refs[1 + n_y:4 + n_y]
    rest = refs[4 + n_y:]
    y = y_refs[0][0]
    for r in y_refs[1:]:
        y = y + r[0]
    hh = _layer_norm(ALPHA * h_ref[0] + g_ref[0] * y, lg_ref[...], lb_ref[...])
    if with_mod:
        sc_ref, sh_ref, ho_ref, u_ref = rest
        ho_ref[0] = hh
        u_ref[0] = (hh * (1.0 + sc_ref[0]) + sh_ref[0]).astype(BF16)
    else:
        rest[0][0] = hh


def _resid_ln(h, ys, g, ln_g, ln_b, mod=None):
    b, n, d = h.shape
    tm = 1024
    blk = pl.BlockSpec((1, tm, d), lambda bi, i: (bi, i, 0))
    vec = pl.BlockSpec((1, 1, d), lambda bi, i: (bi, 0, 0))
    row = pl.BlockSpec((1, d), lambda bi, i: (0, 0))
    in_specs = [blk] + [blk] * len(ys) + [vec, row, row]
    args = [h, *ys, g, ln_g.reshape(1, -1), ln_b.reshape(1, -1)]
    out_shape = [jax.ShapeDtypeStruct((b, n, d), F32)]
    out_specs = [blk]
    if mod is not None:
        in_specs += [vec, vec]
        args += list(mod)
        out_shape.append(jax.ShapeDtypeStruct((b, n, d), BF16))
        out_specs.append(blk)
    return pl.pallas_call(
        functools.partial(_resid_ln_kernel, n_y=len(ys), with_mod=mod is not None),
        out_shape=out_shape,
        grid=(b, n // tm),
        in_specs=in_specs,
        out_specs=out_specs,
        compiler_params=_cparams("parallel", "parallel"),
        name="resid_ln",
    )(*args)


def _l1a_kernel(up_ref, uc_ref, un_ref, w_ref, cw_ref, cb_ref, x0_ref, vx_ref, *, tm, n):
    i = pl.program_id(1)
    halo = 16
    ext = jnp.concatenate([up_ref[0], uc_ref[0], un_ref[0]], axis=0)
    pos = i * tm - halo + lax.broadcasted_iota(jnp.int32, (tm + 2 * halo, 1), 0)
    ext = jnp.where((pos >= 0) & (pos < n), ext, jnp.zeros_like(ext))
    z = jnp.dot(ext, w_ref[...], preferred_element_type=F32)
    cw = cw_ref[...]
    zc = (cw[0:1] * z[halo - 1:halo - 1 + tm] + cw[1:2] * z[halo:halo + tm]
          + cw[2:3] * z[halo + 1:halo + 1 + tm] + cb_ref[...])
    c = zc.shape[1] // 3
    x0_ref[0] = zc[:, :c].astype(x0_ref.dtype)
    vx_ref[0] = (zc[:, 2 * c:] * zc[:, c:2 * c]).astype(vx_ref.dtype)


def _l1a(u, hy_in_w, conv_w, conv_b):
    b, n, d = u.shape
    c3 = hy_in_w.shape[1]
    c = c3 // 3
    tm = 512
    hb = tm // 16
    nb16 = n // 16
    full = lambda shape: pl.BlockSpec(shape, lambda bi, i: (0,) * len(shape))
    return pl.pallas_call(
        functools.partial(_l1a_kernel, tm=tm, n=n),
        out_shape=[jax.ShapeDtypeStruct((b, n, c), BF16), jax.ShapeDtypeStruct((b, n, c), BF16)],
        grid=(b, n // tm),
        in_specs=[
            pl.BlockSpec((1, 16, d), lambda bi, i: (bi, jnp.maximum(i * hb - 1, 0), 0)),
            pl.BlockSpec((1, tm, d), lambda bi, i: (bi, i, 0)),
            pl.BlockSpec((1, 16, d), lambda bi, i: (bi, jnp.minimum((i + 1) * hb, nb16 - 1), 0)),
            full((d, c3)), full((3, c3)), full((1, c3)),
        ],
        out_specs=[pl.BlockSpec((1, tm, c), lambda bi, i: (bi, i, 0)),
                   pl.BlockSpec((1, tm, c), lambda bi, i: (bi, i, 0))],
        compiler_params=_cparams("parallel", "parallel"),
        name="hyena_in_conv",
    )(u, u, u, hy_in_w.astype(BF16), conv_w, conv_b.reshape(1, -1))


def _filter_kernel(z_ref, w1_ref, b1_ref, w2_ref, b2_ref, w3_ref, b3_ref, wo_ref, fr_ref, dl_ref, k_ref, *, tm, n):
    z = z_ref[...]
    fr = fr_ref[...]
    dot = functools.partial(jnp.dot, preferred_element_type=F32, precision=HIGHEST)
    h = jnp.sin(fr * (dot(z, w1_ref[...]) + b1_ref[...]))
    h = jnp.sin(fr * (dot(h, w2_ref[...]) + b2_ref[...]))
    h = jnp.sin(fr * (dot(h, w3_ref[...]) + b3_ref[...]))
    o = dot(h, wo_ref[...]) * jnp.exp(-z[:, 0:1] * dl_ref[...])
    row = pl.program_id(0) * tm + lax.broadcasted_iota(jnp.int32, (tm, 1), 0)
    k_ref[...] = jnp.where(row == n, 0.0, o)


def _hyena_filters(n, fw1, fb1, fw2, fb2, fw3, fb3, fout, freq):
    c = fout.shape[1] // 2
    m = jnp.arange(n, dtype=F32)
    p = jnp.concatenate([m, n - m])[:, None]
    t = p / (n - 1.0)
    w_ang = (2.0 * math.pi / n) * p
    bands = jnp.linspace(1e-4, HY_BANDS - 1, HY_BANDS, dtype=F32)[None, :]
    z = jnp.concatenate([t, jnp.cos(bands * w_ang), -jnp.sin(bands * w_ang),
                         jnp.zeros((2 * n, LANE - HY_EMB), F32)], axis=-1)
    deltas = jnp.abs(jnp.linspace(HY_MIN_DECAY, HY_MAX_DECAY, c, dtype=F32))[None, :]
    padc = lambda a: jnp.pad(a, ((0, 0), (0, LANE - a.shape[1])))
    padr = lambda a: jnp.pad(a, ((0, LANE - a.shape[0]), (0, 0)))
    row = lambda a: padc(a.reshape(1, -1))
    tm = 1024
    half = n // tm
    full = lambda shape: pl.BlockSpec(shape, lambda i: (0,) * len(shape))
    return pl.pallas_call(
        functools.partial(_filter_kernel, tm=tm, n=n),
        out_shape=jax.ShapeDtypeStruct((2 * n, c), F32),
        grid=(2 * half,),
        in_specs=[pl.BlockSpec((tm, LANE), lambda i: (i, 0)),
                  full((LANE, LANE)), full((1, LANE)), full((LANE, LANE)), full((1, LANE)),
                  full((LANE, LANE)), full((1, LANE)), pl.BlockSpec((LANE, c), lambda i: (0, i // half)),
                  full((1, LANE)), full((1, c))],
        out_specs=pl.BlockSpec((tm, c), lambda i: (i, 0)),
        compiler_params=_cparams("parallel"),
        name="hyena_filters",
    )(z, padc(padr(fw1)), row(fb1), padc(padr(fw2)), row(fb2), padc(padr(fw3)), row(fb3),
      padr(fout), row(freq), deltas)


def _dft_consts():
    k = np.arange(DFT_HALF)[:, None]
    n1 = np.arange(DFT_N)[None, :]
    th = 2.0 * np.pi * k * n1 / DFT_N
    fa_r, fa_i = np.cos(th), -np.sin(th)
    kk = np.arange(DFT_N)[:, None] * np.arange(DFT_N)[None, :]
    c, s = np.cos(2.0 * np.pi * kk / DFT_N), np.sin(2.0 * np.pi * kk / DFT_N)
    g_fwd = np.block([[c, s], [-s, c]])
    g_inv = np.block([[c, -s], [s, c]])
    ph = 2.0 * np.pi * np.arange(DFT_HALF)[:, None] * np.arange(DFT_N)[None, :] / (DFT_N * DFT_N)
    tw_c, tw_s = np.cos(ph)[:, :, None], np.sin(ph)[:, :, None]
    n_out = np.arange(DFT_N // 2)[:, None]
    ps = 2.0 * np.pi * n_out * np.arange(DFT_HALF)[None, :] / DFT_N
    wgt = np.full((1, DFT_HALF), 2.0)
    wgt[0, 0] = wgt[0, -1] = 1.0
    length = DFT_N * DFT_N
    fo_r = wgt * np.cos(ps) / length
    fo_i = -wgt * np.sin(ps) / length
    return dict(fa_r=fa_r, fa_i=fa_i, g_fwd=g_fwd, g_inv=g_inv, tw_c=tw_c, tw_s=tw_s, fo_r=fo_r, fo_i=fo_i)


DFT_ROW_PAD = 72
DFT_NB = 16


def _dft_a_kernel(f_ref, x_ref, or_ref, oi_ref):
    f = f_ref[...]
    for j in range(x_ref.shape[2]):
        y = jnp.dot(f, x_ref[0, :, j, :].astype(BF16), preferred_element_type=F32)
        or_ref[0, :, j, :] = y[:DFT_HALF].astype(or_ref.dtype)
        oi_ref[0, :, j, :] = y[DFT_ROW_PAD:DFT_ROW_PAD + DFT_HALF].astype(oi_ref.dtype)


def _dft_a(f_stack, x, out_dtype):
    b, k, n2, c = x.shape
    blk = lambda rows: pl.BlockSpec((1, rows, DFT_NB, c), lambda bi, j: (bi, 0, j, 0))
    return pl.pallas_call(
        _dft_a_kernel,
        out_shape=[jax.ShapeDtypeStruct((b, DFT_HALF, n2, c), out_dtype)] * 2,
        grid=(b, n2 // DFT_NB),
        in_specs=[pl.BlockSpec(f_stack.shape, lambda bi, j: (0, 0)), blk(k)],
        out_specs=[blk(DFT_HALF)] * 2,
        compiler_params=_cparams("parallel", "parallel"),
        name="dft_stage_a",
    )(f_stack, x)


def _spectrum_kernel(ar_ref, ai_ref, tc_ref, ts_ref, g_ref, hr_ref, hi_ref):
    tc = tc_ref[0]
    ts = ts_ref[0]
    ar = ar_ref[0, 0]
    ai = ai_ref[0, 0]
    a2 = jnp.concatenate([ar * tc + ai * ts, ai * tc - ar * ts], axis=0).astype(BF16)
    x = jnp.dot(g_ref[...], a2, preferred_element_type=F32)
    hr_ref[0] = x[:DFT_N]
    hi_ref[0] = x[DFT_N:]


def _filter_spectrum(kr, ki, consts):
    c = kr.shape[-1]
    plane = pl.BlockSpec((1, 1, DFT_N, c), lambda k1: (0, k1, 0, 0))
    tw = pl.BlockSpec((1, DFT_N, 1), lambda k1: (k1, 0, 0))
    out = pl.BlockSpec((1, DFT_N, c), lambda k1: (k1, 0, 0))
    return pl.pallas_call(
        _spectrum_kernel,
        out_shape=[jax.ShapeDtypeStruct((DFT_HALF, DFT_N, c), F32)] * 2,
        grid=(DFT_HALF,),
        in_specs=[plane, plane, tw, tw, pl.BlockSpec((2 * DFT_N, 2 * DFT_N), lambda k1: (0, 0))],
        out_specs=[out, out],
        compiler_params=_cparams("parallel"),
        name="filter_spectrum",
    )(kr, ki, consts["tw_c"], consts["tw_s"], consts["g_fwd"])


CONV_UNROLL = 16
CONV_UNROLL_MID = 5
FFT_PITCH = 136


def _conv_tables(n1):
    length = DFT_N * DFT_N
    k1 = np.arange(DFT_HALF)[None, :, None]
    n = (DFT_N * np.arange(n1)[None, None, :] + np.arange(DFT_N)[:, None, None])
    th = 2.0 * np.pi * k1 * n / length
    pad = np.zeros((DFT_N, DFT_ROW_PAD - DFT_HALF, n1))
    fa = np.concatenate([np.cos(th), pad, -np.sin(th), pad], axis=1)
    wgt = np.full((1, 1, DFT_HALF), 2.0)
    wgt[..., 0] = wgt[..., -1] = 1.0
    tht = np.transpose(th, (0, 2, 1))
    padk = np.zeros((DFT_N, n1, DFT_ROW_PAD - DFT_HALF))
    fo_r = np.concatenate([wgt * np.cos(tht) / length, padk], axis=2)
    fo_i = np.concatenate([-wgt * np.sin(tht) / length, padk], axis=2)
    return (jnp.asarray(fa, BF16), jnp.asarray(fo_r, BF16), jnp.asarray(fo_i, BF16))


def _conv_fused_kernel(vx_ref, x0_ref, sk_ref, gf_ref, gi_ref, fa_hbm, for_hbm, foi_hbm, hr_hbm, hi_hbm, o_ref,
                       xs, ar, ai, fa_s, for_s, foi_s, hr_s, hi_s):
    ci = pl.program_id(0)
    bi = pl.program_id(1)
    lanes = vx_ref.shape[2]
    n1c = vx_ref.shape[1] // DFT_N

    @pl.when((ci == 0) & (bi == 0))
    def _():
        pltpu.sync_copy(fa_hbm, fa_s)
        pltpu.sync_copy(for_hbm, for_s)
        pltpu.sync_copy(foi_hbm, foi_s)

    @pl.when(bi == 0)
    def _():
        lane0 = pl.multiple_of(ci * lanes, lanes)
        pltpu.sync_copy(hr_hbm.at[:, :, pl.ds(lane0, lanes)], hr_s)
        pltpu.sync_copy(hi_hbm.at[:, :, pl.ds(lane0, lanes)], hi_s)

    for p in range(n1c):
        xs[pl.ds(p * FFT_PITCH, DFT_N), :] = vx_ref[0, pl.ds(p * DFT_N, DFT_N), :].astype(F32)

    def fwd(g, carry):
        n2s = [g * CONV_UNROLL + u for u in range(CONV_UNROLL)]
        xgs = [xs[pl.ds(n2, n1c, stride=FFT_PITCH), :].astype(BF16) for n2 in n2s]
        ys = [jnp.dot(fa_s[n2], xg, preferred_element_type=F32) for n2, xg in zip(n2s, xgs)]
        for n2, y in zip(n2s, ys):
            ar[pl.ds(n2, DFT_ROW_PAD, stride=FFT_PITCH), :] = y[:DFT_ROW_PAD]
            ai[pl.ds(n2, DFT_ROW_PAD, stride=FFT_PITCH), :] = y[DFT_ROW_PAD:]
        return carry
    lax.fori_loop(0, DFT_N // CONV_UNROLL, fwd, 0)

    def mid_one(a2, k1):
        x = jnp.dot(gf_ref[...], a2, preferred_element_type=F32)
        xr, xi = x[:DFT_N], x[DFT_N:]
        hr = hr_s[k1]
        hi = hi_s[k1]
        z = jnp.concatenate([xr * hr - xi * hi, xr * hi + xi * hr], axis=0).astype(BF16)
        return jnp.dot(gi_ref[...], z, preferred_element_type=F32)

    def mid(g, carry):
        k1s = [jnp.minimum(g * CONV_UNROLL_MID + u, DFT_HALF - 1) for u in range(CONV_UNROLL_MID)]
        bases = [pl.multiple_of(k1 * FFT_PITCH, 8) for k1 in k1s]
        a2s = [jnp.concatenate([ar[pl.ds(bs, DFT_N), :], ai[pl.ds(bs, DFT_N), :]], axis=0).astype(BF16)
               for bs in bases]
        ys = [mid_one(a2, k1) for a2, k1 in zip(a2s, k1s)]
        for bs, y in zip(bases, ys):
            ar[pl.ds(bs, DFT_N), :] = y[:DFT_N]
            ai[pl.ds(bs, DFT_N), :] = y[DFT_N:]
        return carry
    lax.fori_loop(0, -(-DFT_HALF // CONV_UNROLL_MID), mid, 0)

    def inv(g, carry):
        n2s = [g * CONV_UNROLL + u for u in range(CONV_UNROLL)]
        brs = [ar[pl.ds(n2, DFT_ROW_PAD, stride=FFT_PITCH), :].astype(BF16) for n2 in n2s]
        bis = [ai[pl.ds(n2, DFT_ROW_PAD, stride=FFT_PITCH), :].astype(BF16) for n2 in n2s]
        ys = [jnp.dot(for_s[n2], br, preferred_element_type=F32) + jnp.dot(foi_s[n2], bi_, preferred_element_type=F32)
              for n2, br, bi_ in zip(n2s, brs, bis)]
        for n2, y in zip(n2s, ys):
            xs[pl.ds(n2, n1c, stride=FFT_PITCH), :] = y
        return carry
    lax.fori_loop(0, DFT_N // CONV_UNROLL, inv, 0)

    sk = sk_ref[...]
    for p in range(n1c):
        rows = pl.ds(p * DFT_N, DFT_N)
        conv = xs[pl.ds(p * FFT_PITCH, DFT_N), :]
        o_ref[0, rows, :] = (x0_ref[0, rows, :].astype(F32)
                             * (conv + vx_ref[0, rows, :].astype(F32) * sk)).astype(o_ref.dtype)


def _conv_fused(x0, vx, hr, hi, skip, g_fwd, g_inv):
    b, n, c = vx.shape
    n1 = n // DFT_N
    lanes = LANE
    fa, fo_r, fo_i = _conv_tables(n1)
    blk = pl.BlockSpec((1, n, lanes), lambda ci, bi: (bi, 0, ci))
    full = lambda a: pl.BlockSpec(a.shape, lambda ci, bi: (0,) * a.ndim)
    hbm = pl.BlockSpec(memory_space=pl.ANY)
    return pl.pallas_call(
        _conv_fused_kernel,
        out_shape=jax.ShapeDtypeStruct((b, n, c), BF16),
        grid=(c // lanes, b),
        in_specs=[blk, blk, pl.BlockSpec((1, lanes), lambda ci, bi: (0, ci)), full(g_fwd), full(g_inv),
                  hbm, hbm, hbm, hbm, hbm],
        out_specs=blk,
        scratch_shapes=[
            pltpu.VMEM((n1 * FFT_PITCH, lanes), F32),
            pltpu.VMEM((DFT_ROW_PAD * FFT_PITCH, lanes), F32),
            pltpu.VMEM((DFT_ROW_PAD * FFT_PITCH, lanes), F32),
            pltpu.VMEM(fa.shape, BF16), pltpu.VMEM(fo_r.shape, BF16), pltpu.VMEM(fo_i.shape, BF16),
            pltpu.VMEM((DFT_HALF, DFT_N, lanes), F32), pltpu.VMEM((DFT_HALF, DFT_N, lanes), F32),
        ],
        compiler_params=_cparams("arbitrary", "arbitrary"),
        name="long_conv_fused",
    )(vx, x0, skip.reshape(1, c), g_fwd, g_inv, fa, fo_r, fo_i, hr, hi)


def _stack_rows(fr, fi):
    pad = np.zeros((DFT_ROW_PAD - DFT_HALF, fr.shape[1]))
    return jnp.asarray(np.concatenate([fr, pad, fi, pad], axis=0), BF16)


def _hyena_long_conv(x0, vx, kfull, skip):
    c = vx.shape[-1]
    cn = _dft_consts()
    bf = lambda a: jnp.asarray(a, BF16)
    f32 = lambda a: jnp.asarray(a, F32)
    consts = dict(tw_c=f32(cn["tw_c"]), tw_s=f32(cn["tw_s"]), g_fwd=bf(cn["g_fwd"]), g_inv=bf(cn["g_inv"]))
    kr, ki = _dft_a(_stack_rows(cn["fa_r"], cn["fa_i"]), kfull.reshape(1, DFT_N, DFT_N, c), F32)
    hr, hi = _filter_spectrum(kr, ki, consts)
    return _conv_fused(x0, vx, hr, hi, skip, consts["g_fwd"], consts["g_inv"])


def _l1c_kernel(y_ref, h_ref, g1_ref, sc2_ref, sh2_ref, w_ref, lg_ref, lb_ref, rw_ref, ho_ref, z_ref, lo_ref):
    y = jnp.dot(y_ref[0], w_ref[...], preferred_element_type=F32)
    hh = _layer_norm(ALPHA * h_ref[0] + g1_ref[0] * y, lg_ref[...], lb_ref[...])
    ho_ref[0] = hh
    z = hh * (1.0 + sc2_ref[0]) + sh2_ref[0]
    for s in range(z_ref.shape[2]):
        z_ref[0, :, s, :] = z[:, s * LANE:(s + 1) * LANE]
    lo_ref[0] = jnp.dot(z, rw_ref[...], preferred_element_type=F32, precision=HIGHEST)


def _l1c(y, h, g1, sc2, sh2, out_w, ln_g, ln_b, router_w):
    b, n, d = h.shape
    tm = 512
    blk = pl.BlockSpec((1, tm, d), lambda bi, i: (bi, i, 0))
    vec = pl.BlockSpec((1, 1, d), lambda bi, i: (bi, 0, 0))
    full = lambda shape: pl.BlockSpec(shape, lambda bi, i: (0,) * len(shape))
    rw = jnp.pad(router_w, ((0, 0), (0, LANE - router_w.shape[1])))
    return pl.pallas_call(
        _l1c_kernel,
        out_shape=[jax.ShapeDtypeStruct((b, n, d), F32), jax.ShapeDtypeStruct((b, n, d // LANE, LANE), F32),
                   jax.ShapeDtypeStruct((b, n, LANE), F32)],
        grid=(b, n // tm),
        in_specs=[blk, blk, vec, vec, vec, full((d, d)), full((1, d)), full((1, d)), full((d, LANE))],
        out_specs=[blk, pl.BlockSpec((1, tm, d // LANE, LANE), lambda bi, i: (bi, i, 0, 0)),
                   pl.BlockSpec((1, tm, LANE), lambda bi, i: (bi, i, 0))],
        compiler_params=_cparams("parallel", "parallel"),
        name="hyena_out_ln_router",
    )(y, h, g1, sc2, sh2, out_w.astype(BF16), ln_g.reshape(1, -1), ln_b.reshape(1, -1), rw)


def _take(a, idx):
    return a.at[idx].get(mode="promise_in_bounds")


def _route(logits, tm):
    t = logits.shape[0]
    top_v, top_i = lax.top_k(logits, TOP_K)
    gates = jax.nn.softmax(top_v, axis=-1)
    flat_e = top_i.reshape(-1).astype(jnp.int32)
    flat_g = gates.reshape(-1)
    n_sel = t * TOP_K
    p = n_sel + N_EXPERTS * tm
    eids = jnp.arange(N_EXPERTS, dtype=jnp.int32)[None, :]
    onehot = (flat_e[:, None] == eids).astype(jnp.int32)
    csum = jnp.cumsum(onehot, axis=0)
    counts = csum[-1]
    rank = jnp.sum((csum - onehot) * onehot, axis=1)
    padded = ((counts + tm - 1) // tm) * tm
    end_p = jnp.cumsum(padded)
    start_p = end_p - padded
    start = jnp.cumsum(counts) - counts
    pos = (jnp.sum(onehot * start_p[None, :], axis=1) + rank).reshape(t, TOP_K)
    order = jnp.argsort(flat_e, stable=True).astype(jnp.int32)
    r = jnp.arange(p, dtype=jnp.int32)
    e_row = jnp.sum(r[:, None] >= end_p[None, :], axis=1).astype(jnp.int32)
    oh_r = (jnp.minimum(e_row, N_EXPERTS - 1)[:, None] == eids).astype(jnp.int32)
    j = r - jnp.sum(oh_r * start_p[None, :], axis=1)
    valid = (e_row < N_EXPERTS) & (j < jnp.sum(oh_r * counts[None, :], axis=1))
    src = jnp.clip(jnp.sum(oh_r * start[None, :], axis=1) + j, 0, n_sel - 1)
    flat_idx = _take(order, src)
    row_token = jnp.where(valid, flat_idx // TOP_K, 0)
    row_gate = jnp.where(valid, _take(flat_g, flat_idx), 0.0)
    tile_start = jnp.arange(p // tm, dtype=jnp.int32) * tm
    tile_expert = jnp.sum(tile_start[:, None] >= end_p[None, :], axis=1).astype(jnp.int32)
    tile_valid = (tile_expert < N_EXPERTS).astype(jnp.int32)
    tile_expert = jnp.minimum(tile_expert, N_EXPERTS - 1)
    return row_token, row_gate[:, None], tile_expert, tile_valid, pos


def kernel(x, c, ctx, c_ctx, ada_w, ada_b, ln_g, ln_b, mix_in_w, pool_w, pool_scale, q_norm, q_up, kv_norm, kv_up, mix_out_w, ffn_gate, ffn_up, ffn_down, hy_in_w, hy_conv_w, hy_conv_b, hy_fw1, hy_fb1, hy_fw2, hy_fb2, hy_fw3, hy_fb3, hy_fout, hy_freq, hy_skip, hy_out_w, router_w, moe_gate, moe_up, moe_down):
    b, n, d = x.shape
    t = b * n
    assert b + 1 <= 8
    s_rows = jnp.concatenate([c, c_ctx[None, :], jnp.zeros((8 - b - 1, d), F32)], axis=0)
    mod = _ada(s_rows, ada_w, ada_b)

    def chunks(l, rows):
        m = mod[l, rows].reshape(-1, 6, d)
        return [m[:, k][:, None, :] for k in range(6)]

    sh1, sc1, g1, sh2, sc2, g2 = chunks(0, slice(0, b))
    shc, scc = chunks(0, slice(b, b + 1))[:2]
    pool_u, q, k, v = _l0a(x, ctx, sc1, sh1, scc, shc, mix_in_w[0], q_norm[0], q_up[0], kv_norm[0], kv_up[0])
    attn_o = _attention(q, k, v, n)
    h1, z1 = _l0b(pool_u, attn_o, x, g1, sc2, sh2, pool_w[0], pool_scale[0], mix_out_w[0], ln_g[0, 0], ln_b[0, 0])
    f0 = _ffn_dense(z1.reshape(t, d), ffn_gate.astype(BF16), ffn_up.astype(BF16), ffn_down.astype(BF16), 512, 1408)
    sh1, sc1, g1b, sh2b, sc2b, g2b = chunks(1, slice(0, b))
    h2, u2 = _resid_ln(h1, [f0.reshape(b, n, d)], g2, ln_g[0, 1], ln_b[0, 1], mod=(sc1, sh1))

    x0, vx = _l1a(u2, hy_in_w[0], hy_conv_w[0], hy_conv_b[0])
    kfull = _hyena_filters(n, hy_fw1[0], hy_fb1[0], hy_fw2[0], hy_fb2[0], hy_fw3[0], hy_fb3[0],
                           hy_fout[0], hy_freq[0])
    yl = _hyena_long_conv(x0, vx, kfull, hy_skip[0])
    h3, z3, logits = _l1c(yl, h2, g1b, sc2b, sh2b, hy_out_w[0], ln_g[1, 0], ln_b[1, 0], router_w[0])

    tm_moe = 1024
    row_token, row_gate, tile_expert, tile_valid, pos = _route(logits.reshape(t, LANE)[:, :N_EXPERTS], tm_moe)
    ys = _moe_grouped(z3.reshape(t, d // LANE, LANE), row_token, moe_gate[0], moe_up[0], moe_down[0], row_gate, tile_expert,
                      tile_valid, tm_moe, 512)
    y_a = _take(ys, pos[:, 0]).reshape(b, n, d)
    y_b = _take(ys, pos[:, 1]).reshape(b, n, d)
    (h4,) = _resid_ln(h3, [y_a, y_b], g2b, ln_g[1, 1], ln_b[1, 1])
    return h4
```

```python
import functools
import math

import numpy as np
import jax
import jax.numpy as jnp
from jax import lax
from jax.experimental import pallas as pl
from jax.experimental.pallas import tpu as pltpu

F32 = jnp.float32
BF16 = jnp.bfloat16
HIGHEST = lax.Precision.HIGHEST

D_MODEL = 1024
GRID_W = 64
POOL_W = 512
POOL_WINDOWS = (2, 4, 8, 16)
POOL_GC = 128
MLA_HEADS = 8
QK_NOPE = 64
QK_ROPE = 32
V_HEAD = 64
Q_LORA = 256
KV_LORA = 256
ROPE_AXIS = 16
ROPE_BASE = 10000.0
N_EXPERTS = 8
TOP_K = 2
HY_BANDS = 16
HY_EMB = 1 + 2 * HY_BANDS
HY_FAST_DECAY = 0.3
HY_SLOW_DECAY = 1.5
HY_TARGET = 1e-2
HY_MIN_DECAY = math.log(HY_TARGET) / HY_SLOW_DECAY
HY_MAX_DECAY = math.log(HY_TARGET) / HY_FAST_DECAY
LN_EPS = 1e-5
RMS_EPS = 1e-6
DEPTH = 2
ALPHA = (2.0 * DEPTH) ** 0.25

LANE = 128
HEAD_SLOT = 128
DFT_N = 128
DFT_HALF = DFT_N // 2 + 1
VMEM_LIMIT = 56 * 1024 * 1024
ATTN_LOOKAHEAD = 2


def _cparams(*sem):
    return pltpu.CompilerParams(dimension_semantics=sem, vmem_limit_bytes=VMEM_LIMIT)


def _layer_norm(v, g, b):
    mu = jnp.mean(v, axis=-1, keepdims=True)
    c = v - mu
    var = jnp.mean(c * c, axis=-1, keepdims=True)
    return c * lax.rsqrt(var + LN_EPS) * g + b


def _rms_norm(v, g):
    return v * lax.rsqrt(jnp.mean(v * v, axis=-1, keepdims=True) + RMS_EPS) * g


def _ada_kernel(s_ref, w_ref, b_ref, o_ref):
    s = s_ref[...]
    s = s * jax.nn.sigmoid(s)
    o_ref[0] = jnp.dot(s, w_ref[0], preferred_element_type=F32, precision=HIGHEST) + b_ref[0]


def _ada(s_rows, ada_w, ada_b):
    depth, d, n6 = ada_w.shape
    tn = 768
    return pl.pallas_call(
        _ada_kernel,
        out_shape=jax.ShapeDtypeStruct((depth, 8, n6), F32),
        grid=(depth, n6 // tn),
        in_specs=[
            pl.BlockSpec((8, d), lambda l, j: (0, 0)),
            pl.BlockSpec((1, d, tn), lambda l, j: (l, 0, j)),
            pl.BlockSpec((1, 1, tn), lambda l, j: (l, 0, j)),
        ],
        out_specs=pl.BlockSpec((1, 8, tn), lambda l, j: (l, 0, j)),
        compiler_params=_cparams("parallel", "parallel"),
        name="ada_mod",
    )(s_rows, ada_w, ada_b.reshape(depth, 1, n6))


def _l0a_kernel(x_ref, c_ref, sc_ref, sh_ref, scc_ref, shc_ref, win_ref, qn_ref, kvn_ref,
                wqa_ref, wqb_ref, wk_ref, wv_ref, cos_ref, sin_ref, cost_ref, sint_ref,
                pu_ref, q_ref, k_ref, v_ref, u_scr, *, n_lat_tiles):
    i = pl.program_id(1)

    @pl.when(i < n_lat_tiles)
    def _():
        u_scr[...] = (x_ref[0] * (1.0 + sc_ref[0]) + sh_ref[0]).astype(BF16)

    @pl.when(i >= n_lat_tiles)
    def _():
        u_scr[...] = (c_ref[0] * (1.0 + scc_ref[0]) + shc_ref[0]).astype(BF16)

    proj = jnp.dot(u_scr[...], win_ref[...], preferred_element_type=F32)
    pu_ref[0] = proj[:, :POOL_W]
    cos = cos_ref[...]
    sin = sin_ref[...]
    q0 = POOL_W
    kv0 = POOL_W + Q_LORA
    r0 = kv0 + KV_LORA
    qn = _rms_norm(proj[:, q0:kv0], qn_ref[...]).astype(BF16)
    kvn = _rms_norm(proj[:, kv0:r0], kvn_ref[...]).astype(BF16)
    nt_dims = (((1,), (1,)), ((), ()))
    qa = lax.dot_general(wqa_ref[...], qn, nt_dims, preferred_element_type=F32)
    qb = lax.dot_general(wqb_ref[...], qn, nt_dims, preferred_element_type=F32)
    vt = lax.dot_general(wv_ref[...], kvn, nt_dims, preferred_element_type=F32)
    kn = jnp.dot(kvn, wk_ref[...], preferred_element_type=F32)
    krot = proj[:, r0:r0 + HEAD_SLOT] * cos + proj[:, r0 + HEAD_SLOT:r0 + 2 * HEAD_SLOT] * sin
    cos_t = cost_ref[...]
    sin_t = sint_ref[...]
    for h in range(MLA_HEADS):
        sl = slice(h * HEAD_SLOT, (h + 1) * HEAD_SLOT)
        q_ref[0, h] = (qa[sl] * cos_t + qb[sl] * sin_t).astype(BF16)
        k_ref[0, h] = (kn[:, sl] + krot).astype(BF16)
        v_ref[0, h] = vt[h * V_HEAD:(h + 1) * V_HEAD].astype(BF16)


def _rope_swap_index():
    half = ROPE_AXIS // 2
    idx = []
    for a in range(2):
        base = a * ROPE_AXIS
        idx += list(range(base + half, base + ROPE_AXIS)) + list(range(base, base + half))
    return np.array(idx)


def _rope_tables(n, n_ctx):
    rows = n // GRID_W
    r = jnp.repeat(jnp.arange(rows, dtype=F32), GRID_W)
    col = jnp.tile(jnp.arange(GRID_W, dtype=F32), rows)
    inv = ROPE_BASE ** (-jnp.arange(0, ROPE_AXIS, 2, dtype=F32) / ROPE_AXIS)
    ang_r = r[:, None] * inv
    ang_c = col[:, None] * inv
    cos32 = jnp.concatenate([jnp.cos(ang_r), jnp.cos(ang_r), jnp.cos(ang_c), jnp.cos(ang_c)], axis=-1)
    sin32 = jnp.concatenate([-jnp.sin(ang_r), jnp.sin(ang_r), -jnp.sin(ang_c), jnp.sin(ang_c)], axis=-1)
    pad = HEAD_SLOT - QK_NOPE - QK_ROPE
    cos = jnp.concatenate([jnp.ones((n, QK_NOPE), F32), cos32, jnp.ones((n, pad), F32)], axis=-1)
    sin = jnp.concatenate([jnp.zeros((n, QK_NOPE), F32), sin32, jnp.zeros((n, pad), F32)], axis=-1)
    cos = jnp.concatenate([cos, jnp.ones((n_ctx, HEAD_SLOT), F32)], axis=0)
    sin = jnp.concatenate([sin, jnp.zeros((n_ctx, HEAD_SLOT), F32)], axis=0)
    return cos, sin


def _l0a_weights(in_w, q_up, kv_up):
    swap = _rope_swap_index()
    d = in_w.shape[0]
    r0 = POOL_W + Q_LORA + KV_LORA
    w_rope = in_w[:, r0:]
    pad_l = jnp.zeros((d, QK_NOPE), F32)
    pad_r = jnp.zeros((d, HEAD_SLOT - QK_NOPE - QK_ROPE), F32)
    kr_a = jnp.concatenate([pad_l, w_rope, pad_r], axis=1)
    kr_b = jnp.concatenate([pad_l, w_rope[:, swap], pad_r], axis=1)
    w_in = jnp.concatenate([in_w[:, :r0], kr_a, kr_b], axis=1).astype(BF16)

    scale = (QK_NOPE + QK_ROPE) ** -0.5 * math.log2(math.e)
    qu = q_up.reshape(Q_LORA, MLA_HEADS, QK_NOPE + QK_ROPE) * scale
    zpad = jnp.zeros((Q_LORA, MLA_HEADS, HEAD_SLOT - QK_NOPE - QK_ROPE), F32)
    wq_a = jnp.concatenate([qu, zpad], axis=-1).reshape(Q_LORA, MLA_HEADS * HEAD_SLOT).T.astype(BF16)
    wq_b = jnp.concatenate([jnp.zeros((Q_LORA, MLA_HEADS, QK_NOPE), F32), qu[..., QK_NOPE:][..., swap], zpad],
                           axis=-1).reshape(Q_LORA, MLA_HEADS * HEAD_SLOT).T.astype(BF16)
    kvu = kv_up.reshape(KV_LORA, MLA_HEADS, QK_NOPE + V_HEAD)
    wk = jnp.concatenate([kvu[..., :QK_NOPE], jnp.zeros((KV_LORA, MLA_HEADS, HEAD_SLOT - QK_NOPE), F32)],
                         axis=-1).reshape(KV_LORA, MLA_HEADS * HEAD_SLOT).astype(BF16)
    wv_t = kvu[..., QK_NOPE:].reshape(KV_LORA, MLA_HEADS * V_HEAD).T.astype(BF16)
    return w_in, wq_a, wq_b, wk, wv_t


def _l0a(x, ctx, sc, sh, scc, shc, in_w, q_norm, q_up, kv_norm, kv_up):
    b, n, d = x.shape
    n_ctx = ctx.shape[1]
    tm = 256
    nl = n // tm
    nt = (n + n_ctx) // tm
    w_in, wq_a, wq_b, wk, wv_t = _l0a_weights(in_w, q_up, kv_up)
    cos, sin = _rope_tables(n, n_ctx)
    hw = MLA_HEADS * HEAD_SLOT
    full = lambda shape: pl.BlockSpec(shape, lambda bi, i: (0,) * len(shape))
    vec = pl.BlockSpec((1, 1, d), lambda bi, i: (bi, 0, 0))
    return pl.pallas_call(
        functools.partial(_l0a_kernel, n_lat_tiles=nl),
        out_shape=[
            jax.ShapeDtypeStruct((b, n + n_ctx, POOL_W), F32),
            jax.ShapeDtypeStruct((b, MLA_HEADS, HEAD_SLOT, n + n_ctx), BF16),
            jax.ShapeDtypeStruct((b, MLA_HEADS, n + n_ctx, HEAD_SLOT), BF16),
            jax.ShapeDtypeStruct((b, MLA_HEADS, V_HEAD, n + n_ctx), BF16),
        ],
        grid=(b, nt),
        in_specs=[
            pl.BlockSpec((1, tm, d), lambda bi, i: (bi, jnp.minimum(i, nl - 1), 0)),
            pl.BlockSpec((1, tm, d), lambda bi, i: (bi, jnp.maximum(i - nl, 0), 0)),
            vec, vec, full((1, 1, d)), full((1, 1, d)),
            full(w_in.shape), full((1, Q_LORA)), full((1, KV_LORA)),
            full((hw, Q_LORA)), full((hw, Q_LORA)), full((KV_LORA, hw)), full((MLA_HEADS * V_HEAD, KV_LORA)),
            pl.BlockSpec((tm, HEAD_SLOT), lambda bi, i: (i, 0)),
            pl.BlockSpec((tm, HEAD_SLOT), lambda bi, i: (i, 0)),
            pl.BlockSpec((HEAD_SLOT, tm), lambda bi, i: (0, i)),
            pl.BlockSpec((HEAD_SLOT, tm), lambda bi, i: (0, i)),
        ],
        out_specs=[
            pl.BlockSpec((1, tm, POOL_W), lambda bi, i: (bi, i, 0)),
            pl.BlockSpec((1, MLA_HEADS, HEAD_SLOT, tm), lambda bi, i: (bi, 0, 0, i)),
            pl.BlockSpec((1, MLA_HEADS, tm, HEAD_SLOT), lambda bi, i: (bi, 0, i, 0)),
            pl.BlockSpec((1, MLA_HEADS, V_HEAD, tm), lambda bi, i: (bi, 0, 0, i)),
        ],
        scratch_shapes=[pltpu.VMEM((tm, d), BF16)],
        compiler_params=_cparams("parallel", "arbitrary"),
        name="l0_in_proj",
    )(x, ctx, sc, sh, scc, shc, w_in, q_norm.reshape(1, -1), kv_norm.reshape(1, -1),
      wq_a, wq_b, wk, wv_t, cos, sin, cos.T, sin.T)


def _attn_kernel(q_ref, k_ref, v_ref, o_ref, m_ref, l_ref, acc_ref, *, nk):
    ki = pl.program_id(2)

    @pl.when(ki == 0)
    def _():
        m_ref[...] = jnp.full(m_ref.shape, -jnp.inf, F32)
        l_ref[...] = jnp.zeros(l_ref.shape, F32)
        acc_ref[...] = jnp.zeros(acc_ref.shape, F32)

    def scores(h):
        return jnp.dot(k_ref[0, h], q_ref[0, h], preferred_element_type=F32)

    pending = [scores(h) for h in range(ATTN_LOOKAHEAD)]
    for h in range(MLA_HEADS):
        if h + ATTN_LOOKAHEAD < MLA_HEADS:
            pending.append(scores(h + ATTN_LOOKAHEAD))
        st = pending.pop(0)
        m_prev = m_ref[h]
        m_new = jnp.maximum(m_prev, jnp.max(st, axis=0, keepdims=True))
        a = jnp.exp2(m_prev - m_new)
        p = jnp.exp2(st - m_new)
        l_ref[h] = a * l_ref[h] + jnp.sum(p, axis=0, keepdims=True)
        pv = jnp.dot(v_ref[0, h], p.astype(BF16), preferred_element_type=F32)
        rows = slice(h * V_HEAD, (h + 1) * V_HEAD)
        acc_ref[rows, :] = a * acc_ref[rows, :] + pv
        m_ref[h] = m_new

    @pl.when(ki == nk - 1)
    def _():
        for h in range(MLA_HEADS):
            rows = slice(h * V_HEAD, (h + 1) * V_HEAD)
            acc_ref[rows, :] = acc_ref[rows, :] / l_ref[h]
        o_ref[0] = acc_ref[...].T.astype(o_ref.dtype)


def _attention(q, k, vt, n):
    b, h, n_all, _ = k.shape
    tq, tk = 512, 1408
    nk = n_all // tk
    return pl.pallas_call(
        functools.partial(_attn_kernel, nk=nk),
        out_shape=jax.ShapeDtypeStruct((b, n, h * V_HEAD), BF16),
        grid=(b, n // tq, nk),
        in_specs=[
            pl.BlockSpec((1, h, HEAD_SLOT, tq), lambda bi, qi, ki: (bi, 0, 0, qi)),
            pl.BlockSpec((1, h, tk, HEAD_SLOT), lambda bi, qi, ki: (bi, 0, ki, 0)),
            pl.BlockSpec((1, h, V_HEAD, tk), lambda bi, qi, ki: (bi, 0, 0, ki)),
        ],
        out_specs=pl.BlockSpec((1, tq, h * V_HEAD), lambda bi, qi, ki: (bi, qi, 0)),
        scratch_shapes=[
            pltpu.VMEM((h, 1, tq), F32),
            pltpu.VMEM((h, 1, tq), F32),
            pltpu.VMEM((h * V_HEAD, tq), F32),
        ],
        compiler_params=_cparams("parallel", "parallel", "arbitrary"),
        name="mla_attention",
    )(q, k, vt)


def _l0b_kernel(pp_ref, pc_ref, pn_ref, o_ref, x_ref, g1_ref, sc2_ref, sh2_ref, pw_ref, ps_ref, ow_ref,
                lg_ref, lb_ref, h_ref, z_ref, *, tm, n):
    i = pl.program_id(1)
    halo = 8
    ext = jnp.concatenate([pp_ref[0], pc_ref[0], pn_ref[0]], axis=0)
    pos = i * tm - halo + lax.broadcasted_iota(jnp.int32, (tm + 2 * halo, 1), 0)
    ext = jnp.where((pos >= 0) & (pos < n), ext, 0.0)
    t = i * tm + lax.broadcasted_iota(jnp.int32, (tm, 1), 0)
    y = jnp.dot(o_ref[0], ow_ref[POOL_W:, :], preferred_element_type=F32)
    for g, w in enumerate(POOL_WINDOWS):
        hw = w // 2
        e = ext[:, g * POOL_GC:(g + 1) * POOL_GC]
        s = e
        width = 1
        while width < w:
            s = s[:s.shape[0] - width] + s[width:]
            width *= 2
        win = s[halo - hw:halo - hw + tm]
        cnt = (jnp.minimum(t + hw, n) - jnp.maximum(t - hw, 0)).astype(F32)
        dd = (win / cnt - e[halo:halo + tm]).astype(BF16)
        yg = jnp.dot(dd, pw_ref[g], preferred_element_type=F32) * ps_ref[:, g * POOL_GC:(g + 1) * POOL_GC]
        y = y + jnp.dot(yg.astype(BF16), ow_ref[g * POOL_GC:(g + 1) * POOL_GC, :], preferred_element_type=F32)
    hh = _layer_norm(ALPHA * x_ref[0] + g1_ref[0] * y, lg_ref[...], lb_ref[...])
    h_ref[0] = hh
    z_ref[0] = (hh * (1.0 + sc2_ref[0]) + sh2_ref[0]).astype(BF16)


def _l0b(pool_u, attn_o, x, g1, sc2, sh2, pool_w, pool_scale, out_w, ln_g, ln_b):
    b, n, d = x.shape
    tm = 512
    hb = tm // 8
    vec = pl.BlockSpec((1, 1, d), lambda bi, i: (bi, 0, 0))
    full = lambda shape: pl.BlockSpec(shape, lambda bi, i: (0,) * len(shape))
    return pl.pallas_call(
        functools.partial(_l0b_kernel, tm=tm, n=n),
        out_shape=[jax.ShapeDtypeStruct((b, n, d), F32), jax.ShapeDtypeStruct((b, n, d), BF16)],
        grid=(b, n // tm),
        in_specs=[
            pl.BlockSpec((1, 8, POOL_W), lambda bi, i: (bi, jnp.maximum(i * hb - 1, 0), 0)),
            pl.BlockSpec((1, tm, POOL_W), lambda bi, i: (bi, i, 0)),
            pl.BlockSpec((1, 8, POOL_W), lambda bi, i: (bi, (i + 1) * hb, 0)),
            pl.BlockSpec((1, tm, POOL_W), lambda bi, i: (bi, i, 0)),
            pl.BlockSpec((1, tm, d), lambda bi, i: (bi, i, 0)),
            vec, vec, vec,
            full(pool_w.shape), full((1, POOL_W)), full(out_w.shape), full((1, d)), full((1, d)),
        ],
        out_specs=[pl.BlockSpec((1, tm, d), lambda bi, i: (bi, i, 0)),
                   pl.BlockSpec((1, tm, d), lambda bi, i: (bi, i, 0))],
        compiler_params=_cparams("parallel", "parallel"),
        name="l0_pool_out_ln",
    )(pool_u, pool_u, pool_u, attn_o, x, g1, sc2, sh2, pool_w.astype(BF16), pool_scale.reshape(1, -1),
      out_w.astype(BF16), ln_g.reshape(1, -1), ln_b.reshape(1, -1))


def _swiglu_step(z, wg_ref, wu_ref, wd_ref, acc_ref):
    g = jnp.dot(z, wg_ref[0].astype(BF16), preferred_element_type=F32)
    u = jnp.dot(z, wu_ref[0].astype(BF16), preferred_element_type=F32)
    a = (g * jax.nn.sigmoid(g) * u).astype(BF16)
    acc_ref[...] += jnp.dot(a, wd_ref[0].astype(BF16), preferred_element_type=F32)


def _ffn_kernel(z_ref, wg_ref, wu_ref, wd_ref, o_ref, acc_ref, *, nj):
    j = pl.program_id(1)

    @pl.when(j == 0)
    def _():
        acc_ref[...] = jnp.zeros(acc_ref.shape, F32)

    _swiglu_step(z_ref[...], wg_ref, wu_ref, wd_ref, acc_ref)

    @pl.when(j == nj - 1)
    def _():
        o_ref[...] = acc_ref[...]


def _ffn_dense(z, wg, wu, wd, tm, tf):
    t, d = z.shape
    f = wg.shape[-1]
    nj = f // tf
    return pl.pallas_call(
        functools.partial(_ffn_kernel, nj=nj),
        out_shape=jax.ShapeDtypeStruct((t, d), F32),
        grid=(t // tm, nj),
        in_specs=[
            pl.BlockSpec((tm, d), lambda i, j: (i, 0)),
            pl.BlockSpec((1, d, tf), lambda i, j: (0, 0, j)),
            pl.BlockSpec((1, d, tf), lambda i, j: (0, 0, j)),
            pl.BlockSpec((1, tf, d), lambda i, j: (0, j, 0)),
        ],
        out_specs=pl.BlockSpec((tm, d), lambda i, j: (i, 0)),
        scratch_shapes=[pltpu.VMEM((tm, d), F32)],
        compiler_params=_cparams("parallel", "arbitrary"),
        name="swiglu_dense",
    )(z, wg, wu, wd)


def _cast_kernel(x_ref, o_ref):
    o_ref[...] = x_ref[...].astype(o_ref.dtype)


def _cast_bf16(w):
    e, r, c = w.shape
    tr = 512
    blk = pl.BlockSpec((1, tr, c), lambda ei, ri: (ei, ri, 0))
    return pl.pallas_call(
        _cast_kernel,
        out_shape=jax.ShapeDtypeStruct(w.shape, BF16),
        grid=(e, r // tr),
        in_specs=[blk],
        out_specs=blk,
        compiler_params=_cparams("parallel", "parallel"),
        name="cast_bf16",
    )(w)


def _moe_kernel(te_ref, tv_ref, tok0_ref, tokn_ref, z_hbm, wg_ref, wu_ref, wd_ref, gate_ref, o_ref,
                zbuf, zb16, sem, acc_ref, *, nj, tm, n_tiles):
    i = pl.program_id(0)
    j = pl.program_id(1)
    slot = lax.rem(i, 2)
    nxt = 1 - slot
    segs = z_hbm.shape[1]
    per_step = zbuf.shape[1] // (segs * nj)

    def row_copy(tok_ref, row, s):
        tok = tok_ref[0, 0, jnp.minimum(row, tm - 1)]
        return pltpu.make_async_copy(z_hbm.at[tok], zbuf.at[s, pl.ds(row * segs, segs), :], sem.at[s])

    def wait_slot(s):
        pltpu.make_async_copy(zbuf.at[s], zbuf.at[s], sem.at[s]).wait()

    def issue_next():
        for r in range(per_step):
            row_copy(tokn_ref, j * per_step + r, nxt).start()

    @pl.when((i == 0) & (j == 0))
    def _():
        def body(r, carry):
            row_copy(tok0_ref, r, 0).start()
            return carry
        lax.fori_loop(0, per_step * nj, body, 0)

    @pl.when(j == 0)
    def _():
        wait_slot(slot)
        for g in range(segs):
            zb16[:, g * LANE:(g + 1) * LANE] = zbuf.at[slot][pl.ds(g, tm, stride=segs), :].astype(BF16)
        acc_ref[...] = jnp.zeros(acc_ref.shape, F32)

    @pl.when(tv_ref[i] > 0)
    def _():
        issue_next()
        _swiglu_step(zb16[...], wg_ref, wu_ref, wd_ref, acc_ref)

    @pl.when(tv_ref[i] == 0)
    def _():
        issue_next()

    @pl.when(j == nj - 1)
    def _():
        o_ref[...] = acc_ref[...] * gate_ref[...]

    @pl.when((i == n_tiles - 1) & (j == nj - 1))
    def _():
        wait_slot(nxt)


def _moe_grouped(z, row_token, wg, wu, wd, gate, tile_expert, tile_valid, tm, tf):
    t, segs, _ = z.shape
    d = segs * LANE
    p = row_token.shape[0]
    n_tiles = p // tm
    f = wg.shape[-1]
    nj = f // tf
    tok = row_token.reshape(n_tiles, 1, tm)
    buf_rows = nj * (-(-tm // (8 * nj)) * 8)
    smem_blk = lambda imap: pl.BlockSpec((1, 1, tm), imap, memory_space=pltpu.SMEM)
    return pl.pallas_call(
        functools.partial(_moe_kernel, nj=nj, tm=tm, n_tiles=n_tiles),
        out_shape=jax.ShapeDtypeStruct((p, d), F32),
        grid_spec=pltpu.PrefetchScalarGridSpec(
            num_scalar_prefetch=2,
            grid=(n_tiles, nj),
            in_specs=[
                smem_blk(lambda i, j, te, tv: (0, 0, 0)),
                smem_blk(lambda i, j, te, tv: (jnp.minimum(i + 1, n_tiles - 1), 0, 0)),
                pl.BlockSpec(memory_space=pl.ANY),
                pl.BlockSpec((1, d, tf), lambda i, j, te, tv: (te[i], 0, j)),
                pl.BlockSpec((1, d, tf), lambda i, j, te, tv: (te[i], 0, j)),
                pl.BlockSpec((1, tf, d), lambda i, j, te, tv: (te[i], j, 0)),
                pl.BlockSpec((tm, 1), lambda i, j, te, tv: (i, 0)),
            ],
            out_specs=pl.BlockSpec((tm, d), lambda i, j, te, tv: (i, 0)),
            scratch_shapes=[
                pltpu.VMEM((2, buf_rows * segs, LANE), F32),
                pltpu.VMEM((tm, d), BF16),
                pltpu.SemaphoreType.DMA((2,)),
                pltpu.VMEM((tm, d), F32),
            ],
        ),
        compiler_params=_cparams("arbitrary", "arbitrary"),
        name="moe_swiglu_gather",
    )(tile_expert, tile_valid, tok, tok, z, wg, wu, wd, gate)


def _resid_ln_kernel(*refs, n_y, with_mod):
    h_ref = refs[0]
    y_refs = refs[1:1 + n_y]
    g_ref, lg_ref, lb_ref = refs[1 + n_y:4 + n_y]
    rest = refs[4 + n_y:]
    y = y_refs[0][0]
    for r in y_refs[1:]:
        y = y + r[0]
    hh = _layer_norm(ALPHA * h_ref[0] + g_ref[0] * y, lg_ref[...], lb_ref[...])
    if with_mod:
        sc_ref, sh_ref, ho_ref, u_ref = rest
        ho_ref[0] = hh
        u_ref[0] = (hh * (1.0 + sc_ref[0]) + sh_ref[0]).astype(BF16)
    else:
        rest[0][0] = hh


def _resid_ln(h, ys, g, ln_g, ln_b, mod=None):
    b, n, d = h.shape
    tm = 1024
    blk = pl.BlockSpec((1, tm, d), lambda bi, i: (bi, i, 0))
    vec = pl.BlockSpec((1, 1, d), lambda bi, i: (bi, 0, 0))
    row = pl.BlockSpec((1, d), lambda bi, i: (0, 0))
    in_specs = [blk] + [blk] * len(ys) + [vec, row, row]
    args = [h, *ys, g, ln_g.reshape(1, -1), ln_b.reshape(1, -1)]
    out_shape = [jax.ShapeDtypeStruct((b, n, d), F32)]
    out_specs = [blk]
    if mod is not None:
        in_specs += [vec, vec]
        args += list(mod)
        out_shape.append(jax.ShapeDtypeStruct((b, n, d), BF16))
        out_specs.append(blk)
    return pl.pallas_call(
        functools.partial(_resid_ln_kernel, n_y=len(ys), with_mod=mod is not None),
        out_shape=out_shape,
        grid=(b, n // tm),
        in_specs=in_specs,
        out_specs=out_specs,
        compiler_params=_cparams("parallel", "parallel"),
        name="resid_ln",
    )(*args)


def _l1a_kernel(up_ref, uc_ref, un_ref, w_ref, cw_ref, cb_ref, x0_ref, vx_ref, *, tm, n):
    i = pl.program_id(1)
    halo = 16
    ext = jnp.concatenate([up_ref[0], uc_ref[0], un_ref[0]], axis=0)
    pos = i * tm - halo + lax.broadcasted_iota(jnp.int32, (tm + 2 * halo, 1), 0)
    ext = jnp.where((pos >= 0) & (pos < n), ext, jnp.zeros_like(ext))
    z = jnp.dot(ext, w_ref[...], preferred_element_type=F32)
    cw = cw_ref[...]
    zc = (cw[0:1] * z[halo - 1:halo - 1 + tm] + cw[1:2] * z[halo:halo + tm]
          + cw[2:3] * z[halo + 1:halo + 1 + tm] + cb_ref[...])
    c = zc.shape[1] // 3
    x0_ref[0] = zc[:, :c].astype(x0_ref.dtype)
    vx_ref[0] = (zc[:, 2 * c:] * zc[:, c:2 * c]).astype(vx_ref.dtype)


def _l1a(u, hy_in_w, conv_w, conv_b):
    b, n, d = u.shape
    c3 = hy_in_w.shape[1]
    c = c3 // 3
    tm = 512
    hb = tm // 16
    nb16 = n // 16
    full = lambda shape: pl.BlockSpec(shape, lambda bi, i: (0,) * len(shape))
    return pl.pallas_call(
        functools.partial(_l1a_kernel, tm=tm, n=n),
        out_shape=[jax.ShapeDtypeStruct((b, n, c), BF16), jax.ShapeDtypeStruct((b, n, c), BF16)],
        grid=(b, n // tm),
        in_specs=[
            pl.BlockSpec((1, 16, d), lambda bi, i: (bi, jnp.maximum(i * hb - 1, 0), 0)),
            pl.BlockSpec((1, tm, d), lambda bi, i: (bi, i, 0)),
            pl.BlockSpec((1, 16, d), lambda bi, i: (bi, jnp.minimum((i + 1) * hb, nb16 - 1), 0)),
            full((d, c3)), full((3, c3)), full((1, c3)),
        ],
        out_specs=[pl.BlockSpec((1, tm, c), lambda bi, i: (bi, i, 0)),
                   pl.BlockSpec((1, tm, c), lambda bi, i: (bi, i, 0))],
        compiler_params=_cparams("parallel", "parallel"),
        name="hyena_in_conv",
    )(u, u, u, hy_in_w.astype(BF16), conv_w, conv_b.reshape(1, -1))


def _filter_kernel(zf_ref, zb_ref, w1f_ref, w1b_ref, b1_ref, w2_ref, b2_ref, w3_ref, b3_ref, wof_ref, wob_ref,
                   fr_ref, dl_ref, k_ref):
    zf = zf_ref[...]
    zb = zb_ref[...]
    fr = fr_ref[...]
    dot = functools.partial(jnp.dot, preferred_element_type=F32, precision=HIGHEST)
    h = jnp.sin(fr * (dot(zf, w1f_ref[...]) + dot(zb, w1b_ref[...]) + b1_ref[...]))
    h = jnp.sin(fr * (dot(h, w2_ref[...]) + b2_ref[...]))
    h = jnp.sin(fr * (dot(h, w3_ref[...]) + b3_ref[...]))
    dl = dl_ref[...]
    h_hi = h.astype(BF16)
    h_lo = (h - h_hi.astype(F32)).astype(BF16)

    def dot3(w_ref):
        bdot = functools.partial(jnp.dot, preferred_element_type=F32)
        return bdot(h_hi, w_ref[0]) + (bdot(h_hi, w_ref[1]) + bdot(h_lo, w_ref[0]))

    k_ref[0] = dot3(wof_ref) * jnp.exp(-zf[:, 0:1] * dl)
    ob = dot3(wob_ref) * jnp.exp(-zb[:, 0:1] * dl)
    first = (pl.program_id(0) == 0) & (lax.broadcasted_iota(jnp.int32, (ob.shape[0], 1), 0) == 0)
    k_ref[1] = jnp.where(first, 0.0, ob)


def _hyena_filters(n, fw1, fb1, fw2, fb2, fw3, fb3, fout, freq):
    c = fout.shape[1] // 2
    fh = fw2.shape[0]
    bands = jnp.linspace(1e-4, HY_BANDS - 1, HY_BANDS, dtype=F32)[None, :]

    def features(p):
        w_ang = (2.0 * math.pi / n) * p
        return jnp.concatenate([p / (n - 1.0), jnp.cos(bands * w_ang), -jnp.sin(bands * w_ang),
                                jnp.zeros((n, LANE - HY_EMB), F32)], axis=-1)

    m = jnp.arange(n, dtype=F32)[:, None]
    z_f, z_b = features(m), features(n - m)
    deltas = jnp.abs(jnp.linspace(HY_MIN_DECAY, HY_MAX_DECAY, c, dtype=F32))[None, :]
    zeros = lambda r, cc: jnp.zeros((r, cc), F32)
    two = lambda a: jnp.concatenate([a.reshape(1, -1), a.reshape(1, -1)], axis=1)
    diag2 = lambda w: jnp.concatenate([jnp.concatenate([w, zeros(fh, fh)], axis=1),
                                       jnp.concatenate([zeros(fh, fh), w], axis=1)], axis=0)
    w1 = jnp.pad(fw1, ((0, LANE - fw1.shape[0]), (0, 0)))
    w1f = jnp.concatenate([w1, zeros(LANE, fh)], axis=1)
    w1b = jnp.concatenate([zeros(LANE, fh), w1], axis=1)

    def split(w):
        hi = w.astype(BF16)
        return jnp.stack([hi, (w - hi.astype(F32)).astype(BF16)])

    wof = split(jnp.concatenate([fout[:, :c], zeros(fh, c)], axis=0))
    wob = split(jnp.concatenate([zeros(fh, c), fout[:, c:]], axis=0))
    tm = 1024
    full = lambda shape: pl.BlockSpec(shape, lambda i: (0,) * len(shape))
    rows = pl.BlockSpec((tm, LANE), lambda i: (i, 0))
    out = pl.pallas_call(
        _filter_kernel,
        out_shape=jax.ShapeDtypeStruct((2, n, c), F32),
        grid=(n // tm,),
        in_specs=[rows, rows, full((LANE, LANE)), full((LANE, LANE)), full((1, LANE)),
                  full((LANE, LANE)), full((1, LANE)), full((LANE, LANE)), full((1, LANE)),
                  full((2, LANE, c)), full((2, LANE, c)), full((1, LANE)), full((1, c))],
        out_specs=pl.BlockSpec((2, tm, c), lambda i: (0, i, 0)),
        compiler_params=_cparams("parallel"),
        name="hyena_filters",
    )(z_f, z_b, w1f, w1b, two(fb1), diag2(fw2), two(fb2), diag2(fw3), two(fb3), wof, wob, two(freq), deltas)
    return out.reshape(2 * n, c)


def _dft_consts():
    k = np.arange(DFT_HALF)[:, None]
    n1 = np.arange(DFT_N)[None, :]
    th = 2.0 * np.pi * k * n1 / DFT_N
    fa_r, fa_i = np.cos(th), -np.sin(th)
    kk = np.arange(DFT_N)[:, None] * np.arange(DFT_N)[None, :]
    c, s = np.cos(2.0 * np.pi * kk / DFT_N), np.sin(2.0 * np.pi * kk / DFT_N)
    g_fwd = np.block([[c, s], [-s, c]])
    g_inv = np.block([[c, -s], [s, c]])
    ph = 2.0 * np.pi * np.arange(DFT_HALF)[:, None] * np.arange(DFT_N)[None, :] / (DFT_N * DFT_N)
    tw_c, tw_s = np.cos(ph)[:, :, None], np.sin(ph)[:, :, None]
    n_out = np.arange(DFT_N // 2)[:, None]
    ps = 2.0 * np.pi * n_out * np.arange(DFT_HALF)[None, :] / DFT_N
    wgt = np.full((1, DFT_HALF), 2.0)
    wgt[0, 0] = wgt[0, -1] = 1.0
    length = DFT_N * DFT_N
    fo_r = wgt * np.cos(ps) / length
    fo_i = -wgt * np.sin(ps) / length
    return dict(fa_r=fa_r, fa_i=fa_i, g_fwd=g_fwd, g_inv=g_inv, tw_c=tw_c, tw_s=tw_s, fo_r=fo_r, fo_i=fo_i)


DFT_ROW_PAD = 72
DFT_NB = 16


def _dft_a_kernel(f_ref, x_ref, or_ref, oi_ref):
    f = f_ref[...]
    for j in range(x_ref.shape[2]):
        y = jnp.dot(f, x_ref[0, :, j, :].astype(BF16), preferred_element_type=F32)
        or_ref[0, :, j, :] = y[:DFT_HALF].astype(or_ref.dtype)
        oi_ref[0, :, j, :] = y[DFT_ROW_PAD:DFT_ROW_PAD + DFT_HALF].astype(oi_ref.dtype)


def _dft_a(f_stack, x, out_dtype):
    b, k, n2, c = x.shape
    blk = lambda rows: pl.BlockSpec((1, rows, DFT_NB, c), lambda bi, j: (bi, 0, j, 0))
    return pl.pallas_call(
        _dft_a_kernel,
        out_shape=[jax.ShapeDtypeStruct((b, DFT_HALF, n2, c), out_dtype)] * 2,
        grid=(b, n2 // DFT_NB),
        in_specs=[pl.BlockSpec(f_stack.shape, lambda bi, j: (0, 0)), blk(k)],
        out_specs=[blk(DFT_HALF)] * 2,
        compiler_params=_cparams("parallel", "parallel"),
        name="dft_stage_a",
    )(f_stack, x)


def _spectrum_kernel(ar_ref, ai_ref, tc_ref, ts_ref, g_ref, hr_ref, hi_ref):
    tc = tc_ref[0]
    ts = ts_ref[0]
    ar = ar_ref[0, 0]
    ai = ai_ref[0, 0]
    a2 = jnp.concatenate([ar * tc + ai * ts, ai * tc - ar * ts], axis=0).astype(BF16)
    x = jnp.dot(g_ref[...], a2, preferred_element_type=F32)
    hr_ref[0] = x[:DFT_N]
    hi_ref[0] = x[DFT_N:]


def _filter_spectrum(kr, ki, consts):
    c = kr.shape[-1]
    plane = pl.BlockSpec((1, 1, DFT_N, c), lambda k1: (0, k1, 0, 0))
    tw = pl.BlockSpec((1, DFT_N, 1), lambda k1: (k1, 0, 0))
    out = pl.BlockSpec((1, DFT_N, c), lambda k1: (k1, 0, 0))
    return pl.pallas_call(
        _spectrum_kernel,
        out_shape=[jax.ShapeDtypeStruct((DFT_HALF, DFT_N, c), F32)] * 2,
        grid=(DFT_HALF,),
        in_specs=[plane, plane, tw, tw, pl.BlockSpec((2 * DFT_N, 2 * DFT_N), lambda k1: (0, 0))],
        out_specs=[out, out],
        compiler_params=_cparams("parallel"),
        name="filter_spectrum",
    )(kr, ki, consts["tw_c"], consts["tw_s"], consts["g_fwd"])


CONV_UNROLL = 16
CONV_UNROLL_MID = 5
FFT_PITCH = 136


def _conv_tables(n1):
    length = DFT_N * DFT_N
    k1 = np.arange(DFT_HALF)[None, :, None]
    n = (DFT_N * np.arange(n1)[None, None, :] + np.arange(DFT_N)[:, None, None])
    th = 2.0 * np.pi * k1 * n / length
    pad = np.zeros((DFT_N, DFT_ROW_PAD - DFT_HALF, n1))
    fa = np.concatenate([np.cos(th), pad, -np.sin(th), pad], axis=1)
    wgt = np.full((1, 1, DFT_HALF), 2.0)
    wgt[..., 0] = wgt[..., -1] = 1.0
    tht = np.transpose(th, (0, 2, 1))
    padk = np.zeros((DFT_N, n1, DFT_ROW_PAD - DFT_HALF))
    fo_r = np.concatenate([wgt * np.cos(tht) / length, padk], axis=2)
    fo_i = np.concatenate([-wgt * np.sin(tht) / length, padk], axis=2)
    return (jnp.asarray(fa, BF16), jnp.asarray(fo_r, BF16), jnp.asarray(fo_i, BF16))


def _conv_fused_kernel(vx_ref, x0_ref, sk_ref, gf_ref, gi_ref, fa_hbm, for_hbm, foi_hbm, hr_hbm, hi_hbm, o_ref,
                       xs, ar, ai, fa_s, for_s, foi_s, hr_s, hi_s):
    ci = pl.program_id(0)
    bi = pl.program_id(1)
    lanes = vx_ref.shape[2]
    n1c = vx_ref.shape[1] // DFT_N

    @pl.when((ci == 0) & (bi == 0))
    def _():
        pltpu.sync_copy(fa_hbm, fa_s)
        pltpu.sync_copy(for_hbm, for_s)
        pltpu.sync_copy(foi_hbm, foi_s)

    @pl.when(bi == 0)
    def _():
        lane0 = pl.multiple_of(ci * lanes, lanes)
        pltpu.sync_copy(hr_hbm.at[:, :, pl.ds(lane0, lanes)], hr_s)
        pltpu.sync_copy(hi_hbm.at[:, :, pl.ds(lane0, lanes)], hi_s)

    for p in range(n1c):
        xs[pl.ds(p * FFT_PITCH, DFT_N), :] = vx_ref[0, pl.ds(p * DFT_N, DFT_N), :].astype(F32)

    def fwd(g, carry):
        n2s = [g * CONV_UNROLL + u for u in range(CONV_UNROLL)]
        xgs = [xs[pl.ds(n2, n1c, stride=FFT_PITCH), :].astype(BF16) for n2 in n2s]
        ys = [jnp.dot(fa_s[n2], xg, preferred_element_type=F32) for n2, xg in zip(n2s, xgs)]
        for n2, y in zip(n2s, ys):
            ar[pl.ds(n2, DFT_ROW_PAD, stride=FFT_PITCH), :] = y[:DFT_ROW_PAD]
            ai[pl.ds(n2, DFT_ROW_PAD, stride=FFT_PITCH), :] = y[DFT_ROW_PAD:]
        return carry
    lax.fori_loop(0, DFT_N // CONV_UNROLL, fwd, 0)

    def mid_one(a2, k1):
        x = jnp.dot(gf_ref[...], a2, preferred_element_type=F32)
        xr, xi = x[:DFT_N], x[DFT_N:]
        hr = hr_s[k1]
        hi = hi_s[k1]
        z = jnp.concatenate([xr * hr - xi * hi, xr * hi + xi * hr], axis=0).astype(BF16)
        return jnp.dot(gi_ref[...], z, preferred_element_type=F32)

    def mid(g, carry):
        k1s = [jnp.minimum(g * CONV_UNROLL_MID + u, DFT_HALF - 1) for u in range(CONV_UNROLL_MID)]
        bases = [pl.multiple_of(k1 * FFT_PITCH, 8) for k1 in k1s]
        a2s = [jnp.concatenate([ar[pl.ds(bs, DFT_N), :], ai[pl.ds(bs, DFT_N), :]], axis=0).astype(BF16)
               for bs in bases]
        ys = [mid_one(a2, k1) for a2, k1 in zip(a2s, k1s)]
        for bs, y in zip(bases, ys):
            ar[pl.ds(bs, DFT_N), :] = y[:DFT_N]
            ai[pl.ds(bs, DFT_N), :] = y[DFT_N:]
        return carry
    lax.fori_loop(0, -(-DFT_HALF // CONV_UNROLL_MID), mid, 0)

    def inv(g, carry):
        n2s = [g * CONV_UNROLL + u for u in range(CONV_UNROLL)]
        brs = [ar[pl.ds(n2, DFT_ROW_PAD, stride=FFT_PITCH), :].astype(BF16) for n2 in n2s]
        bis = [ai[pl.ds(n2, DFT_ROW_PAD, stride=FFT_PITCH), :].astype(BF16) for n2 in n2s]
        ys = [jnp.dot(for_s[n2], br, preferred_element_type=F32) + jnp.dot(foi_s[n2], bi_, preferred_element_type=F32)
              for n2, br, bi_ in zip(n2s, brs, bis)]
        for n2, y in zip(n2s, ys):
            xs[pl.ds(n2, n1c, stride=FFT_PITCH), :] = y
        return carry
    lax.fori_loop(0, DFT_N // CONV_UNROLL, inv, 0)

    sk = sk_ref[...]
    for p in range(n1c):
        rows = pl.ds(p * DFT_N, DFT_N)
        conv = xs[pl.ds(p * FFT_PITCH, DFT_N), :]
        o_ref[0, rows, :] = (x0_ref[0, rows, :].astype(F32)
                             * (conv + vx_ref[0, rows, :].astype(F32) * sk)).astype(o_ref.dtype)


def _conv_fused(x0, vx, hr, hi, skip, g_fwd, g_inv):
    b, n, c = vx.shape
    n1 = n // DFT_N
    lanes = LANE
    fa, fo_r, fo_i = _conv_tables(n1)
    blk = pl.BlockSpec((1, n, lanes), lambda ci, bi: (bi, 0, ci))
    full = lambda a: pl.BlockSpec(a.shape, lambda ci, bi: (0,) * a.ndim)
    hbm = pl.BlockSpec(memory_space=pl.ANY)
    return pl.pallas_call(
        _conv_fused_kernel,
        out_shape=jax.ShapeDtypeStruct((b, n, c), BF16),
        grid=(c // lanes, b),
        in_specs=[blk, blk, pl.BlockSpec((1, lanes), lambda ci, bi: (0, ci)), full(g_fwd), full(g_inv),
                  hbm, hbm, hbm, hbm, hbm],
        out_specs=blk,
        scratch_shapes=[
            pltpu.VMEM((n1 * FFT_PITCH, lanes), F32),
            pltpu.VMEM((DFT_ROW_PAD * FFT_PITCH, lanes), F32),
            pltpu.VMEM((DFT_ROW_PAD * FFT_PITCH, lanes), F32),
            pltpu.VMEM(fa.shape, BF16), pltpu.VMEM(fo_r.shape, BF16), pltpu.VMEM(fo_i.shape, BF16),
            pltpu.VMEM((DFT_HALF, DFT_N, lanes), F32), pltpu.VMEM((DFT_HALF, DFT_N, lanes), F32),
        ],
        compiler_params=_cparams("arbitrary", "arbitrary"),
        name="long_conv_fused",
    )(vx, x0, skip.reshape(1, c), g_fwd, g_inv, fa, fo_r, fo_i, hr, hi)


def _stack_rows(fr, fi):
    pad = np.zeros((DFT_ROW_PAD - DFT_HALF, fr.shape[1]))
    return jnp.asarray(np.concatenate([fr, pad, fi, pad], axis=0), BF16)


def _hyena_long_conv(x0, vx, kfull, skip):
    c = vx.shape[-1]
    cn = _dft_consts()
    bf = lambda a: jnp.asarray(a, BF16)
    f32 = lambda a: jnp.asarray(a, F32)
    consts = dict(tw_c=f32(cn["tw_c"]), tw_s=f32(cn["tw_s"]), g_fwd=bf(cn["g_fwd"]), g_inv=bf(cn["g_inv"]))
    kr, ki = _dft_a(_stack_rows(cn["fa_r"], cn["fa_i"]), kfull.reshape(1, DFT_N, DFT_N, c), F32)
    hr, hi = _filter_spectrum(kr, ki, consts)
    return _conv_fused(x0, vx, hr, hi, skip, consts["g_fwd"], consts["g_inv"])


def _l1c_kernel(y_ref, h_ref, g1_ref, sc2_ref, sh2_ref, w_ref, lg_ref, lb_ref, rw_ref, ho_ref, z_ref, lo_ref):
    y = jnp.dot(y_ref[0], w_ref[...], preferred_element_type=F32)
    hh = _layer_norm(ALPHA * h_ref[0] + g1_ref[0] * y, lg_ref[...], lb_ref[...])
    ho_ref[0] = hh
    z = hh * (1.0 + sc2_ref[0]) + sh2_ref[0]
    for s in range(z_ref.shape[2]):
        z_ref[0, :, s, :] = z[:, s * LANE:(s + 1) * LANE]
    lo_ref[0] = jnp.dot(z, rw_ref[...], preferred_element_type=F32, precision=HIGHEST)


def _l1c(y, h, g1, sc2, sh2, out_w, ln_g, ln_b, router_w):
    b, n, d = h.shape
    tm = 512
    blk = pl.BlockSpec((1, tm, d), lambda bi, i: (bi, i, 0))
    vec = pl.BlockSpec((1, 1, d), lambda bi, i: (bi, 0, 0))
    full = lambda shape: pl.BlockSpec(shape, lambda bi, i: (0,) * len(shape))
    rw = jnp.pad(router_w, ((0, 0), (0, LANE - router_w.shape[1])))
    return pl.pallas_call(
        _l1c_kernel,
        out_shape=[jax.ShapeDtypeStruct((b, n, d), F32), jax.ShapeDtypeStruct((b, n, d // LANE, LANE), F32),
                   jax.ShapeDtypeStruct((b, n, LANE), F32)],
        grid=(b, n // tm),
        in_specs=[blk, blk, vec, vec, vec, full((d, d)), full((1, d)), full((1, d)), full((d, LANE))],
        out_specs=[blk, pl.BlockSpec((1, tm, d // LANE, LANE), lambda bi, i: (bi, i, 0, 0)),
                   pl.BlockSpec((1, tm, LANE), lambda bi, i: (bi, i, 0))],
        compiler_params=_cparams("parallel", "parallel"),
        name="hyena_out_ln_router",
    )(y, h, g1, sc2, sh2, out_w.astype(BF16), ln_g.reshape(1, -1), ln_b.reshape(1, -1), rw)


def _take(a, idx):
    return a.at[idx].get(mode="promise_in_bounds")


def _route(logits, tm):
    t = logits.shape[0]
    top_v, top_i = lax.top_k(logits, TOP_K)
    gates = jax.nn.softmax(top_v, axis=-1)
    flat_e = top_i.reshape(-1).astype(jnp.int32)
    flat_g = gates.reshape(-1)
    n_sel = t * TOP_K
    p = n_sel + N_EXPERTS * tm
    eids = jnp.arange(N_EXPERTS, dtype=jnp.int32)[None, :]
    onehot = (flat_e[:, None] == eids).astype(jnp.int32)
    csum = jnp.cumsum(onehot, axis=0)
    counts = csum[-1]
    rank = jnp.sum((csum - onehot) * onehot, axis=1)
    padded = ((counts + tm - 1) // tm) * tm
    end_p = jnp.cumsum(padded)
    start_p = end_p - padded
    start = jnp.cumsum(counts) - counts
    pos = (jnp.sum(onehot * start_p[None, :], axis=1) + rank).reshape(t, TOP_K)
    order = jnp.argsort(flat_e, stable=True).astype(jnp.int32)
    r = jnp.arange(p, dtype=jnp.int32)
    e_row = jnp.sum(r[:, None] >= end_p[None, :], axis=1).astype(jnp.int32)
    oh_r = (jnp.minimum(e_row, N_EXPERTS - 1)[:, None] == eids).astype(jnp.int32)
    j = r - jnp.sum(oh_r * start_p[None, :], axis=1)
    valid = (e_row < N_EXPERTS) & (j < jnp.sum(oh_r * counts[None, :], axis=1))
    src = jnp.clip(jnp.sum(oh_r * start[None, :], axis=1) + j, 0, n_sel - 1)
    flat_idx = _take(order, src)
    row_token = jnp.where(valid, flat_idx // TOP_K, 0)
    row_gate = jnp.where(valid, _take(flat_g, flat_idx), 0.0)
    tile_start = jnp.arange(p // tm, dtype=jnp.int32) * tm
    tile_expert = jnp.sum(tile_start[:, None] >= end_p[None, :], axis=1).astype(jnp.int32)
    tile_valid = (tile_expert < N_EXPERTS).astype(jnp.int32)
    tile_expert = jnp.minimum(tile_expert, N_EXPERTS - 1)
    return row_token, row_gate[:, None], tile_expert, tile_valid, pos


def kernel(x, c, ctx, c_ctx, ada_w, ada_b, ln_g, ln_b, mix_in_w, pool_w, pool_scale, q_norm, q_up, kv_norm, kv_up, mix_out_w, ffn_gate, ffn_up, ffn_down, hy_in_w, hy_conv_w, hy_conv_b, hy_fw1, hy_fb1, hy_fw2, hy_fb2, hy_fw3, hy_fb3, hy_fout, hy_freq, hy_skip, hy_out_w, router_w, moe_gate, moe_up, moe_down):
    b, n, d = x.shape
    t = b * n
    assert b + 1 <= 8
    s_rows = jnp.concatenate([c, c_ctx[None, :], jnp.zeros((8 - b - 1, d), F32)], axis=0)
    mod = _ada(s_rows, ada_w, ada_b)

    def chunks(l, rows):
        m = mod[l, rows].reshape(-1, 6, d)
        return [m[:, k][:, None, :] for k in range(6)]

    sh1, sc1, g1, sh2, sc2, g2 = chunks(0, slice(0, b))
    shc, scc = chunks(0, slice(b, b + 1))[:2]
    pool_u, q, k, v = _l0a(x, ctx, sc1, sh1, scc, shc, mix_in_w[0], q_norm[0], q_up[0], kv_norm[0], kv_up[0])
    attn_o = _attention(q, k, v, n)
    h1, z1 = _l0b(pool_u, attn_o, x, g1, sc2, sh2, pool_w[0], pool_scale[0], mix_out_w[0], ln_g[0, 0], ln_b[0, 0])
    f0 = _ffn_dense(z1.reshape(t, d), ffn_gate.astype(BF16), ffn_up.astype(BF16), ffn_down.astype(BF16), 512, 1408)
    sh1, sc1, g1b, sh2b, sc2b, g2b = chunks(1, slice(0, b))
    h2, u2 = _resid_ln(h1, [f0.reshape(b, n, d)], g2, ln_g[0, 1], ln_b[0, 1], mod=(sc1, sh1))

    x0, vx = _l1a(u2, hy_in_w[0], hy_conv_w[0], hy_conv_b[0])
    kfull = _hyena_filters(n, hy_fw1[0], hy_fb1[0], hy_fw2[0], hy_fb2[0], hy_fw3[0], hy_fb3[0],
                           hy_fout[0], hy_freq[0])
    yl = _hyena_long_conv(x0, vx, kfull, hy_skip[0])
    h3, z3, logits = _l1c(yl, h2, g1b, sc2b, sh2b, hy_out_w[0], ln_g[1, 0], ln_b[1, 0], router_w[0])

    tm_moe = 1024
    row_token, row_gate, tile_expert, tile_valid, pos = _route(logits.reshape(t, LANE)[:, :N_EXPERTS], tm_moe)
    ys = _moe_grouped(z3.reshape(t, d // LANE, LANE), row_token, _cast_bf16(moe_gate[0]), _cast_bf16(moe_up[0]),
                      _cast_bf16(moe_down[0]), row_gate, tile_expert, tile_valid, tm_moe, 896)
    y_a = _take(ys, pos[:, 0]).reshape(b, n, d)
    y_b = _take(ys, pos[:, 1]).reshape(b, n, d)
    (h4,) = _resid_ln(h3, [y_a, y_b], g2b, ln_g[1, 1], ln_b[1, 1])
    return h4
```

```python
import functools
import math

import numpy as np
import jax
import jax.numpy as jnp
from jax import lax
from jax.experimental import pallas as pl
from jax.experimental.pallas import tpu as pltpu

F32 = jnp.float32
BF16 = jnp.bfloat16
HIGHEST = lax.Precision.HIGHEST

D_MODEL = 1024
GRID_W = 64
POOL_W = 512
POOL_WINDOWS = (2, 4, 8, 16)
POOL_GC = 128
MLA_HEADS = 8
QK_NOPE = 64
QK_ROPE = 32
V_HEAD = 64
Q_LORA = 256
KV_LORA = 256
ROPE_AXIS = 16
ROPE_BASE = 10000.0
N_EXPERTS = 8
TOP_K = 2
HY_BANDS = 16
HY_EMB = 1 + 2 * HY_BANDS
HY_FAST_DECAY = 0.3
HY_SLOW_DECAY = 1.5
HY_TARGET = 1e-2
HY_MIN_DECAY = math.log(HY_TARGET) / HY_SLOW_DECAY
HY_MAX_DECAY = math.log(HY_TARGET) / HY_FAST_DECAY
LN_EPS = 1e-5
RMS_EPS = 1e-6
DEPTH = 2
ALPHA = (2.0 * DEPTH) ** 0.25

LANE = 128
HEAD_SLOT = 128
DFT_N = 128
DFT_HALF = DFT_N // 2 + 1
VMEM_LIMIT = 56 * 1024 * 1024
ATTN_LOOKAHEAD = 2


def _cparams(*sem):
    return pltpu.CompilerParams(dimension_semantics=sem, vmem_limit_bytes=VMEM_LIMIT)


def _layer_norm(v, g, b):
    mu = jnp.mean(v, axis=-1, keepdims=True)
    c = v - mu
    var = jnp.mean(c * c, axis=-1, keepdims=True)
    return c * lax.rsqrt(var + LN_EPS) * g + b


def _rms_norm(v, g):
    return v * lax.rsqrt(jnp.mean(v * v, axis=-1, keepdims=True) + RMS_EPS) * g


def _ada_kernel(s_ref, w_ref, b_ref, o_ref):
    s = s_ref[...]
    s = s * jax.nn.sigmoid(s)
    o_ref[0] = jnp.dot(s, w_ref[0], preferred_element_type=F32, precision=HIGHEST) + b_ref[0]


def _ada(s_rows, ada_w, ada_b):
    depth, d, n6 = ada_w.shape
    tn = 768
    return pl.pallas_call(
        _ada_kernel,
        out_shape=jax.ShapeDtypeStruct((depth, 8, n6), F32),
        grid=(depth, n6 // tn),
        in_specs=[
            pl.BlockSpec((8, d), lambda l, j: (0, 0)),
            pl.BlockSpec((1, d, tn), lambda l, j: (l, 0, j)),
            pl.BlockSpec((1, 1, tn), lambda l, j: (l, 0, j)),
        ],
        out_specs=pl.BlockSpec((1, 8, tn), lambda l, j: (l, 0, j)),
        compiler_params=_cparams("parallel", "parallel"),
        name="ada_mod",
    )(s_rows, ada_w, ada_b.reshape(depth, 1, n6))


def _l0a_kernel(x_ref, c_ref, sc_ref, sh_ref, scc_ref, shc_ref, win_ref, qn_ref, kvn_ref,
                wqa_ref, wqb_ref, wk_ref, wv_ref, cos_ref, sin_ref, cost_ref, sint_ref,
                pu_ref, q_ref, k_ref, v_ref, u_scr, *, n_lat_tiles):
    i = pl.program_id(1)

    @pl.when(i < n_lat_tiles)
    def _():
        u_scr[...] = (x_ref[0] * (1.0 + sc_ref[0]) + sh_ref[0]).astype(BF16)

    @pl.when(i >= n_lat_tiles)
    def _():
        u_scr[...] = (c_ref[0] * (1.0 + scc_ref[0]) + shc_ref[0]).astype(BF16)

    proj = jnp.dot(u_scr[...], win_ref[...], preferred_element_type=F32)
    pu_ref[0] = proj[:, :POOL_W]
    cos = cos_ref[...]
    sin = sin_ref[...]
    q0 = POOL_W
    kv0 = POOL_W + Q_LORA
    r0 = kv0 + KV_LORA
    qn = _rms_norm(proj[:, q0:kv0], qn_ref[...]).astype(BF16)
    kvn = _rms_norm(proj[:, kv0:r0], kvn_ref[...]).astype(BF16)
    nt_dims = (((1,), (1,)), ((), ()))
    qa = lax.dot_general(wqa_ref[...], qn, nt_dims, preferred_element_type=F32)
    qb = lax.dot_general(wqb_ref[...], qn, nt_dims, preferred_element_type=F32)
    vt = lax.dot_general(wv_ref[...], kvn, nt_dims, preferred_element_type=F32)
    kn = jnp.dot(kvn, wk_ref[...], preferred_element_type=F32)
    krot = proj[:, r0:r0 + HEAD_SLOT] * cos + proj[:, r0 + HEAD_SLOT:r0 + 2 * HEAD_SLOT] * sin
    cos_t = cost_ref[...]
    sin_t = sint_ref[...]
    for h in range(MLA_HEADS):
        sl = slice(h * HEAD_SLOT, (h + 1) * HEAD_SLOT)
        q_ref[0, h] = (qa[sl] * cos_t + qb[sl] * sin_t).astype(BF16)
        k_ref[0, h] = (kn[:, sl] + krot).astype(BF16)
        v_ref[0, h] = vt[h * V_HEAD:(h + 1) * V_HEAD].astype(BF16)


def _rope_swap_index():
    half = ROPE_AXIS // 2
    idx = []
    for a in range(2):
        base = a * ROPE_AXIS
        idx += list(range(base + half, base + ROPE_AXIS)) + list(range(base, base + half))
    return np.array(idx)


def _rope_tables(n, n_ctx):
    rows = n // GRID_W
    r = jnp.repeat(jnp.arange(rows, dtype=F32), GRID_W)
    col = jnp.tile(jnp.arange(GRID_W, dtype=F32), rows)
    inv = ROPE_BASE ** (-jnp.arange(0, ROPE_AXIS, 2, dtype=F32) / ROPE_AXIS)
    ang_r = r[:, None] * inv
    ang_c = col[:, None] * inv
    cos32 = jnp.concatenate([jnp.cos(ang_r), jnp.cos(ang_r), jnp.cos(ang_c), jnp.cos(ang_c)], axis=-1)
    sin32 = jnp.concatenate([-jnp.sin(ang_r), jnp.sin(ang_r), -jnp.sin(ang_c), jnp.sin(ang_c)], axis=-1)
    pad = HEAD_SLOT - QK_NOPE - QK_ROPE
    cos = jnp.concatenate([jnp.ones((n, QK_NOPE), F32), cos32, jnp.ones((n, pad), F32)], axis=-1)
    sin = jnp.concatenate([jnp.zeros((n, QK_NOPE), F32), sin32, jnp.zeros((n, pad), F32)], axis=-1)
    cos = jnp.concatenate([cos, jnp.ones((n_ctx, HEAD_SLOT), F32)], axis=0)
    sin = jnp.concatenate([sin, jnp.zeros((n_ctx, HEAD_SLOT), F32)], axis=0)
    return cos, sin


def _l0a_weights(in_w, q_up, kv_up):
    swap = _rope_swap_index()
    d = in_w.shape[0]
    r0 = POOL_W + Q_LORA + KV_LORA
    w_rope = in_w[:, r0:]
    pad_l = jnp.zeros((d, QK_NOPE), F32)
    pad_r = jnp.zeros((d, HEAD_SLOT - QK_NOPE - QK_ROPE), F32)
    kr_a = jnp.concatenate([pad_l, w_rope, pad_r], axis=1)
    kr_b = jnp.concatenate([pad_l, w_rope[:, swap], pad_r], axis=1)
    w_in = jnp.concatenate([in_w[:, :r0], kr_a, kr_b], axis=1).astype(BF16)

    scale = (QK_NOPE + QK_ROPE) ** -0.5 * math.log2(math.e)
    qu = q_up.reshape(Q_LORA, MLA_HEADS, QK_NOPE + QK_ROPE) * scale
    zpad = jnp.zeros((Q_LORA, MLA_HEADS, HEAD_SLOT - QK_NOPE - QK_ROPE), F32)
    wq_a = jnp.concatenate([qu, zpad], axis=-1).reshape(Q_LORA, MLA_HEADS * HEAD_SLOT).T.astype(BF16)
    wq_b = jnp.concatenate([jnp.zeros((Q_LORA, MLA_HEADS, QK_NOPE), F32), qu[..., QK_NOPE:][..., swap], zpad],
                           axis=-1).reshape(Q_LORA, MLA_HEADS * HEAD_SLOT).T.astype(BF16)
    kvu = kv_up.reshape(KV_LORA, MLA_HEADS, QK_NOPE + V_HEAD)
    wk = jnp.concatenate([kvu[..., :QK_NOPE], jnp.zeros((KV_LORA, MLA_HEADS, HEAD_SLOT - QK_NOPE), F32)],
                         axis=-1).reshape(KV_LORA, MLA_HEADS * HEAD_SLOT).astype(BF16)
    wv_t = kvu[..., QK_NOPE:].reshape(KV_LORA, MLA_HEADS * V_HEAD).T.astype(BF16)
    return w_in, wq_a, wq_b, wk, wv_t


def _l0a(x, ctx, sc, sh, scc, shc, in_w, q_norm, q_up, kv_norm, kv_up):
    b, n, d = x.shape
    n_ctx = ctx.shape[1]
    tm = 256
    nl = n // tm
    nt = (n + n_ctx) // tm
    w_in, wq_a, wq_b, wk, wv_t = _l0a_weights(in_w, q_up, kv_up)
    cos, sin = _rope_tables(n, n_ctx)
    hw = MLA_HEADS * HEAD_SLOT
    full = lambda shape: pl.BlockSpec(shape, lambda bi, i: (0,) * len(shape))
    vec = pl.BlockSpec((1, 1, d), lambda bi, i: (bi, 0, 0))
    return pl.pallas_call(
        functools.partial(_l0a_kernel, n_lat_tiles=nl),
        out_shape=[
            jax.ShapeDtypeStruct((b, n + n_ctx, POOL_W), F32),
            jax.ShapeDtypeStruct((b, MLA_HEADS, HEAD_SLOT, n + n_ctx), BF16),
            jax.ShapeDtypeStruct((b, MLA_HEADS, n + n_ctx, HEAD_SLOT), BF16),
            jax.ShapeDtypeStruct((b, MLA_HEADS, V_HEAD, n + n_ctx), BF16),
        ],
        grid=(b, nt),
        in_specs=[
            pl.BlockSpec((1, tm, d), lambda bi, i: (bi, jnp.minimum(i, nl - 1), 0)),
            pl.BlockSpec((1, tm, d), lambda bi, i: (bi, jnp.maximum(i - nl, 0), 0)),
            vec, vec, full((1, 1, d)), full((1, 1, d)),
            full(w_in.shape), full((1, Q_LORA)), full((1, KV_LORA)),
            full((hw, Q_LORA)), full((hw, Q_LORA)), full((KV_LORA, hw)), full((MLA_HEADS * V_HEAD, KV_LORA)),
            pl.BlockSpec((tm, HEAD_SLOT), lambda bi, i: (i, 0)),
            pl.BlockSpec((tm, HEAD_SLOT), lambda bi, i: (i, 0)),
            pl.BlockSpec((HEAD_SLOT, tm), lambda bi, i: (0, i)),
            pl.BlockSpec((HEAD_SLOT, tm), lambda bi, i: (0, i)),
        ],
        out_specs=[
            pl.BlockSpec((1, tm, POOL_W), lambda bi, i: (bi, i, 0)),
            pl.BlockSpec((1, MLA_HEADS, HEAD_SLOT, tm), lambda bi, i: (bi, 0, 0, i)),
            pl.BlockSpec((1, MLA_HEADS, tm, HEAD_SLOT), lambda bi, i: (bi, 0, i, 0)),
            pl.BlockSpec((1, MLA_HEADS, V_HEAD, tm), lambda bi, i: (bi, 0, 0, i)),
        ],
        scratch_shapes=[pltpu.VMEM((tm, d), BF16)],
        compiler_params=_cparams("parallel", "arbitrary"),
        name="l0_in_proj",
    )(x, ctx, sc, sh, scc, shc, w_in, q_norm.reshape(1, -1), kv_norm.reshape(1, -1),
      wq_a, wq_b, wk, wv_t, cos, sin, cos.T, sin.T)


def _attn_kernel(q_ref, k_ref, v_ref, o_ref, m_ref, l_ref, acc_ref, *, nk):
    ki = pl.program_id(2)

    @pl.when(ki == 0)
    def _():
        m_ref[...] = jnp.full(m_ref.shape, -jnp.inf, F32)
        l_ref[...] = jnp.zeros(l_ref.shape, F32)
        acc_ref[...] = jnp.zeros(acc_ref.shape, F32)

    def scores(h):
        return jnp.dot(k_ref[0, h], q_ref[0, h], preferred_element_type=F32)

    pending = [scores(h) for h in range(ATTN_LOOKAHEAD)]
    for h in range(MLA_HEADS):
        if h + ATTN_LOOKAHEAD < MLA_HEADS:
            pending.append(scores(h + ATTN_LOOKAHEAD))
        st = pending.pop(0)
        m_prev = m_ref[h]
        m_new = jnp.maximum(m_prev, jnp.max(st, axis=0, keepdims=True))
        a = jnp.exp2(m_prev - m_new)
        p = jnp.exp2(st - m_new)
        l_ref[h] = a * l_ref[h] + jnp.sum(p, axis=0, keepdims=True)
        pv = jnp.dot(v_ref[0, h], p.astype(BF16), preferred_element_type=F32)
        rows = slice(h * V_HEAD, (h + 1) * V_HEAD)
        acc_ref[rows, :] = a * acc_ref[rows, :] + pv
        m_ref[h] = m_new

    @pl.when(ki == nk - 1)
    def _():
        for h in range(MLA_HEADS):
            rows = slice(h * V_HEAD, (h + 1) * V_HEAD)
            acc_ref[rows, :] = acc_ref[rows, :] / l_ref[h]
        o_ref[0] = acc_ref[...].T.astype(o_ref.dtype)


def _attention(q, k, vt, n):
    b, h, n_all, _ = k.shape
    tq, tk = 512, 1408
    nk = n_all // tk
    return pl.pallas_call(
        functools.partial(_attn_kernel, nk=nk),
        out_shape=jax.ShapeDtypeStruct((b, n, h * V_HEAD), BF16),
        grid=(b, n // tq, nk),
        in_specs=[
            pl.BlockSpec((1, h, HEAD_SLOT, tq), lambda bi, qi, ki: (bi, 0, 0, qi)),
            pl.BlockSpec((1, h, tk, HEAD_SLOT), lambda bi, qi, ki: (bi, 0, ki, 0)),
            pl.BlockSpec((1, h, V_HEAD, tk), lambda bi, qi, ki: (bi, 0, 0, ki)),
        ],
        out_specs=pl.BlockSpec((1, tq, h * V_HEAD), lambda bi, qi, ki: (bi, qi, 0)),
        scratch_shapes=[
            pltpu.VMEM((h, 1, tq), F32),
            pltpu.VMEM((h, 1, tq), F32),
            pltpu.VMEM((h * V_HEAD, tq), F32),
        ],
        compiler_params=_cparams("parallel", "parallel", "arbitrary"),
        name="mla_attention",
    )(q, k, vt)


def _l0b_kernel(pp_ref, pc_ref, pn_ref, o_ref, x_ref, g1_ref, sc2_ref, sh2_ref, pw_ref, ps_ref, ow_ref,
                lg_ref, lb_ref, h_ref, z_ref, *, tm, n):
    i = pl.program_id(1)
    halo = 8
    ext = jnp.concatenate([pp_ref[0], pc_ref[0], pn_ref[0]], axis=0)
    pos = i * tm - halo + lax.broadcasted_iota(jnp.int32, (tm + 2 * halo, 1), 0)
    ext = jnp.where((pos >= 0) & (pos < n), ext, 0.0)
    t = i * tm + lax.broadcasted_iota(jnp.int32, (tm, 1), 0)
    y = jnp.dot(o_ref[0], ow_ref[POOL_W:, :], preferred_element_type=F32)
    for g, w in enumerate(POOL_WINDOWS):
        hw = w // 2
        e = ext[:, g * POOL_GC:(g + 1) * POOL_GC]
        s = e
        width = 1
        while width < w:
            s = s[:s.shape[0] - width] + s[width:]
            width *= 2
        win = s[halo - hw:halo - hw + tm]
        cnt = (jnp.minimum(t + hw, n) - jnp.maximum(t - hw, 0)).astype(F32)
        dd = (win / cnt - e[halo:halo + tm]).astype(BF16)
        yg = jnp.dot(dd, pw_ref[g], preferred_element_type=F32) * ps_ref[:, g * POOL_GC:(g + 1) * POOL_GC]
        y = y + jnp.dot(yg.astype(BF16), ow_ref[g * POOL_GC:(g + 1) * POOL_GC, :], preferred_element_type=F32)
    hh = _layer_norm(ALPHA * x_ref[0] + g1_ref[0] * y, lg_ref[...], lb_ref[...])
    h_ref[0] = hh
    z_ref[0] = (hh * (1.0 + sc2_ref[0]) + sh2_ref[0]).astype(BF16)


def _l0b(pool_u, attn_o, x, g1, sc2, sh2, pool_w, pool_scale, out_w, ln_g, ln_b):
    b, n, d = x.shape
    tm = 512
    hb = tm // 8
    vec = pl.BlockSpec((1, 1, d), lambda bi, i: (bi, 0, 0))
    full = lambda shape: pl.BlockSpec(shape, lambda bi, i: (0,) * len(shape))
    return pl.pallas_call(
        functools.partial(_l0b_kernel, tm=tm, n=n),
        out_shape=[jax.ShapeDtypeStruct((b, n, d), F32), jax.ShapeDtypeStruct((b, n, d), BF16)],
        grid=(b, n // tm),
        in_specs=[
            pl.BlockSpec((1, 8, POOL_W), lambda bi, i: (bi, jnp.maximum(i * hb - 1, 0), 0)),
            pl.BlockSpec((1, tm, POOL_W), lambda bi, i: (bi, i, 0)),
            pl.BlockSpec((1, 8, POOL_W), lambda bi, i: (bi, (i + 1) * hb, 0)),
            pl.BlockSpec((1, tm, POOL_W), lambda bi, i: (bi, i, 0)),
            pl.BlockSpec((1, tm, d), lambda bi, i: (bi, i, 0)),
            vec, vec, vec,
            full(pool_w.shape), full((1, POOL_W)), full(out_w.shape), full((1, d)), full((1, d)),
        ],
        out_specs=[pl.BlockSpec((1, tm, d), lambda bi, i: (bi, i, 0)),
                   pl.BlockSpec((1, tm, d), lambda bi, i: (bi, i, 0))],
        compiler_params=_cparams("parallel", "parallel"),
        name="l0_pool_out_ln",
    )(pool_u, pool_u, pool_u, attn_o, x, g1, sc2, sh2, pool_w.astype(BF16), pool_scale.reshape(1, -1),
      out_w.astype(BF16), ln_g.reshape(1, -1), ln_b.reshape(1, -1))


def _swiglu_step(z, wg_ref, wu_ref, wd_ref, acc_ref):
    g = jnp.dot(z, wg_ref[0].astype(BF16), preferred_element_type=F32)
    u = jnp.dot(z, wu_ref[0].astype(BF16), preferred_element_type=F32)
    a = (g * jax.nn.sigmoid(g) * u).astype(BF16)
    acc_ref[...] += jnp.dot(a, wd_ref[0].astype(BF16), preferred_element_type=F32)


def _ffn_kernel(z_ref, wg_ref, wu_ref, wd_ref, o_ref, acc_ref, *, nj):
    j = pl.program_id(1)

    @pl.when(j == 0)
    def _():
        acc_ref[...] = jnp.zeros(acc_ref.shape, F32)

    _swiglu_step(z_ref[...], wg_ref, wu_ref, wd_ref, acc_ref)

    @pl.when(j == nj - 1)
    def _():
        o_ref[...] = acc_ref[...]


def _ffn_dense(z, wg, wu, wd, tm, tf):
    t, d = z.shape
    f = wg.shape[-1]
    nj = f // tf
    return pl.pallas_call(
        functools.partial(_ffn_kernel, nj=nj),
        out_shape=jax.ShapeDtypeStruct((t, d), F32),
        grid=(t // tm, nj),
        in_specs=[
            pl.BlockSpec((tm, d), lambda i, j: (i, 0)),
            pl.BlockSpec((1, d, tf), lambda i, j: (0, 0, j)),
            pl.BlockSpec((1, d, tf), lambda i, j: (0, 0, j)),
            pl.BlockSpec((1, tf, d), lambda i, j: (0, j, 0)),
        ],
        out_specs=pl.BlockSpec((tm, d), lambda i, j: (i, 0)),
        scratch_shapes=[pltpu.VMEM((tm, d), F32)],
        compiler_params=_cparams("parallel", "arbitrary"),
        name="swiglu_dense",
    )(z, wg, wu, wd)


def _cast_kernel(x_ref, o_ref):
    o_ref[...] = x_ref[...].astype(o_ref.dtype)


def _cast_bf16(w):
    e, r, c = w.shape
    tr = 512
    blk = pl.BlockSpec((1, tr, c), lambda ei, ri: (ei, ri, 0))
    return pl.pallas_call(
        _cast_kernel,
        out_shape=jax.ShapeDtypeStruct(w.shape, BF16),
        grid=(e, r // tr),
        in_specs=[blk],
        out_specs=blk,
        compiler_params=_cparams("parallel", "parallel"),
        name="cast_bf16",
    )(w)


MOE_LOOKAHEAD = 2


def _moe_kernel(te_ref, tv_ref, *refs, nj, tm, n_tiles):
    tok_first = refs[:MOE_LOOKAHEAD]
    tokn_ref, z_hbm, wg_ref, wu_ref, wd_ref, gate_ref, o_ref, zbuf, zb16, sem, acc_ref = refs[MOE_LOOKAHEAD:]
    i = pl.program_id(0)
    j = pl.program_id(1)
    slots = zbuf.shape[0]
    slot = lax.rem(i, slots)
    nxt = lax.rem(i + MOE_LOOKAHEAD, slots)
    segs = z_hbm.shape[1]
    per_step = zbuf.shape[1] // (segs * nj)

    def row_copy(tok_ref, row, s):
        tok = tok_ref[0, 0, jnp.minimum(row, tm - 1)]
        return pltpu.make_async_copy(z_hbm.at[tok], zbuf.at[s, pl.ds(row * segs, segs), :], sem.at[s])

    def wait_slot(s):
        pltpu.make_async_copy(zbuf.at[s], zbuf.at[s], sem.at[s]).wait()

    def issue_next():
        for r in range(per_step):
            row_copy(tokn_ref, j * per_step + r, nxt).start()

    @pl.when((i == 0) & (j == 0))
    def _():
        for k, tok_ref in enumerate(tok_first):
            def body(r, carry):
                row_copy(tok_ref, r, k).start()
                return carry
            lax.fori_loop(0, per_step * nj, body, 0)

    @pl.when(j == 0)
    def _():
        wait_slot(slot)
        for g in range(segs):
            zb16[:, g * LANE:(g + 1) * LANE] = zbuf.at[slot][pl.ds(g, tm, stride=segs), :].astype(BF16)
        acc_ref[...] = jnp.zeros(acc_ref.shape, F32)

    @pl.when(tv_ref[i] > 0)
    def _():
        issue_next()
        _swiglu_step(zb16[...], wg_ref, wu_ref, wd_ref, acc_ref)

    @pl.when(tv_ref[i] == 0)
    def _():
        issue_next()

    @pl.when(j == nj - 1)
    def _():
        o_ref[...] = acc_ref[...] * gate_ref[...]

    @pl.when((i == n_tiles - 1) & (j == nj - 1))
    def _():
        for k in range(1, MOE_LOOKAHEAD + 1):
            wait_slot(lax.rem(i + k, slots))


def _moe_grouped(z, row_token, wg, wu, wd, gate, tile_expert, tile_valid, tm, tf):
    t, segs, _ = z.shape
    d = segs * LANE
    p = row_token.shape[0]
    n_tiles = p // tm
    f = wg.shape[-1]
    nj = f // tf
    tok = row_token.reshape(n_tiles, 1, tm)
    buf_rows = nj * (-(-tm // (8 * nj)) * 8)
    smem_blk = lambda imap: pl.BlockSpec((1, 1, tm), imap, memory_space=pltpu.SMEM)
    return pl.pallas_call(
        functools.partial(_moe_kernel, nj=nj, tm=tm, n_tiles=n_tiles),
        out_shape=jax.ShapeDtypeStruct((p, d), F32),
        grid_spec=pltpu.PrefetchScalarGridSpec(
            num_scalar_prefetch=2,
            grid=(n_tiles, nj),
            in_specs=[
                *[smem_blk(functools.partial(lambda i, j, te, tv, k: (k, 0, 0), k=k)) for k in range(MOE_LOOKAHEAD)],
                smem_blk(lambda i, j, te, tv: (jnp.minimum(i + MOE_LOOKAHEAD, n_tiles - 1), 0, 0)),
                pl.BlockSpec(memory_space=pl.ANY),
                pl.BlockSpec((1, d, tf), lambda i, j, te, tv: (te[i], 0, j)),
                pl.BlockSpec((1, d, tf), lambda i, j, te, tv: (te[i], 0, j)),
                pl.BlockSpec((1, tf, d), lambda i, j, te, tv: (te[i], j, 0)),
                pl.BlockSpec((tm, 1), lambda i, j, te, tv: (i, 0)),
            ],
            out_specs=pl.BlockSpec((tm, d), lambda i, j, te, tv: (i, 0)),
            scratch_shapes=[
                pltpu.VMEM((MOE_LOOKAHEAD + 1, buf_rows * segs, LANE), F32),
                pltpu.VMEM((tm, d), BF16),
                pltpu.SemaphoreType.DMA((MOE_LOOKAHEAD + 1,)),
                pltpu.VMEM((tm, d), F32),
            ],
        ),
        compiler_params=_cparams("arbitrary", "arbitrary"),
        name="moe_swiglu_gather",
    )(tile_expert, tile_valid, *([tok] * (MOE_LOOKAHEAD + 1)), z, wg, wu, wd, gate)


def _resid_ln_kernel(*refs, n_y, with_mod):
    h_ref = refs[0]
    y_refs = refs[1:1 + n_y]
    g_ref, lg_ref, lb_ref = refs[1 + n_y:4 + n_y]
    rest = refs[4 + n_y:]
    y = y_refs[0][0]
    for r in y_refs[1:]:
        y = y + r[0]
    hh = _layer_norm(ALPHA * h_ref[0] + g_ref[0] * y, lg_ref[...], lb_ref[...])
    if with_mod:
        sc_ref, sh_ref, ho_ref, u_ref = rest
        ho_ref[0] = hh
        u_ref[0] = (hh * (1.0 + sc_ref[0]) + sh_ref[0]).astype(BF16)
    else:
        rest[0][0] = hh


def _resid_ln(h, ys, g, ln_g, ln_b, mod=None):
    b, n, d = h.shape
    tm = 1024
    blk = pl.BlockSpec((1, tm, d), lambda bi, i: (bi, i, 0))
    vec = pl.BlockSpec((1, 1, d), lambda bi, i: (bi, 0, 0))
    row = pl.BlockSpec((1, d), lambda bi, i: (0, 0))
    in_specs = [blk] + [blk] * len(ys) + [vec, row, row]
    args = [h, *ys, g, ln_g.reshape(1, -1), ln_b.reshape(1, -1)]
    out_shape = [jax.ShapeDtypeStruct((b, n, d), F32)]
    out_specs = [blk]
    if mod is not None:
        in_specs += [vec, vec]
        args += list(mod)
        out_shape.append(jax.ShapeDtypeStruct((b, n, d), BF16))
        out_specs.append(blk)
    return pl.pallas_call(
        functools.partial(_resid_ln_kernel, n_y=len(ys), with_mod=mod is not None),
        out_shape=out_shape,
        grid=(b, n // tm),
        in_specs=in_specs,
        out_specs=out_specs,
        compiler_params=_cparams("parallel", "parallel"),
        name="resid_ln",
    )(*args)


def _l1a_kernel(up_ref, uc_ref, un_ref, w_ref, cw_ref, cb_ref, x0_ref, vx_ref, *, tm, n):
    i = pl.program_id(1)
    halo = 16
    ext = jnp.concatenate([up_ref[0], uc_ref[0], un_ref[0]], axis=0)
    pos = i * tm - halo + lax.broadcasted_iota(jnp.int32, (tm + 2 * halo, 1), 0)
    ext = jnp.where((pos >= 0) & (pos < n), ext, jnp.zeros_like(ext))
    z = jnp.dot(ext, w_ref[...], preferred_element_type=F32)
    cw = cw_ref[...]
    zc = (cw[0:1] * z[halo - 1:halo - 1 + tm] + cw[1:2] * z[halo:halo + tm]
          + cw[2:3] * z[halo + 1:halo + 1 + tm] + cb_ref[...])
    c = zc.shape[1] // 3
    x0_ref[0] = zc[:, :c].astype(x0_ref.dtype)
    vx_ref[0] = (zc[:, 2 * c:] * zc[:, c:2 * c]).astype(vx_ref.dtype)


def _l1a(u, hy_in_w, conv_w, conv_b):
    b, n, d = u.shape
    c3 = hy_in_w.shape[1]
    c = c3 // 3
    tm = 512
    hb = tm // 16
    nb16 = n // 16
    full = lambda shape: pl.BlockSpec(shape, lambda bi, i: (0,) * len(shape))
    return pl.pallas_call(
        functools.partial(_l1a_kernel, tm=tm, n=n),
        out_shape=[jax.ShapeDtypeStruct((b, n, c), BF16), jax.ShapeDtypeStruct((b, n, c), BF16)],
        grid=(b, n // tm),
        in_specs=[
            pl.BlockSpec((1, 16, d), lambda bi, i: (bi, jnp.maximum(i * hb - 1, 0), 0)),
            pl.BlockSpec((1, tm, d), lambda bi, i: (bi, i, 0)),
            pl.BlockSpec((1, 16, d), lambda bi, i: (bi, jnp.minimum((i + 1) * hb, nb16 - 1), 0)),
            full((d, c3)), full((3, c3)), full((1, c3)),
        ],
        out_specs=[pl.BlockSpec((1, tm, c), lambda bi, i: (bi, i, 0)),
                   pl.BlockSpec((1, tm, c), lambda bi, i: (bi, i, 0))],
        compiler_params=_cparams("parallel", "parallel"),
        name="hyena_in_conv",
    )(u, u, u, hy_in_w.astype(BF16), conv_w, conv_b.reshape(1, -1))


def _filter_kernel(zf_ref, zb_ref, w1f_ref, w1b_ref, b1_ref, w2_ref, b2_ref, w3_ref, b3_ref, wof_ref, wob_ref,
                   fr_ref, dl_ref, k_ref):
    zf = zf_ref[...]
    zb = zb_ref[...]
    fr = fr_ref[...]
    dot = functools.partial(jnp.dot, preferred_element_type=F32, precision=HIGHEST)
    h = jnp.sin(fr * (dot(zf, w1f_ref[...]) + dot(zb, w1b_ref[...]) + b1_ref[...]))
    h = jnp.sin(fr * (dot(h, w2_ref[...]) + b2_ref[...]))
    h = jnp.sin(fr * (dot(h, w3_ref[...]) + b3_ref[...]))
    dl = dl_ref[...]
    h_hi = h.astype(BF16)
    h_lo = (h - h_hi.astype(F32)).astype(BF16)

    def dot3(w_ref):
        bdot = functools.partial(jnp.dot, preferred_element_type=F32)
        return bdot(h_hi, w_ref[0]) + (bdot(h_hi, w_ref[1]) + bdot(h_lo, w_ref[0]))

    k_ref[0] = dot3(wof_ref) * jnp.exp(-zf[:, 0:1] * dl)
    ob = dot3(wob_ref) * jnp.exp(-zb[:, 0:1] * dl)
    first = (pl.program_id(0) == 0) & (lax.broadcasted_iota(jnp.int32, (ob.shape[0], 1), 0) == 0)
    k_ref[1] = jnp.where(first, 0.0, ob)


def _hyena_filters(n, fw1, fb1, fw2, fb2, fw3, fb3, fout, freq):
    c = fout.shape[1] // 2
    fh = fw2.shape[0]
    bands = jnp.linspace(1e-4, HY_BANDS - 1, HY_BANDS, dtype=F32)[None, :]

    def features(p):
        w_ang = (2.0 * math.pi / n) * p
        return jnp.concatenate([p / (n - 1.0), jnp.cos(bands * w_ang), -jnp.sin(bands * w_ang),
                                jnp.zeros((n, LANE - HY_EMB), F32)], axis=-1)

    m = jnp.arange(n, dtype=F32)[:, None]
    z_f, z_b = features(m), features(n - m)
    deltas = jnp.abs(jnp.linspace(HY_MIN_DECAY, HY_MAX_DECAY, c, dtype=F32))[None, :]
    zeros = lambda r, cc: jnp.zeros((r, cc), F32)
    two = lambda a: jnp.concatenate([a.reshape(1, -1), a.reshape(1, -1)], axis=1)
    diag2 = lambda w: jnp.concatenate([jnp.concatenate([w, zeros(fh, fh)], axis=1),
                                       jnp.concatenate([zeros(fh, fh), w], axis=1)], axis=0)
    w1 = jnp.pad(fw1, ((0, LANE - fw1.shape[0]), (0, 0)))
    w1f = jnp.concatenate([w1, zeros(LANE, fh)], axis=1)
    w1b = jnp.concatenate([zeros(LANE, fh), w1], axis=1)

    def split(w):
        hi = w.astype(BF16)
        return jnp.stack([hi, (w - hi.astype(F32)).astype(BF16)])

    wof = split(jnp.concatenate([fout[:, :c], zeros(fh, c)], axis=0))
    wob = split(jnp.concatenate([zeros(fh, c), fout[:, c:]], axis=0))
    tm = 1024
    full = lambda shape: pl.BlockSpec(shape, lambda i: (0,) * len(shape))
    rows = pl.BlockSpec((tm, LANE), lambda i: (i, 0))
    out = pl.pallas_call(
        _filter_kernel,
        out_shape=jax.ShapeDtypeStruct((2, n, c), F32),
        grid=(n // tm,),
        in_specs=[rows, rows, full((LANE, LANE)), full((LANE, LANE)), full((1, LANE)),
                  full((LANE, LANE)), full((1, LANE)), full((LANE, LANE)), full((1, LANE)),
                  full((2, LANE, c)), full((2, LANE, c)), full((1, LANE)), full((1, c))],
        out_specs=pl.BlockSpec((2, tm, c), lambda i: (0, i, 0)),
        compiler_params=_cparams("parallel"),
        name="hyena_filters",
    )(z_f, z_b, w1f, w1b, two(fb1), diag2(fw2), two(fb2), diag2(fw3), two(fb3), wof, wob, two(freq), deltas)
    return out.reshape(2 * n, c)


def _dft_consts():
    kk = np.arange(DFT_N)[:, None] * np.arange(DFT_N)[None, :]
    c, s = np.cos(2.0 * np.pi * kk / DFT_N), np.sin(2.0 * np.pi * kk / DFT_N)
    g_fwd = np.block([[c, s], [-s, c]])
    g_inv = np.block([[c, -s], [s, c]])
    return jnp.asarray(g_fwd, BF16), jnp.asarray(g_inv, BF16)


DFT_ROW_PAD = 72
CONV_UNROLL = 16
CONV_UNROLL_MID = 5
FFT_PITCH = 136


def _conv_tables(n1):
    length = DFT_N * DFT_N
    k1 = np.arange(DFT_HALF)[None, :, None]
    n = (DFT_N * np.arange(n1)[None, None, :] + np.arange(DFT_N)[:, None, None])
    th = 2.0 * np.pi * k1 * n / length
    pad = np.zeros((DFT_N, DFT_ROW_PAD - DFT_HALF, n1))
    fa = np.concatenate([np.cos(th), pad, -np.sin(th), pad], axis=1)
    wgt = np.full((1, 1, DFT_HALF), 2.0)
    wgt[..., 0] = wgt[..., -1] = 1.0
    tht = np.transpose(th, (0, 2, 1))
    padk = np.zeros((DFT_N, n1, DFT_ROW_PAD - DFT_HALF))
    fo_r = np.concatenate([wgt * np.cos(tht) / length, padk], axis=2)
    fo_i = np.concatenate([-wgt * np.sin(tht) / length, padk], axis=2)
    return (jnp.asarray(fa, BF16), jnp.asarray(fo_r, BF16), jnp.asarray(fo_i, BF16))


def _conv_fused_kernel(vx_ref, x0_ref, sk_ref, gf_ref, gi_ref, fa_hbm, for_hbm, foi_hbm, hr_hbm, hi_hbm, o_ref,
                       xs, ar, ai, fa_s, for_s, foi_s, hr_s, hi_s):
    ci = pl.program_id(0)
    bi = pl.program_id(1)
    lanes = vx_ref.shape[2]
    n1c = vx_ref.shape[1] // DFT_N

    @pl.when((ci == 0) & (bi == 0))
    def _():
        pltpu.sync_copy(fa_hbm, fa_s)
        pltpu.sync_copy(for_hbm, for_s)
        pltpu.sync_copy(foi_hbm, foi_s)

    @pl.when(bi == 0)
    def _():
        lane0 = pl.multiple_of(ci * lanes, lanes)
        pltpu.sync_copy(hr_hbm.at[:, :, pl.ds(lane0, lanes)], hr_s)
        pltpu.sync_copy(hi_hbm.at[:, :, pl.ds(lane0, lanes)], hi_s)

    for p in range(n1c):
        xs[pl.ds(p * FFT_PITCH, DFT_N), :] = vx_ref[0, pl.ds(p * DFT_N, DFT_N), :].astype(F32)

    def fwd(g, carry):
        n2s = [g * CONV_UNROLL + u for u in range(CONV_UNROLL)]
        xgs = [xs[pl.ds(n2, n1c, stride=FFT_PITCH), :].astype(BF16) for n2 in n2s]
        ys = [jnp.dot(fa_s[n2], xg, preferred_element_type=F32) for n2, xg in zip(n2s, xgs)]
        for n2, y in zip(n2s, ys):
            ar[pl.ds(n2, DFT_ROW_PAD, stride=FFT_PITCH), :] = y[:DFT_ROW_PAD]
            ai[pl.ds(n2, DFT_ROW_PAD, stride=FFT_PITCH), :] = y[DFT_ROW_PAD:]
        return carry
    lax.fori_loop(0, DFT_N // CONV_UNROLL, fwd, 0)

    def mid_one(a2, k1):
        x = jnp.dot(gf_ref[...], a2, preferred_element_type=F32)
        xr, xi = x[:DFT_N], x[DFT_N:]
        hr = hr_s[k1]
        hi = hi_s[k1]
        z = jnp.concatenate([xr * hr - xi * hi, xr * hi + xi * hr], axis=0).astype(BF16)
        return jnp.dot(gi_ref[...], z, preferred_element_type=F32)

    def mid(g, carry):
        k1s = [jnp.minimum(g * CONV_UNROLL_MID + u, DFT_HALF - 1) for u in range(CONV_UNROLL_MID)]
        bases = [pl.multiple_of(k1 * FFT_PITCH, 8) for k1 in k1s]
        a2s = [jnp.concatenate([ar[pl.ds(bs, DFT_N), :], ai[pl.ds(bs, DFT_N), :]], axis=0).astype(BF16)
               for bs in bases]
        ys = [mid_one(a2, k1) for a2, k1 in zip(a2s, k1s)]
        for bs, y in zip(bases, ys):
            ar[pl.ds(bs, DFT_N), :] = y[:DFT_N]
            ai[pl.ds(bs, DFT_N), :] = y[DFT_N:]
        return carry
    lax.fori_loop(0, -(-DFT_HALF // CONV_UNROLL_MID), mid, 0)

    def inv(g, carry):
        n2s = [g * CONV_UNROLL + u for u in range(CONV_UNROLL)]
        brs = [ar[pl.ds(n2, DFT_ROW_PAD, stride=FFT_PITCH), :].astype(BF16) for n2 in n2s]
        bis = [ai[pl.ds(n2, DFT_ROW_PAD, stride=FFT_PITCH), :].astype(BF16) for n2 in n2s]
        ys = [jnp.dot(for_s[n2], br, preferred_element_type=F32) + jnp.dot(foi_s[n2], bi_, preferred_element_type=F32)
              for n2, br, bi_ in zip(n2s, brs, bis)]
        for n2, y in zip(n2s, ys):
            xs[pl.ds(n2, n1c, stride=FFT_PITCH), :] = y
        return carry
    lax.fori_loop(0, DFT_N // CONV_UNROLL, inv, 0)

    sk = sk_ref[...]
    for p in range(n1c):
        rows = pl.ds(p * DFT_N, DFT_N)
        conv = xs[pl.ds(p * FFT_PITCH, DFT_N), :]
        o_ref[0, rows, :] = (x0_ref[0, rows, :].astype(F32)
                             * (conv + vx_ref[0, rows, :].astype(F32) * sk)).astype(o_ref.dtype)


def _conv_fused(x0, vx, hr, hi, skip, g_fwd, g_inv):
    b, n, c = vx.shape
    n1 = n // DFT_N
    lanes = LANE
    fa, fo_r, fo_i = _conv_tables(n1)
    blk = pl.BlockSpec((1, n, lanes), lambda ci, bi: (bi, 0, ci))
    full = lambda a: pl.BlockSpec(a.shape, lambda ci, bi: (0,) * a.ndim)
    hbm = pl.BlockSpec(memory_space=pl.ANY)
    return pl.pallas_call(
        _conv_fused_kernel,
        out_shape=jax.ShapeDtypeStruct((b, n, c), BF16),
        grid=(c // lanes, b),
        in_specs=[blk, blk, pl.BlockSpec((1, lanes), lambda ci, bi: (0, ci)), full(g_fwd), full(g_inv),
                  hbm, hbm, hbm, hbm, hbm],
        out_specs=blk,
        scratch_shapes=[
            pltpu.VMEM((n1 * FFT_PITCH, lanes), F32),
            pltpu.VMEM((DFT_ROW_PAD * FFT_PITCH, lanes), F32),
            pltpu.VMEM((DFT_ROW_PAD * FFT_PITCH, lanes), F32),
            pltpu.VMEM(fa.shape, BF16), pltpu.VMEM(fo_r.shape, BF16), pltpu.VMEM(fo_i.shape, BF16),
            pltpu.VMEM((DFT_HALF, DFT_N, lanes), F32), pltpu.VMEM((DFT_HALF, DFT_N, lanes), F32),
        ],
        compiler_params=_cparams("arbitrary", "arbitrary"),
        name="long_conv_fused",
    )(vx, x0, skip.reshape(1, c), g_fwd, g_inv, fa, fo_r, fo_i, hr, hi)


def _filter_spec_kernel(k_hbm, gf_ref, fa_hbm, hr_ref, hi_ref, ks, ar, ai, fa_s, sem):
    ci = pl.program_id(0)
    lanes = ks.shape[1]
    lane0 = pl.multiple_of(ci * lanes, lanes)

    @pl.when(ci == 0)
    def _():
        pltpu.sync_copy(fa_hbm, fa_s)

    def plane_copy(p):
        return pltpu.make_async_copy(k_hbm.at[pl.ds(p * DFT_N, DFT_N), pl.ds(lane0, lanes)],
                                     ks.at[pl.ds(p * FFT_PITCH, DFT_N), :], sem.at[0])
    for p in range(DFT_N):
        plane_copy(p).start()
    for p in range(DFT_N):
        plane_copy(p).wait()

    def fwd(g, carry):
        n2s = [g * CONV_UNROLL + u for u in range(CONV_UNROLL)]
        xgs = [ks[pl.ds(n2, DFT_N, stride=FFT_PITCH), :].astype(BF16) for n2 in n2s]
        ys = [jnp.dot(fa_s[n2], xg, preferred_element_type=F32) for n2, xg in zip(n2s, xgs)]
        for n2, y in zip(n2s, ys):
            ar[pl.ds(n2, DFT_ROW_PAD, stride=FFT_PITCH), :] = y[:DFT_ROW_PAD]
            ai[pl.ds(n2, DFT_ROW_PAD, stride=FFT_PITCH), :] = y[DFT_ROW_PAD:]
        return carry
    lax.fori_loop(0, DFT_N // CONV_UNROLL, fwd, 0)

    def spec(g, carry):
        k1s = [jnp.minimum(g * CONV_UNROLL_MID + u, DFT_HALF - 1) for u in range(CONV_UNROLL_MID)]
        bases = [pl.multiple_of(k1 * FFT_PITCH, 8) for k1 in k1s]
        a2s = [jnp.concatenate([ar[pl.ds(bs, DFT_N), :], ai[pl.ds(bs, DFT_N), :]], axis=0).astype(BF16)
               for bs in bases]
        xs_ = [jnp.dot(gf_ref[...], a2, preferred_element_type=F32) for a2 in a2s]
        for k1, x in zip(k1s, xs_):
            hr_ref[k1] = x[:DFT_N]
            hi_ref[k1] = x[DFT_N:]
        return carry
    lax.fori_loop(0, -(-DFT_HALF // CONV_UNROLL_MID), spec, 0)


def _filter_spectrum_fused(kfull, g_fwd):
    rows, c = kfull.shape
    lanes = LANE
    fa = _conv_tables(rows // DFT_N)[0]
    out = pl.BlockSpec((DFT_HALF, DFT_N, lanes), lambda ci: (0, 0, ci))
    hbm = pl.BlockSpec(memory_space=pl.ANY)
    return pl.pallas_call(
        _filter_spec_kernel,
        out_shape=[jax.ShapeDtypeStruct((DFT_HALF, DFT_N, c), F32)] * 2,
        grid=(c // lanes,),
        in_specs=[hbm, pl.BlockSpec(g_fwd.shape, lambda ci: (0, 0)), hbm],
        out_specs=[out, out],
        scratch_shapes=[
            pltpu.VMEM((DFT_N * FFT_PITCH, lanes), F32),
            pltpu.VMEM((DFT_ROW_PAD * FFT_PITCH, lanes), F32),
            pltpu.VMEM((DFT_ROW_PAD * FFT_PITCH, lanes), F32),
            pltpu.VMEM(fa.shape, BF16),
            pltpu.SemaphoreType.DMA((1,)),
        ],
        compiler_params=_cparams("arbitrary"),
        name="filter_spectrum_fused",
    )(kfull, g_fwd, fa)


def _hyena_long_conv(x0, vx, kfull, skip):
    g_fwd, g_inv = _dft_consts()
    hr, hi = _filter_spectrum_fused(kfull, g_fwd)
    return _conv_fused(x0, vx, hr, hi, skip, g_fwd, g_inv)


def _l1c_kernel(y_ref, h_ref, g1_ref, sc2_ref, sh2_ref, w_ref, lg_ref, lb_ref, rw_ref, ho_ref, z_ref, lo_ref):
    y = jnp.dot(y_ref[0], w_ref[...], preferred_element_type=F32)
    hh = _layer_norm(ALPHA * h_ref[0] + g1_ref[0] * y, lg_ref[...], lb_ref[...])
    ho_ref[0] = hh
    z = hh * (1.0 + sc2_ref[0]) + sh2_ref[0]
    segs = z.shape[1] // LANE
    for s in range(segs):
        z_ref[0, pl.ds(s, z.shape[0], stride=segs), :] = z[:, s * LANE:(s + 1) * LANE]
    z_hi = z.astype(BF16)
    z_lo = (z - z_hi.astype(F32)).astype(BF16)
    bdot = functools.partial(jnp.dot, preferred_element_type=F32)
    lo_ref[0] = bdot(z_hi, rw_ref[0]) + (bdot(z_hi, rw_ref[1]) + bdot(z_lo, rw_ref[0]))


def _l1c(y, h, g1, sc2, sh2, out_w, ln_g, ln_b, router_w):
    b, n, d = h.shape
    tm = 512
    blk = pl.BlockSpec((1, tm, d), lambda bi, i: (bi, i, 0))
    vec = pl.BlockSpec((1, 1, d), lambda bi, i: (bi, 0, 0))
    full = lambda shape: pl.BlockSpec(shape, lambda bi, i: (0,) * len(shape))
    rw = jnp.pad(router_w, ((0, 0), (0, LANE - router_w.shape[1])))
    rw_hi = rw.astype(BF16)
    rw = jnp.stack([rw_hi, (rw - rw_hi.astype(F32)).astype(BF16)])
    segs = d // LANE
    return pl.pallas_call(
        _l1c_kernel,
        out_shape=[jax.ShapeDtypeStruct((b, n, d), F32), jax.ShapeDtypeStruct((b, n * segs, LANE), F32),
                   jax.ShapeDtypeStruct((b, n, LANE), F32)],
        grid=(b, n // tm),
        in_specs=[blk, blk, vec, vec, vec, full((d, d)), full((1, d)), full((1, d)), full((2, d, LANE))],
        out_specs=[blk, pl.BlockSpec((1, tm * segs, LANE), lambda bi, i: (bi, i, 0)),
                   pl.BlockSpec((1, tm, LANE), lambda bi, i: (bi, i, 0))],
        compiler_params=_cparams("parallel", "parallel"),
        name="hyena_out_ln_router",
    )(y, h, g1, sc2, sh2, out_w.astype(BF16), ln_g.reshape(1, -1), ln_b.reshape(1, -1), rw)


def _take(a, idx):
    return a.at[idx].get(mode="promise_in_bounds")


def _route(logits, tm):
    t = logits.shape[0]
    top_v, top_i = lax.top_k(logits, TOP_K)
    gates = jax.nn.softmax(top_v, axis=-1)
    flat_e = top_i.reshape(-1).astype(jnp.int32)
    flat_g = gates.reshape(-1)
    n_sel = t * TOP_K
    p = n_sel + N_EXPERTS * tm
    eids = jnp.arange(N_EXPERTS, dtype=jnp.int32)[None, :]
    onehot = (flat_e[:, None] == eids).astype(jnp.int32)
    csum = jnp.cumsum(onehot, axis=0)
    counts = csum[-1]
    rank = jnp.sum((csum - onehot) * onehot, axis=1)
    padded = ((counts + tm - 1) // tm) * tm
    end_p = jnp.cumsum(padded)
    start_p = end_p - padded
    start = jnp.cumsum(counts) - counts
    pos = (jnp.sum(onehot * start_p[None, :], axis=1) + rank).reshape(t, TOP_K)
    order = jnp.argsort(flat_e, stable=True).astype(jnp.int32)
    r = jnp.arange(p, dtype=jnp.int32)
    e_row = jnp.sum(r[:, None] >= end_p[None, :], axis=1).astype(jnp.int32)
    oh_r = (jnp.minimum(e_row, N_EXPERTS - 1)[:, None] == eids).astype(jnp.int32)
    j = r - jnp.sum(oh_r * start_p[None, :], axis=1)
    valid = (e_row < N_EXPERTS) & (j < jnp.sum(oh_r * counts[None, :], axis=1))
    src = jnp.clip(jnp.sum(oh_r * start[None, :], axis=1) + j, 0, n_sel - 1)
    flat_idx = _take(order, src)
    row_token = jnp.where(valid, flat_idx // TOP_K, 0)
    row_gate = jnp.where(valid, _take(flat_g, flat_idx), 0.0)
    tile_start = jnp.arange(p // tm, dtype=jnp.int32) * tm
    tile_expert = jnp.sum(tile_start[:, None] >= end_p[None, :], axis=1).astype(jnp.int32)
    tile_valid = (tile_expert < N_EXPERTS).astype(jnp.int32)
    tile_expert = jnp.minimum(tile_expert, N_EXPERTS - 1)
    return row_token, row_gate[:, None], tile_expert, tile_valid, pos


def kernel(x, c, ctx, c_ctx, ada_w, ada_b, ln_g, ln_b, mix_in_w, pool_w, pool_scale, q_norm, q_up, kv_norm, kv_up, mix_out_w, ffn_gate, ffn_up, ffn_down, hy_in_w, hy_conv_w, hy_conv_b, hy_fw1, hy_fb1, hy_fw2, hy_fb2, hy_fw3, hy_fb3, hy_fout, hy_freq, hy_skip, hy_out_w, router_w, moe_gate, moe_up, moe_down):
    b, n, d = x.shape
    t = b * n
    assert b + 1 <= 8
    s_rows = jnp.concatenate([c, c_ctx[None, :], jnp.zeros((8 - b - 1, d), F32)], axis=0)
    mod = _ada(s_rows, ada_w, ada_b)

    def chunks(l, rows):
        m = mod[l, rows].reshape(-1, 6, d)
        return [m[:, k][:, None, :] for k in range(6)]

    sh1, sc1, g1, sh2, sc2, g2 = chunks(0, slice(0, b))
    shc, scc = chunks(0, slice(b, b + 1))[:2]
    pool_u, q, k, v = _l0a(x, ctx, sc1, sh1, scc, shc, mix_in_w[0], q_norm[0], q_up[0], kv_norm[0], kv_up[0])
    attn_o = _attention(q, k, v, n)
    h1, z1 = _l0b(pool_u, attn_o, x, g1, sc2, sh2, pool_w[0], pool_scale[0], mix_out_w[0], ln_g[0, 0], ln_b[0, 0])
    f0 = _ffn_dense(z1.reshape(t, d), ffn_gate.astype(BF16), ffn_up.astype(BF16), ffn_down.astype(BF16), 512, 1408)
    sh1, sc1, g1b, sh2b, sc2b, g2b = chunks(1, slice(0, b))
    h2, u2 = _resid_ln(h1, [f0.reshape(b, n, d)], g2, ln_g[0, 1], ln_b[0, 1], mod=(sc1, sh1))

    x0, vx = _l1a(u2, hy_in_w[0], hy_conv_w[0], hy_conv_b[0])
    kfull = _hyena_filters(n, hy_fw1[0], hy_fb1[0], hy_fw2[0], hy_fb2[0], hy_fw3[0], hy_fb3[0],
                           hy_fout[0], hy_freq[0])
    yl = _hyena_long_conv(x0, vx, kfull, hy_skip[0])
    h3, z3, logits = _l1c(yl, h2, g1b, sc2b, sh2b, hy_out_w[0], ln_g[1, 0], ln_b[1, 0], router_w[0])

    tm_moe = 1024
    row_token, row_gate, tile_expert, tile_valid, pos = _route(logits.reshape(t, LANE)[:, :N_EXPERTS], tm_moe)
    ys = _moe_grouped(z3.reshape(t, d // LANE, LANE), row_token, _cast_bf16(moe_gate[0]), _cast_bf16(moe_up[0]),
                      _cast_bf16(moe_down[0]), row_gate, tile_expert, tile_valid, tm_moe, 896)
    y_a = _take(ys, pos[:, 0]).reshape(b, n, d)
    y_b = _take(ys, pos[:, 1]).reshape(b, n, d)
    (h4,) = _resid_ln(h3, [y_a, y_b], g2b, ln_g[1, 1], ln_b[1, 1])
    return h4
```

```python
import functools
import math

import numpy as np
import jax
import jax.numpy as jnp
from jax import lax
from jax.experimental import pallas as pl
from jax.experimental.pallas import tpu as pltpu

F32 = jnp.float32
BF16 = jnp.bfloat16
HIGHEST = lax.Precision.HIGHEST

D_MODEL = 1024
GRID_W = 64
POOL_W = 512
POOL_WINDOWS = (2, 4, 8, 16)
POOL_GC = 128
MLA_HEADS = 8
QK_NOPE = 64
QK_ROPE = 32
V_HEAD = 64
Q_LORA = 256
KV_LORA = 256
ROPE_AXIS = 16
ROPE_BASE = 10000.0
N_EXPERTS = 8
TOP_K = 2
HY_BANDS = 16
HY_EMB = 1 + 2 * HY_BANDS
HY_FAST_DECAY = 0.3
HY_SLOW_DECAY = 1.5
HY_TARGET = 1e-2
HY_MIN_DECAY = math.log(HY_TARGET) / HY_SLOW_DECAY
HY_MAX_DECAY = math.log(HY_TARGET) / HY_FAST_DECAY
LN_EPS = 1e-5
RMS_EPS = 1e-6
DEPTH = 2
ALPHA = (2.0 * DEPTH) ** 0.25

LANE = 128
HEAD_SLOT = 128
DFT_N = 128
DFT_HALF = DFT_N // 2 + 1
VMEM_LIMIT = 56 * 1024 * 1024
ATTN_LOOKAHEAD = 2


def _cparams(*sem):
    return pltpu.CompilerParams(dimension_semantics=sem, vmem_limit_bytes=VMEM_LIMIT)


def _layer_norm(v, g, b):
    mu = jnp.mean(v, axis=-1, keepdims=True)
    c = v - mu
    var = jnp.mean(c * c, axis=-1, keepdims=True)
    return c * lax.rsqrt(var + LN_EPS) * g + b


def _rms_norm(v, g):
    return v * lax.rsqrt(jnp.mean(v * v, axis=-1, keepdims=True) + RMS_EPS) * g


def _ada_kernel(s_ref, w_ref, b_ref, o_ref):
    s = s_ref[...]
    s = s * jax.nn.sigmoid(s)
    o_ref[0] = jnp.dot(s, w_ref[0], preferred_element_type=F32, precision=HIGHEST) + b_ref[0]


def _ada(s_rows, ada_w, ada_b):
    depth, d, n6 = ada_w.shape
    tn = 768
    return pl.pallas_call(
        _ada_kernel,
        out_shape=jax.ShapeDtypeStruct((depth, 8, n6), F32),
        grid=(depth, n6 // tn),
        in_specs=[
            pl.BlockSpec((8, d), lambda l, j: (0, 0)),
            pl.BlockSpec((1, d, tn), lambda l, j: (l, 0, j)),
            pl.BlockSpec((1, 1, tn), lambda l, j: (l, 0, j)),
        ],
        out_specs=pl.BlockSpec((1, 8, tn), lambda l, j: (l, 0, j)),
        compiler_params=_cparams("parallel", "parallel"),
        name="ada_mod",
    )(s_rows, ada_w, ada_b.reshape(depth, 1, n6))


def _l0a_kernel(x_ref, c_ref, sc_ref, sh_ref, scc_ref, shc_ref, win_ref, qn_ref, kvn_ref,
                wqa_ref, wqb_ref, wk_ref, wv_ref, cos_ref, sin_ref, cost_ref, sint_ref,
                pu_ref, q_ref, k_ref, v_ref, u_scr, *, n_lat_tiles):
    i = pl.program_id(1)

    @pl.when(i < n_lat_tiles)
    def _():
        u_scr[...] = (x_ref[0] * (1.0 + sc_ref[0]) + sh_ref[0]).astype(BF16)

    @pl.when(i >= n_lat_tiles)
    def _():
        u_scr[...] = (c_ref[0] * (1.0 + scc_ref[0]) + shc_ref[0]).astype(BF16)

    proj = jnp.dot(u_scr[...], win_ref[...], preferred_element_type=F32)
    pu_ref[0] = proj[:, :POOL_W]
    cos = cos_ref[...]
    sin = sin_ref[...]
    q0 = POOL_W
    kv0 = POOL_W + Q_LORA
    r0 = kv0 + KV_LORA
    qn = _rms_norm(proj[:, q0:kv0], qn_ref[...]).astype(BF16)
    kvn = _rms_norm(proj[:, kv0:r0], kvn_ref[...]).astype(BF16)
    nt_dims = (((1,), (1,)), ((), ()))
    qa = lax.dot_general(wqa_ref[...], qn, nt_dims, preferred_element_type=F32)
    qb = lax.dot_general(wqb_ref[...], qn, nt_dims, preferred_element_type=F32)
    vt = lax.dot_general(wv_ref[...], kvn, nt_dims, preferred_element_type=F32)
    kn = jnp.dot(kvn, wk_ref[...], preferred_element_type=F32)
    krot = proj[:, r0:r0 + HEAD_SLOT] * cos + proj[:, r0 + HEAD_SLOT:r0 + 2 * HEAD_SLOT] * sin
    cos_t = cost_ref[...]
    sin_t = sint_ref[...]
    for h in range(MLA_HEADS):
        sl = slice(h * HEAD_SLOT, (h + 1) * HEAD_SLOT)
        q_ref[0, h] = (qa[sl] * cos_t + qb[sl] * sin_t).astype(BF16)
        k_ref[0, h] = (kn[:, sl] + krot).astype(BF16)
        v_ref[0, h] = vt[h * V_HEAD:(h + 1) * V_HEAD].astype(BF16)


def _rope_swap_index():
    half = ROPE_AXIS // 2
    idx = []
    for a in range(2):
        base = a * ROPE_AXIS
        idx += list(range(base + half, base + ROPE_AXIS)) + list(range(base, base + half))
    return np.array(idx)


def _rope_tables(n, n_ctx):
    rows = n // GRID_W
    r = jnp.repeat(jnp.arange(rows, dtype=F32), GRID_W)
    col = jnp.tile(jnp.arange(GRID_W, dtype=F32), rows)
    inv = ROPE_BASE ** (-jnp.arange(0, ROPE_AXIS, 2, dtype=F32) / ROPE_AXIS)
    ang_r = r[:, None] * inv
    ang_c = col[:, None] * inv
    cos32 = jnp.concatenate([jnp.cos(ang_r), jnp.cos(ang_r), jnp.cos(ang_c), jnp.cos(ang_c)], axis=-1)
    sin32 = jnp.concatenate([-jnp.sin(ang_r), jnp.sin(ang_r), -jnp.sin(ang_c), jnp.sin(ang_c)], axis=-1)
    pad = HEAD_SLOT - QK_NOPE - QK_ROPE
    cos = jnp.concatenate([jnp.ones((n, QK_NOPE), F32), cos32, jnp.ones((n, pad), F32)], axis=-1)
    sin = jnp.concatenate([jnp.zeros((n, QK_NOPE), F32), sin32, jnp.zeros((n, pad), F32)], axis=-1)
    cos = jnp.concatenate([cos, jnp.ones((n_ctx, HEAD_SLOT), F32)], axis=0)
    sin = jnp.concatenate([sin, jnp.zeros((n_ctx, HEAD_SLOT), F32)], axis=0)
    return cos, sin


def _l0a_weights(in_w, q_up, kv_up):
    swap = _rope_swap_index()
    d = in_w.shape[0]
    r0 = POOL_W + Q_LORA + KV_LORA
    w_rope = in_w[:, r0:]
    pad_l = jnp.zeros((d, QK_NOPE), F32)
    pad_r = jnp.zeros((d, HEAD_SLOT - QK_NOPE - QK_ROPE), F32)
    kr_a = jnp.concatenate([pad_l, w_rope, pad_r], axis=1)
    kr_b = jnp.concatenate([pad_l, w_rope[:, swap], pad_r], axis=1)
    w_in = jnp.concatenate([in_w[:, :r0], kr_a, kr_b], axis=1).astype(BF16)

    scale = (QK_NOPE + QK_ROPE) ** -0.5 * math.log2(math.e)
    qu = q_up.reshape(Q_LORA, MLA_HEADS, QK_NOPE + QK_ROPE) * scale
    zpad = jnp.zeros((Q_LORA, MLA_HEADS, HEAD_SLOT - QK_NOPE - QK_ROPE), F32)
    wq_a = jnp.concatenate([qu, zpad], axis=-1).reshape(Q_LORA, MLA_HEADS * HEAD_SLOT).T.astype(BF16)
    wq_b = jnp.concatenate([jnp.zeros((Q_LORA, MLA_HEADS, QK_NOPE), F32), qu[..., QK_NOPE:][..., swap], zpad],
                           axis=-1).reshape(Q_LORA, MLA_HEADS * HEAD_SLOT).T.astype(BF16)
    kvu = kv_up.reshape(KV_LORA, MLA_HEADS, QK_NOPE + V_HEAD)
    wk = jnp.concatenate([kvu[..., :QK_NOPE], jnp.zeros((KV_LORA, MLA_HEADS, HEAD_SLOT - QK_NOPE), F32)],
                         axis=-1).reshape(KV_LORA, MLA_HEADS * HEAD_SLOT).astype(BF16)
    wv_t = kvu[..., QK_NOPE:].reshape(KV_LORA, MLA_HEADS * V_HEAD).T.astype(BF16)
    return w_in, wq_a, wq_b, wk, wv_t


def _l0a(x, ctx, sc, sh, scc, shc, in_w, q_norm, q_up, kv_norm, kv_up):
    b, n, d = x.shape
    n_ctx = ctx.shape[1]
    tm = 256
    nl = n // tm
    nt = (n + n_ctx) // tm
    w_in, wq_a, wq_b, wk, wv_t = _l0a_weights(in_w, q_up, kv_up)
    cos, sin = _rope_tables(n, n_ctx)
    hw = MLA_HEADS * HEAD_SLOT
    full = lambda shape: pl.BlockSpec(shape, lambda bi, i: (0,) * len(shape))
    vec = pl.BlockSpec((1, 1, d), lambda bi, i: (bi, 0, 0))
    return pl.pallas_call(
        functools.partial(_l0a_kernel, n_lat_tiles=nl),
        out_shape=[
            jax.ShapeDtypeStruct((b, n + n_ctx, POOL_W), F32),
            jax.ShapeDtypeStruct((b, MLA_HEADS, HEAD_SLOT, n + n_ctx), BF16),
            jax.ShapeDtypeStruct((b, MLA_HEADS, n + n_ctx, HEAD_SLOT), BF16),
            jax.ShapeDtypeStruct((b, MLA_HEADS, V_HEAD, n + n_ctx), BF16),
        ],
        grid=(b, nt),
        in_specs=[
            pl.BlockSpec((1, tm, d), lambda bi, i: (bi, jnp.minimum(i, nl - 1), 0)),
            pl.BlockSpec((1, tm, d), lambda bi, i: (bi, jnp.maximum(i - nl, 0), 0)),
            vec, vec, full((1, 1, d)), full((1, 1, d)),
            full(w_in.shape), full((1, Q_LORA)), full((1, KV_LORA)),
            full((hw, Q_LORA)), full((hw, Q_LORA)), full((KV_LORA, hw)), full((MLA_HEADS * V_HEAD, KV_LORA)),
            pl.BlockSpec((tm, HEAD_SLOT), lambda bi, i: (i, 0)),
            pl.BlockSpec((tm, HEAD_SLOT), lambda bi, i: (i, 0)),
            pl.BlockSpec((HEAD_SLOT, tm), lambda bi, i: (0, i)),
            pl.BlockSpec((HEAD_SLOT, tm), lambda bi, i: (0, i)),
        ],
        out_specs=[
            pl.BlockSpec((1, tm, POOL_W), lambda bi, i: (bi, i, 0)),
            pl.BlockSpec((1, MLA_HEADS, HEAD_SLOT, tm), lambda bi, i: (bi, 0, 0, i)),
            pl.BlockSpec((1, MLA_HEADS, tm, HEAD_SLOT), lambda bi, i: (bi, 0, i, 0)),
            pl.BlockSpec((1, MLA_HEADS, V_HEAD, tm), lambda bi, i: (bi, 0, 0, i)),
        ],
        scratch_shapes=[pltpu.VMEM((tm, d), BF16)],
        compiler_params=_cparams("parallel", "arbitrary"),
        name="l0_in_proj",
    )(x, ctx, sc, sh, scc, shc, w_in, q_norm.reshape(1, -1), kv_norm.reshape(1, -1),
      wq_a, wq_b, wk, wv_t, cos, sin, cos.T, sin.T)


def _attn_kernel(q_ref, k_ref, v_ref, o_ref, m_ref, l_ref, acc_ref, *, nk):
    ki = pl.program_id(2)

    @pl.when(ki == 0)
    def _():
        m_ref[...] = jnp.full(m_ref.shape, -jnp.inf, F32)
        l_ref[...] = jnp.zeros(l_ref.shape, F32)
        acc_ref[...] = jnp.zeros(acc_ref.shape, F32)

    def scores(h):
        return jnp.dot(k_ref[0, h], q_ref[0, h], preferred_element_type=F32)

    pending = [scores(h) for h in range(ATTN_LOOKAHEAD)]
    for h in range(MLA_HEADS):
        if h + ATTN_LOOKAHEAD < MLA_HEADS:
            pending.append(scores(h + ATTN_LOOKAHEAD))
        st = pending.pop(0)
        m_prev = m_ref[h]
        m_new = jnp.maximum(m_prev, jnp.max(st, axis=0, keepdims=True))
        a = jnp.exp2(m_prev - m_new)
        p = jnp.exp2(st - m_new)
        l_ref[h] = a * l_ref[h] + jnp.sum(p, axis=0, keepdims=True)
        pv = jnp.dot(v_ref[0, h], p.astype(BF16), preferred_element_type=F32)
        rows = slice(h * V_HEAD, (h + 1) * V_HEAD)
        acc_ref[rows, :] = a * acc_ref[rows, :] + pv
        m_ref[h] = m_new

    @pl.when(ki == nk - 1)
    def _():
        for h in range(MLA_HEADS):
            rows = slice(h * V_HEAD, (h + 1) * V_HEAD)
            acc_ref[rows, :] = acc_ref[rows, :] / l_ref[h]
        o_ref[0] = acc_ref[...].T.astype(o_ref.dtype)


def _attention(q, k, vt, n):
    b, h, n_all, _ = k.shape
    tq, tk = 512, 1408
    nk = n_all // tk
    return pl.pallas_call(
        functools.partial(_attn_kernel, nk=nk),
        out_shape=jax.ShapeDtypeStruct((b, n, h * V_HEAD), BF16),
        grid=(b, n // tq, nk),
        in_specs=[
            pl.BlockSpec((1, h, HEAD_SLOT, tq), lambda bi, qi, ki: (bi, 0, 0, qi)),
            pl.BlockSpec((1, h, tk, HEAD_SLOT), lambda bi, qi, ki: (bi, 0, ki, 0)),
            pl.BlockSpec((1, h, V_HEAD, tk), lambda bi, qi, ki: (bi, 0, 0, ki)),
        ],
        out_specs=pl.BlockSpec((1, tq, h * V_HEAD), lambda bi, qi, ki: (bi, qi, 0)),
        scratch_shapes=[
            pltpu.VMEM((h, 1, tq), F32),
            pltpu.VMEM((h, 1, tq), F32),
            pltpu.VMEM((h * V_HEAD, tq), F32),
        ],
        compiler_params=_cparams("parallel", "parallel", "arbitrary"),
        name="mla_attention",
    )(q, k, vt)


def _l0b_kernel(pp_ref, pc_ref, pn_ref, o_ref, x_ref, g1_ref, sc2_ref, sh2_ref, pw_ref, ps_ref, ow_ref,
                lg_ref, lb_ref, h_ref, z_ref, *, tm, n):
    i = pl.program_id(1)
    halo = 8
    ext = jnp.concatenate([pp_ref[0], pc_ref[0], pn_ref[0]], axis=0)
    pos = i * tm - halo + lax.broadcasted_iota(jnp.int32, (tm + 2 * halo, 1), 0)
    ext = jnp.where((pos >= 0) & (pos < n), ext, 0.0)
    t = i * tm + lax.broadcasted_iota(jnp.int32, (tm, 1), 0)
    y = jnp.dot(o_ref[0], ow_ref[POOL_W:, :], preferred_element_type=F32)
    for g, w in enumerate(POOL_WINDOWS):
        hw = w // 2
        e = ext[:, g * POOL_GC:(g + 1) * POOL_GC]
        s = e
        width = 1
        while width < w:
            s = s[:s.shape[0] - width] + s[width:]
            width *= 2
        win = s[halo - hw:halo - hw + tm]
        cnt = (jnp.minimum(t + hw, n) - jnp.maximum(t - hw, 0)).astype(F32)
        dd = (win / cnt - e[halo:halo + tm]).astype(BF16)
        yg = jnp.dot(dd, pw_ref[g], preferred_element_type=F32) * ps_ref[:, g * POOL_GC:(g + 1) * POOL_GC]
        y = y + jnp.dot(yg.astype(BF16), ow_ref[g * POOL_GC:(g + 1) * POOL_GC, :], preferred_element_type=F32)
    hh = _layer_norm(ALPHA * x_ref[0] + g1_ref[0] * y, lg_ref[...], lb_ref[...])
    h_ref[0] = hh
    z_ref[0] = (hh * (1.0 + sc2_ref[0]) + sh2_ref[0]).astype(BF16)


def _l0b(pool_u, attn_o, x, g1, sc2, sh2, pool_w, pool_scale, out_w, ln_g, ln_b):
    b, n, d = x.shape
    tm = 512
    hb = tm // 8
    vec = pl.BlockSpec((1, 1, d), lambda bi, i: (bi, 0, 0))
    full = lambda shape: pl.BlockSpec(shape, lambda bi, i: (0,) * len(shape))
    return pl.pallas_call(
        functools.partial(_l0b_kernel, tm=tm, n=n),
        out_shape=[jax.ShapeDtypeStruct((b, n, d), F32), jax.ShapeDtypeStruct((b, n, d), BF16)],
        grid=(b, n // tm),
        in_specs=[
            pl.BlockSpec((1, 8, POOL_W), lambda bi, i: (bi, jnp.maximum(i * hb - 1, 0), 0)),
            pl.BlockSpec((1, tm, POOL_W), lambda bi, i: (bi, i, 0)),
            pl.BlockSpec((1, 8, POOL_W), lambda bi, i: (bi, (i + 1) * hb, 0)),
            pl.BlockSpec((1, tm, POOL_W), lambda bi, i: (bi, i, 0)),
            pl.BlockSpec((1, tm, d), lambda bi, i: (bi, i, 0)),
            vec, vec, vec,
            full(pool_w.shape), full((1, POOL_W)), full(out_w.shape), full((1, d)), full((1, d)),
        ],
        out_specs=[pl.BlockSpec((1, tm, d), lambda bi, i: (bi, i, 0)),
                   pl.BlockSpec((1, tm, d), lambda bi, i: (bi, i, 0))],
        compiler_params=_cparams("parallel", "parallel"),
        name="l0_pool_out_ln",
    )(pool_u, pool_u, pool_u, attn_o, x, g1, sc2, sh2, pool_w.astype(BF16), pool_scale.reshape(1, -1),
      out_w.astype(BF16), ln_g.reshape(1, -1), ln_b.reshape(1, -1))


def _swiglu_step(z, wg_ref, wu_ref, wd_ref, acc_ref):
    g = jnp.dot(z, wg_ref[0].astype(BF16), preferred_element_type=F32)
    u = jnp.dot(z, wu_ref[0].astype(BF16), preferred_element_type=F32)
    a = (g * jax.nn.sigmoid(g) * u).astype(BF16)
    acc_ref[...] += jnp.dot(a, wd_ref[0].astype(BF16), preferred_element_type=F32)


def _ffn_kernel(z_ref, wg_ref, wu_ref, wd_ref, o_ref, acc_ref, *, nj):
    j = pl.program_id(1)

    @pl.when(j == 0)
    def _():
        acc_ref[...] = jnp.zeros(acc_ref.shape, F32)

    _swiglu_step(z_ref[...], wg_ref, wu_ref, wd_ref, acc_ref)

    @pl.when(j == nj - 1)
    def _():
        o_ref[...] = acc_ref[...]


def _ffn_dense(z, wg, wu, wd, tm, tf):
    t, d = z.shape
    f = wg.shape[-1]
    nj = f // tf
    return pl.pallas_call(
        functools.partial(_ffn_kernel, nj=nj),
        out_shape=jax.ShapeDtypeStruct((t, d), F32),
        grid=(t // tm, nj),
        in_specs=[
            pl.BlockSpec((tm, d), lambda i, j: (i, 0)),
            pl.BlockSpec((1, d, tf), lambda i, j: (0, 0, j)),
            pl.BlockSpec((1, d, tf), lambda i, j: (0, 0, j)),
            pl.BlockSpec((1, tf, d), lambda i, j: (0, j, 0)),
        ],
        out_specs=pl.BlockSpec((tm, d), lambda i, j: (i, 0)),
        scratch_shapes=[pltpu.VMEM((tm, d), F32)],
        compiler_params=_cparams("parallel", "arbitrary"),
        name="swiglu_dense",
    )(z, wg, wu, wd)


def _cast_kernel(x_ref, o_ref):
    o_ref[...] = x_ref[...].astype(o_ref.dtype)


def _cast_bf16(w):
    e, r, c = w.shape
    tr = 512
    blk = pl.BlockSpec((1, tr, c), lambda ei, ri: (ei, ri, 0))
    return pl.pallas_call(
        _cast_kernel,
        out_shape=jax.ShapeDtypeStruct(w.shape, BF16),
        grid=(e, r // tr),
        in_specs=[blk],
        out_specs=blk,
        compiler_params=_cparams("parallel", "parallel"),
        name="cast_bf16",
    )(w)


MOE_LOOKAHEAD = 2


def _moe_kernel(te_ref, tv_ref, *refs, nj, tm, n_tiles):
    tok_first = refs[:MOE_LOOKAHEAD]
    tokn_ref, z_hbm, wg_ref, wu_ref, wd_ref, gate_ref, o_ref, zbuf, zb16, sem, acc_ref = refs[MOE_LOOKAHEAD:]
    i = pl.program_id(0)
    j = pl.program_id(1)
    slots = zbuf.shape[0]
    slot = lax.rem(i, slots)
    nxt = lax.rem(i + MOE_LOOKAHEAD, slots)
    segs = z_hbm.shape[1]
    per_step = zbuf.shape[1] // (segs * nj)

    def row_copy(tok_ref, row, s, priority=0):
        tok = tok_ref[0, 0, jnp.minimum(row, tm - 1)]
        pltpu.async_copy(z_hbm.at[tok], zbuf.at[s, pl.ds(row * segs, segs), :], sem.at[s], priority=priority)

    def wait_slot(s):
        pltpu.make_async_copy(zbuf.at[s], zbuf.at[s], sem.at[s]).wait()

    def issue_next():
        for r in range(per_step):
            row_copy(tokn_ref, j * per_step + r, nxt, priority=1)

    @pl.when((i == 0) & (j == 0))
    def _():
        for k, tok_ref in enumerate(tok_first):
            def body(r, carry):
                row_copy(tok_ref, r, k)
                return carry
            lax.fori_loop(0, per_step * nj, body, 0)

    @pl.when(j == 0)
    def _():
        wait_slot(slot)
        for g in range(segs):
            zb16[:, g * LANE:(g + 1) * LANE] = zbuf.at[slot][pl.ds(g, tm, stride=segs), :].astype(BF16)
        acc_ref[...] = jnp.zeros(acc_ref.shape, F32)

    @pl.when(tv_ref[i] > 0)
    def _():
        issue_next()
        _swiglu_step(zb16[...], wg_ref, wu_ref, wd_ref, acc_ref)

    @pl.when(tv_ref[i] == 0)
    def _():
        issue_next()

    @pl.when(j == nj - 1)
    def _():
        o_ref[...] = acc_ref[...] * gate_ref[...]

    @pl.when((i == n_tiles - 1) & (j == nj - 1))
    def _():
        for k in range(1, MOE_LOOKAHEAD + 1):
            wait_slot(lax.rem(i + k, slots))


def _moe_grouped(z, row_token, wg, wu, wd, gate, tile_expert, tile_valid, tm, tf):
    t, segs, _ = z.shape
    d = segs * LANE
    p = row_token.shape[0]
    n_tiles = p // tm
    f = wg.shape[-1]
    nj = f // tf
    tok = row_token.reshape(n_tiles, 1, tm)
    buf_rows = nj * (-(-tm // (8 * nj)) * 8)
    smem_blk = lambda imap: pl.BlockSpec((1, 1, tm), imap, memory_space=pltpu.SMEM)
    return pl.pallas_call(
        functools.partial(_moe_kernel, nj=nj, tm=tm, n_tiles=n_tiles),
        out_shape=jax.ShapeDtypeStruct((p, d), F32),
        grid_spec=pltpu.PrefetchScalarGridSpec(
            num_scalar_prefetch=2,
            grid=(n_tiles, nj),
            in_specs=[
                *[smem_blk(functools.partial(lambda i, j, te, tv, k: (k, 0, 0), k=k)) for k in range(MOE_LOOKAHEAD)],
                smem_blk(lambda i, j, te, tv: (jnp.minimum(i + MOE_LOOKAHEAD, n_tiles - 1), 0, 0)),
                pl.BlockSpec(memory_space=pl.ANY),
                pl.BlockSpec((1, d, tf), lambda i, j, te, tv: (te[i], 0, j)),
                pl.BlockSpec((1, d, tf), lambda i, j, te, tv: (te[i], 0, j)),
                pl.BlockSpec((1, tf, d), lambda i, j, te, tv: (te[i], j, 0)),
                pl.BlockSpec((tm, 1), lambda i, j, te, tv: (i, 0)),
            ],
            out_specs=pl.BlockSpec((tm, d), lambda i, j, te, tv: (i, 0)),
            scratch_shapes=[
                pltpu.VMEM((MOE_LOOKAHEAD + 1, buf_rows * segs, LANE), F32),
                pltpu.VMEM((tm, d), BF16),
                pltpu.SemaphoreType.DMA((MOE_LOOKAHEAD + 1,)),
                pltpu.VMEM((tm, d), F32),
            ],
        ),
        compiler_params=_cparams("arbitrary", "arbitrary"),
        name="moe_swiglu_gather",
    )(tile_expert, tile_valid, *([tok] * (MOE_LOOKAHEAD + 1)), z, wg, wu, wd, gate)


def _resid_ln_kernel(*refs, n_y, with_mod):
    h_ref = refs[0]
    y_refs = refs[1:1 + n_y]
    g_ref, lg_ref, lb_ref = refs[1 + n_y:4 + n_y]
    rest = refs[4 + n_y:]
    y = y_refs[0][0]
    for r in y_refs[1:]:
        y = y + r[0]
    hh = _layer_norm(ALPHA * h_ref[0] + g_ref[0] * y, lg_ref[...], lb_ref[...])
    if with_mod:
        sc_ref, sh_ref, ho_ref, u_ref = rest
        ho_ref[0] = hh
        u_ref[0] = (hh * (1.0 + sc_ref[0]) + sh_ref[0]).astype(BF16)
    else:
        rest[0][0] = hh


def _resid_ln(h, ys, g, ln_g, ln_b, mod=None):
    b, n, d = h.shape
    tm = 1024
    blk = pl.BlockSpec((1, tm, d), lambda bi, i: (bi, i, 0))
    vec = pl.BlockSpec((1, 1, d), lambda bi, i: (bi, 0, 0))
    row = pl.BlockSpec((1, d), lambda bi, i: (0, 0))
    in_specs = [blk] + [blk] * len(ys) + [vec, row, row]
    args = [h, *ys, g, ln_g.reshape(1, -1), ln_b.reshape(1, -1)]
    out_shape = [jax.ShapeDtypeStruct((b, n, d), F32)]
    out_specs = [blk]
    if mod is not None:
        in_specs += [vec, vec]
        args += list(mod)
        out_shape.append(jax.ShapeDtypeStruct((b, n, d), BF16))
        out_specs.append(blk)
    return pl.pallas_call(
        functools.partial(_resid_ln_kernel, n_y=len(ys), with_mod=mod is not None),
        out_shape=out_shape,
        grid=(b, n // tm),
        in_specs=in_specs,
        out_specs=out_specs,
        compiler_params=_cparams("parallel", "parallel"),
        name="resid_ln",
    )(*args)


def _l1a_kernel(up_ref, uc_ref, un_ref, w_ref, cw_ref, cb_ref, x0_ref, vx_ref, *, tm, n):
    i = pl.program_id(1)
    halo = 16
    ext = jnp.concatenate([up_ref[0], uc_ref[0], un_ref[0]], axis=0)
    pos = i * tm - halo + lax.broadcasted_iota(jnp.int32, (tm + 2 * halo, 1), 0)
    ext = jnp.where((pos >= 0) & (pos < n), ext, jnp.zeros_like(ext))
    z = jnp.dot(ext, w_ref[...], preferred_element_type=F32)
    cw = cw_ref[...]
    zc = (cw[0:1] * z[halo - 1:halo - 1 + tm] + cw[1:2] * z[halo:halo + tm]
          + cw[2:3] * z[halo + 1:halo + 1 + tm] + cb_ref[...])
    c = zc.shape[1] // 3
    x0_ref[0] = zc[:, :c].astype(x0_ref.dtype)
    vx_ref[0] = (zc[:, 2 * c:] * zc[:, c:2 * c]).astype(vx_ref.dtype)


def _l1a(u, hy_in_w, conv_w, conv_b):
    b, n, d = u.shape
    c3 = hy_in_w.shape[1]
    c = c3 // 3
    tm = 512
    hb = tm // 16
    nb16 = n // 16
    full = lambda shape: pl.BlockSpec(shape, lambda bi, i: (0,) * len(shape))
    return pl.pallas_call(
        functools.partial(_l1a_kernel, tm=tm, n=n),
        out_shape=[jax.ShapeDtypeStruct((b, n, c), BF16), jax.ShapeDtypeStruct((b, n, c), BF16)],
        grid=(b, n // tm),
        in_specs=[
            pl.BlockSpec((1, 16, d), lambda bi, i: (bi, jnp.maximum(i * hb - 1, 0), 0)),
            pl.BlockSpec((1, tm, d), lambda bi, i: (bi, i, 0)),
            pl.BlockSpec((1, 16, d), lambda bi, i: (bi, jnp.minimum((i + 1) * hb, nb16 - 1), 0)),
            full((d, c3)), full((3, c3)), full((1, c3)),
        ],
        out_specs=[pl.BlockSpec((1, tm, c), lambda bi, i: (bi, i, 0)),
                   pl.BlockSpec((1, tm, c), lambda bi, i: (bi, i, 0))],
        compiler_params=_cparams("parallel", "parallel"),
        name="hyena_in_conv",
    )(u, u, u, hy_in_w.astype(BF16), conv_w, conv_b.reshape(1, -1))


def _filter_kernel(zf_ref, zb_ref, w1f_ref, w1b_ref, b1_ref, w2_ref, b2_ref, w3_ref, b3_ref, wof_ref, wob_ref,
                   fr_ref, dl_ref, k_ref):
    zf = zf_ref[...]
    zb = zb_ref[...]
    fr = fr_ref[...]
    dot = functools.partial(jnp.dot, preferred_element_type=F32, precision=HIGHEST)
    h = jnp.sin(fr * (dot(zf, w1f_ref[...]) + dot(zb, w1b_ref[...]) + b1_ref[...]))
    h = jnp.sin(fr * (dot(h, w2_ref[...]) + b2_ref[...]))
    h = jnp.sin(fr * (dot(h, w3_ref[...]) + b3_ref[...]))
    dl = dl_ref[...]
    h_hi = h.astype(BF16)
    h_lo = (h - h_hi.astype(F32)).astype(BF16)

    def dot3(w_ref):
        bdot = functools.partial(jnp.dot, preferred_element_type=F32)
        return bdot(h_hi, w_ref[0]) + (bdot(h_hi, w_ref[1]) + bdot(h_lo, w_ref[0]))

    k_ref[0] = dot3(wof_ref) * jnp.exp(-zf[:, 0:1] * dl)
    ob = dot3(wob_ref) * jnp.exp(-zb[:, 0:1] * dl)
    first = (pl.program_id(0) == 0) & (lax.broadcasted_iota(jnp.int32, (ob.shape[0], 1), 0) == 0)
    k_ref[1] = jnp.where(first, 0.0, ob)


def _hyena_filters(n, fw1, fb1, fw2, fb2, fw3, fb3, fout, freq):
    c = fout.shape[1] // 2
    fh = fw2.shape[0]
    bands = jnp.linspace(1e-4, HY_BANDS - 1, HY_BANDS, dtype=F32)[None, :]

    def features(p):
        w_ang = (2.0 * math.pi / n) * p
        return jnp.concatenate([p / (n - 1.0), jnp.cos(bands * w_ang), -jnp.sin(bands * w_ang),
                                jnp.zeros((n, LANE - HY_EMB), F32)], axis=-1)

    m = jnp.arange(n, dtype=F32)[:, None]
    z_f, z_b = features(m), features(n - m)
    deltas = jnp.abs(jnp.linspace(HY_MIN_DECAY, HY_MAX_DECAY, c, dtype=F32))[None, :]
    zeros = lambda r, cc: jnp.zeros((r, cc), F32)
    two = lambda a: jnp.concatenate([a.reshape(1, -1), a.reshape(1, -1)], axis=1)
    diag2 = lambda w: jnp.concatenate([jnp.concatenate([w, zeros(fh, fh)], axis=1),
                                       jnp.concatenate([zeros(fh, fh), w], axis=1)], axis=0)
    w1 = jnp.pad(fw1, ((0, LANE - fw1.shape[0]), (0, 0)))
    w1f = jnp.concatenate([w1, zeros(LANE, fh)], axis=1)
    w1b = jnp.concatenate([zeros(LANE, fh), w1], axis=1)

    def split(w):
        hi = w.astype(BF16)
        return jnp.stack([hi, (w - hi.astype(F32)).astype(BF16)])

    wof = split(jnp.concatenate([fout[:, :c], zeros(fh, c)], axis=0))
    wob = split(jnp.concatenate([zeros(fh, c), fout[:, c:]], axis=0))
    tm = 1024
    full = lambda shape: pl.BlockSpec(shape, lambda i: (0,) * len(shape))
    rows = pl.BlockSpec((tm, LANE), lambda i: (i, 0))
    out = pl.pallas_call(
        _filter_kernel,
        out_shape=jax.ShapeDtypeStruct((2, n, c), F32),
        grid=(n // tm,),
        in_specs=[rows, rows, full((LANE, LANE)), full((LANE, LANE)), full((1, LANE)),
                  full((LANE, LANE)), full((1, LANE)), full((LANE, LANE)), full((1, LANE)),
                  full((2, LANE, c)), full((2, LANE, c)), full((1, LANE)), full((1, c))],
        out_specs=pl.BlockSpec((2, tm, c), lambda i: (0, i, 0)),
        compiler_params=_cparams("parallel"),
        name="hyena_filters",
    )(z_f, z_b, w1f, w1b, two(fb1), diag2(fw2), two(fb2), diag2(fw3), two(fb3), wof, wob, two(freq), deltas)
    return out.reshape(2 * n, c)


def _dft_consts():
    kk = np.arange(DFT_N)[:, None] * np.arange(DFT_N)[None, :]
    c, s = np.cos(2.0 * np.pi * kk / DFT_N), np.sin(2.0 * np.pi * kk / DFT_N)
    g_fwd = np.block([[c, s], [-s, c]])
    g_inv = np.block([[c, -s], [s, c]])
    return jnp.asarray(g_fwd, BF16), jnp.asarray(g_inv, BF16)


DFT_ROW_PAD = 72
CONV_UNROLL = 16
CONV_UNROLL_MID = 5
FFT_PITCH = 136


def _conv_tables(n1):
    length = DFT_N * DFT_N
    k1 = np.arange(DFT_HALF)[None, :, None]
    n = (DFT_N * np.arange(n1)[None, None, :] + np.arange(DFT_N)[:, None, None])
    th = 2.0 * np.pi * k1 * n / length
    pad = np.zeros((DFT_N, DFT_ROW_PAD - DFT_HALF, n1))
    fa = np.concatenate([np.cos(th), pad, -np.sin(th), pad], axis=1)
    wgt = np.full((1, 1, DFT_HALF), 2.0)
    wgt[..., 0] = wgt[..., -1] = 1.0
    tht = np.transpose(th, (0, 2, 1))
    padk = np.zeros((DFT_N, n1, DFT_ROW_PAD - DFT_HALF))
    fo_r = np.concatenate([wgt * np.cos(tht) / length, padk], axis=2)
    fo_i = np.concatenate([-wgt * np.sin(tht) / length, padk], axis=2)
    return (jnp.asarray(fa, BF16), jnp.asarray(fo_r, BF16), jnp.asarray(fo_i, BF16))


def _conv_fused_kernel(vx_ref, x0_ref, sk_ref, gf_ref, gi_ref, fa_hbm, for_hbm, foi_hbm, hr_hbm, hi_hbm, o_ref,
                       xs, ar, ai, fa_s, for_s, foi_s, hr_s, hi_s):
    ci = pl.program_id(0)
    bi = pl.program_id(1)
    lanes = vx_ref.shape[2]
    n1c = vx_ref.shape[1] // DFT_N

    @pl.when((ci == 0) & (bi == 0))
    def _():
        pltpu.sync_copy(fa_hbm, fa_s)
        pltpu.sync_copy(for_hbm, for_s)
        pltpu.sync_copy(foi_hbm, foi_s)

    @pl.when(bi == 0)
    def _():
        lane0 = pl.multiple_of(ci * lanes, lanes)
        pltpu.sync_copy(hr_hbm.at[:, :, pl.ds(lane0, lanes)], hr_s)
        pltpu.sync_copy(hi_hbm.at[:, :, pl.ds(lane0, lanes)], hi_s)

    for p in range(n1c):
        xs[pl.ds(p * FFT_PITCH, DFT_N), :] = vx_ref[0, pl.ds(p * DFT_N, DFT_N), :].astype(F32)

    def fwd(g, carry):
        n2s = [g * CONV_UNROLL + u for u in range(CONV_UNROLL)]
        xgs = [xs[pl.ds(n2, n1c, stride=FFT_PITCH), :].astype(BF16) for n2 in n2s]
        ys = [jnp.dot(fa_s[n2], xg, preferred_element_type=F32) for n2, xg in zip(n2s, xgs)]
        for n2, y in zip(n2s, ys):
            ar[pl.ds(n2, DFT_ROW_PAD, stride=FFT_PITCH), :] = y[:DFT_ROW_PAD]
            ai[pl.ds(n2, DFT_ROW_PAD, stride=FFT_PITCH), :] = y[DFT_ROW_PAD:]
        return carry
    lax.fori_loop(0, DFT_N // CONV_UNROLL, fwd, 0)

    def mid_one(a2, k1):
        x = jnp.dot(gf_ref[...], a2, preferred_element_type=F32)
        xr, xi = x[:DFT_N], x[DFT_N:]
        hr = hr_s[k1]
        hi = hi_s[k1]
        z = jnp.concatenate([xr * hr - xi * hi, xr * hi + xi * hr], axis=0).astype(BF16)
        return jnp.dot(gi_ref[...], z, preferred_element_type=F32)

    def mid(g, carry):
        k1s = [jnp.minimum(g * CONV_UNROLL_MID + u, DFT_HALF - 1) for u in range(CONV_UNROLL_MID)]
        bases = [pl.multiple_of(k1 * FFT_PITCH, 8) for k1 in k1s]
        a2s = [jnp.concatenate([ar[pl.ds(bs, DFT_N), :], ai[pl.ds(bs, DFT_N), :]], axis=0).astype(BF16)
               for bs in bases]
        ys = [mid_one(a2, k1) for a2, k1 in zip(a2s, k1s)]
        for bs, y in zip(bases, ys):
            ar[pl.ds(bs, DFT_N), :] = y[:DFT_N]
            ai[pl.ds(bs, DFT_N), :] = y[DFT_N:]
        return carry
    lax.fori_loop(0, -(-DFT_HALF // CONV_UNROLL_MID), mid, 0)

    def inv(g, carry):
        n2s = [g * CONV_UNROLL + u for u in range(CONV_UNROLL)]
        brs = [ar[pl.ds(n2, DFT_ROW_PAD, stride=FFT_PITCH), :].astype(BF16) for n2 in n2s]
        bis = [ai[pl.ds(n2, DFT_ROW_PAD, stride=FFT_PITCH), :].astype(BF16) for n2 in n2s]
        ys = [jnp.dot(for_s[n2], br, preferred_element_type=F32) + jnp.dot(foi_s[n2], bi_, preferred_element_type=F32)
              for n2, br, bi_ in zip(n2s, brs, bis)]
        for n2, y in zip(n2s, ys):
            xs[pl.ds(n2, n1c, stride=FFT_PITCH), :] = y
        return carry
    lax.fori_loop(0, DFT_N // CONV_UNROLL, inv, 0)

    sk = sk_ref[...]
    for p in range(n1c):
        rows = pl.ds(p * DFT_N, DFT_N)
        conv = xs[pl.ds(p * FFT_PITCH, DFT_N), :]
        o_ref[0, rows, :] = (x0_ref[0, rows, :].astype(F32)
                             * (conv + vx_ref[0, rows, :].astype(F32) * sk)).astype(o_ref.dtype)


def _conv_fused(x0, vx, hr, hi, skip, g_fwd, g_inv):
    b, n, c = vx.shape
    n1 = n // DFT_N
    lanes = LANE
    fa, fo_r, fo_i = _conv_tables(n1)
    blk = pl.BlockSpec((1, n, lanes), lambda ci, bi: (bi, 0, ci))
    full = lambda a: pl.BlockSpec(a.shape, lambda ci, bi: (0,) * a.ndim)
    hbm = pl.BlockSpec(memory_space=pl.ANY)
    return pl.pallas_call(
        _conv_fused_kernel,
        out_shape=jax.ShapeDtypeStruct((b, n, c), BF16),
        grid=(c // lanes, b),
        in_specs=[blk, blk, pl.BlockSpec((1, lanes), lambda ci, bi: (0, ci)), full(g_fwd), full(g_inv),
                  hbm, hbm, hbm, hbm, hbm],
        out_specs=blk,
        scratch_shapes=[
            pltpu.VMEM((n1 * FFT_PITCH, lanes), F32),
            pltpu.VMEM((DFT_ROW_PAD * FFT_PITCH, lanes), F32),
            pltpu.VMEM((DFT_ROW_PAD * FFT_PITCH, lanes), F32),
            pltpu.VMEM(fa.shape, BF16), pltpu.VMEM(fo_r.shape, BF16), pltpu.VMEM(fo_i.shape, BF16),
            pltpu.VMEM((DFT_HALF, DFT_N, lanes), F32), pltpu.VMEM((DFT_HALF, DFT_N, lanes), F32),
        ],
        compiler_params=_cparams("arbitrary", "arbitrary"),
        name="long_conv_fused",
    )(vx, x0, skip.reshape(1, c), g_fwd, g_inv, fa, fo_r, fo_i, hr, hi)


def _filter_spec_kernel(k_hbm, gf_ref, fa_hbm, hr_ref, hi_ref, ks, ar, ai, fa_s, sem):
    ci = pl.program_id(0)
    lanes = ks.shape[1]
    lane0 = pl.multiple_of(ci * lanes, lanes)

    @pl.when(ci == 0)
    def _():
        pltpu.sync_copy(fa_hbm, fa_s)

    def plane_copy(p):
        return pltpu.make_async_copy(k_hbm.at[pl.ds(p * DFT_N, DFT_N), pl.ds(lane0, lanes)],
                                     ks.at[pl.ds(p * FFT_PITCH, DFT_N), :], sem.at[0])
    for p in range(DFT_N):
        plane_copy(p).start()
    for p in range(DFT_N):
        plane_copy(p).wait()

    def fwd(g, carry):
        n2s = [g * CONV_UNROLL + u for u in range(CONV_UNROLL)]
        xgs = [ks[pl.ds(n2, DFT_N, stride=FFT_PITCH), :].astype(BF16) for n2 in n2s]
        ys = [jnp.dot(fa_s[n2], xg, preferred_element_type=F32) for n2, xg in zip(n2s, xgs)]
        for n2, y in zip(n2s, ys):
            ar[pl.ds(n2, DFT_ROW_PAD, stride=FFT_PITCH), :] = y[:DFT_ROW_PAD]
            ai[pl.ds(n2, DFT_ROW_PAD, stride=FFT_PITCH), :] = y[DFT_ROW_PAD:]
        return carry
    lax.fori_loop(0, DFT_N // CONV_UNROLL, fwd, 0)

    def spec(g, carry):
        k1s = [jnp.minimum(g * CONV_UNROLL_MID + u, DFT_HALF - 1) for u in range(CONV_UNROLL_MID)]
        bases = [pl.multiple_of(k1 * FFT_PITCH, 8) for k1 in k1s]
        a2s = [jnp.concatenate([ar[pl.ds(bs, DFT_N), :], ai[pl.ds(bs, DFT_N), :]], axis=0).astype(BF16)
               for bs in bases]
        xs_ = [jnp.dot(gf_ref[...], a2, preferred_element_type=F32) for a2 in a2s]
        for k1, x in zip(k1s, xs_):
            hr_ref[k1] = x[:DFT_N]
            hi_ref[k1] = x[DFT_N:]
        return carry
    lax.fori_loop(0, -(-DFT_HALF // CONV_UNROLL_MID), spec, 0)


def _filter_spectrum_fused(kfull, g_fwd):
    rows, c = kfull.shape
    lanes = LANE
    fa = _conv_tables(rows // DFT_N)[0]
    out = pl.BlockSpec((DFT_HALF, DFT_N, lanes), lambda ci: (0, 0, ci))
    hbm = pl.BlockSpec(memory_space=pl.ANY)
    return pl.pallas_call(
        _filter_spec_kernel,
        out_shape=[jax.ShapeDtypeStruct((DFT_HALF, DFT_N, c), F32)] * 2,
        grid=(c // lanes,),
        in_specs=[hbm, pl.BlockSpec(g_fwd.shape, lambda ci: (0, 0)), hbm],
        out_specs=[out, out],
        scratch_shapes=[
            pltpu.VMEM((DFT_N * FFT_PITCH, lanes), F32),
            pltpu.VMEM((DFT_ROW_PAD * FFT_PITCH, lanes), F32),
            pltpu.VMEM((DFT_ROW_PAD * FFT_PITCH, lanes), F32),
            pltpu.VMEM(fa.shape, BF16),
            pltpu.SemaphoreType.DMA((1,)),
        ],
        compiler_params=_cparams("arbitrary"),
        name="filter_spectrum_fused",
    )(kfull, g_fwd, fa)


def _hyena_long_conv(x0, vx, kfull, skip):
    g_fwd, g_inv = _dft_consts()
    hr, hi = _filter_spectrum_fused(kfull, g_fwd)
    return _conv_fused(x0, vx, hr, hi, skip, g_fwd, g_inv)


def _l1c_kernel(y_ref, h_ref, g1_ref, sc2_ref, sh2_ref, w_ref, lg_ref, lb_ref, rw_ref, ho_ref, z_ref, lo_ref):
    y = jnp.dot(y_ref[0], w_ref[...], preferred_element_type=F32)
    hh = _layer_norm(ALPHA * h_ref[0] + g1_ref[0] * y, lg_ref[...], lb_ref[...])
    ho_ref[0] = hh
    z = hh * (1.0 + sc2_ref[0]) + sh2_ref[0]
    segs = z.shape[1] // LANE
    for s in range(segs):
        z_ref[0, pl.ds(s, z.shape[0], stride=segs), :] = z[:, s * LANE:(s + 1) * LANE]
    z_hi = z.astype(BF16)
    z_lo = (z - z_hi.astype(F32)).astype(BF16)
    bdot = functools.partial(jnp.dot, preferred_element_type=F32)
    lo_ref[0] = bdot(z_hi, rw_ref[0]) + (bdot(z_hi, rw_ref[1]) + bdot(z_lo, rw_ref[0]))


def _l1c(y, h, g1, sc2, sh2, out_w, ln_g, ln_b, router_w):
    b, n, d = h.shape
    tm = 512
    blk = pl.BlockSpec((1, tm, d), lambda bi, i: (bi, i, 0))
    vec = pl.BlockSpec((1, 1, d), lambda bi, i: (bi, 0, 0))
    full = lambda shape: pl.BlockSpec(shape, lambda bi, i: (0,) * len(shape))
    rw = jnp.pad(router_w, ((0, 0), (0, LANE - router_w.shape[1])))
    rw_hi = rw.astype(BF16)
    rw = jnp.stack([rw_hi, (rw - rw_hi.astype(F32)).astype(BF16)])
    segs = d // LANE
    return pl.pallas_call(
        _l1c_kernel,
        out_shape=[jax.ShapeDtypeStruct((b, n, d), F32), jax.ShapeDtypeStruct((b, n * segs, LANE), F32),
                   jax.ShapeDtypeStruct((b, n, LANE), F32)],
        grid=(b, n // tm),
        in_specs=[blk, blk, vec, vec, vec, full((d, d)), full((1, d)), full((1, d)), full((2, d, LANE))],
        out_specs=[blk, pl.BlockSpec((1, tm * segs, LANE), lambda bi, i: (bi, i, 0)),
                   pl.BlockSpec((1, tm, LANE), lambda bi, i: (bi, i, 0))],
        compiler_params=_cparams("parallel", "parallel"),
        name="hyena_out_ln_router",
    )(y, h, g1, sc2, sh2, out_w.astype(BF16), ln_g.reshape(1, -1), ln_b.reshape(1, -1), rw)


def _take(a, idx):
    return a.at[idx].get(mode="promise_in_bounds")


def _route(logits, tm):
    t = logits.shape[0]
    top_v, top_i = lax.top_k(logits, TOP_K)
    gates = jax.nn.softmax(top_v, axis=-1)
    flat_e = top_i.reshape(-1).astype(jnp.int32)
    flat_g = gates.reshape(-1)
    n_sel = t * TOP_K
    p = n_sel + N_EXPERTS * tm
    eids = jnp.arange(N_EXPERTS, dtype=jnp.int32)[None, :]
    onehot = (flat_e[:, None] == eids).astype(jnp.int32)
    csum = jnp.cumsum(onehot, axis=0)
    counts = csum[-1]
    rank = jnp.sum((csum - onehot) * onehot, axis=1)
    padded = ((counts + tm - 1) // tm) * tm
    end_p = jnp.cumsum(padded)
    start_p = end_p - padded
    start = jnp.cumsum(counts) - counts
    pos = (jnp.sum(onehot * start_p[None, :], axis=1) + rank).reshape(t, TOP_K)
    order = jnp.argsort(flat_e, stable=True).astype(jnp.int32)
    r = jnp.arange(p, dtype=jnp.int32)
    e_row = jnp.sum(r[:, None] >= end_p[None, :], axis=1).astype(jnp.int32)
    oh_r = (jnp.minimum(e_row, N_EXPERTS - 1)[:, None] == eids).astype(jnp.int32)
    j = r - jnp.sum(oh_r * start_p[None, :], axis=1)
    valid = (e_row < N_EXPERTS) & (j < jnp.sum(oh_r * counts[None, :], axis=1))
    src = jnp.clip(jnp.sum(oh_r * start[None, :], axis=1) + j, 0, n_sel - 1)
    flat_idx = _take(order, src)
    row_token = jnp.where(valid, flat_idx // TOP_K, 0)
    row_gate = jnp.where(valid, _take(flat_g, flat_idx), 0.0)
    tile_start = jnp.arange(p // tm, dtype=jnp.int32) * tm
    tile_expert = jnp.sum(tile_start[:, None] >= end_p[None, :], axis=1).astype(jnp.int32)
    tile_valid = (tile_expert < N_EXPERTS).astype(jnp.int32)
    tile_expert = jnp.minimum(tile_expert, N_EXPERTS - 1)
    return row_token, row_gate[:, None], tile_expert, tile_valid, pos


def kernel(x, c, ctx, c_ctx, ada_w, ada_b, ln_g, ln_b, mix_in_w, pool_w, pool_scale, q_norm, q_up, kv_norm, kv_up, mix_out_w, ffn_gate, ffn_up, ffn_down, hy_in_w, hy_conv_w, hy_conv_b, hy_fw1, hy_fb1, hy_fw2, hy_fb2, hy_fw3, hy_fb3, hy_fout, hy_freq, hy_skip, hy_out_w, router_w, moe_gate, moe_up, moe_down):
    b, n, d = x.shape
    t = b * n
    assert b + 1 <= 8
    s_rows = jnp.concatenate([c, c_ctx[None, :], jnp.zeros((8 - b - 1, d), F32)], axis=0)
    mod = _ada(s_rows, ada_w, ada_b)

    def chunks(l, rows):
        m = mod[l, rows].reshape(-1, 6, d)
        return [m[:, k][:, None, :] for k in range(6)]

    sh1, sc1, g1, sh2, sc2, g2 = chunks(0, slice(0, b))
    shc, scc = chunks(0, slice(b, b + 1))[:2]
    pool_u, q, k, v = _l0a(x, ctx, sc1, sh1, scc, shc, mix_in_w[0], q_norm[0], q_up[0], kv_norm[0], kv_up[0])
    attn_o = _attention(q, k, v, n)
    h1, z1 = _l0b(pool_u, attn_o, x, g1, sc2, sh2, pool_w[0], pool_scale[0], mix_out_w[0], ln_g[0, 0], ln_b[0, 0])
    f0 = _ffn_dense(z1.reshape(t, d), ffn_gate.astype(BF16), ffn_up.astype(BF16), ffn_down.astype(BF16), 1024, 1408)
    sh1, sc1, g1b, sh2b, sc2b, g2b = chunks(1, slice(0, b))
    h2, u2 = _resid_ln(h1, [f0.reshape(b, n, d)], g2, ln_g[0, 1], ln_b[0, 1], mod=(sc1, sh1))

    x0, vx = _l1a(u2, hy_in_w[0], hy_conv_w[0], hy_conv_b[0])
    kfull = _hyena_filters(n, hy_fw1[0], hy_fb1[0], hy_fw2[0], hy_fb2[0], hy_fw3[0], hy_fb3[0],
                           hy_fout[0], hy_freq[0])
    yl = _hyena_long_conv(x0, vx, kfull, hy_skip[0])
    h3, z3, logits = _l1c(yl, h2, g1b, sc2b, sh2b, hy_out_w[0], ln_g[1, 0], ln_b[1, 0], router_w[0])

    tm_moe = 512
    row_token, row_gate, tile_expert, tile_valid, pos = _route(logits.reshape(t, LANE)[:, :N_EXPERTS], tm_moe)
    ys = _moe_grouped(z3.reshape(t, d // LANE, LANE), row_token, _cast_bf16(moe_gate[0]), _cast_bf16(moe_up[0]),
                      _cast_bf16(moe_down[0]), row_gate, tile_expert, tile_valid, tm_moe, 1792)
    y_a = _take(ys, pos[:, 0]).reshape(b, n, d)
    y_b = _take(ys, pos[:, 1]).reshape(b, n, d)
    (h4,) = _resid_ln(h3, [y_a, y_b], g2b, ln_g[1, 1], ln_b[1, 1])
    return h4
```

```python
import functools
import math

import numpy as np
import jax
import jax.numpy as jnp
from jax import lax
from jax.experimental import pallas as pl
from jax.experimental.pallas import tpu as pltpu

F32 = jnp.float32
BF16 = jnp.bfloat16
HIGHEST = lax.Precision.HIGHEST

D_MODEL = 1024
GRID_W = 64
POOL_W = 512
POOL_WINDOWS = (2, 4, 8, 16)
POOL_GC = 128
MLA_HEADS = 8
QK_NOPE = 64
QK_ROPE = 32
V_HEAD = 64
Q_LORA = 256
KV_LORA = 256
ROPE_AXIS = 16
ROPE_BASE = 10000.0
N_EXPERTS = 8
TOP_K = 2
HY_BANDS = 16
HY_EMB = 1 + 2 * HY_BANDS
HY_FAST_DECAY = 0.3
HY_SLOW_DECAY = 1.5
HY_TARGET = 1e-2
HY_MIN_DECAY = math.log(HY_TARGET) / HY_SLOW_DECAY
HY_MAX_DECAY = math.log(HY_TARGET) / HY_FAST_DECAY
LN_EPS = 1e-5
RMS_EPS = 1e-6
DEPTH = 2
ALPHA = (2.0 * DEPTH) ** 0.25

LANE = 128
HEAD_SLOT = 128
DFT_N = 128
DFT_HALF = DFT_N // 2 + 1
VMEM_LIMIT = 56 * 1024 * 1024
ATTN_LOOKAHEAD = 2


def _cparams(*sem):
    return pltpu.CompilerParams(dimension_semantics=sem, vmem_limit_bytes=VMEM_LIMIT)


def _layer_norm(v, g, b):
    mu = jnp.mean(v, axis=-1, keepdims=True)
    c = v - mu
    var = jnp.mean(c * c, axis=-1, keepdims=True)
    return c * lax.rsqrt(var + LN_EPS) * g + b


def _rms_norm(v, g):
    return v * lax.rsqrt(jnp.mean(v * v, axis=-1, keepdims=True) + RMS_EPS) * g


def _ada_kernel(s_ref, w_ref, b_ref, o_ref):
    s = s_ref[...]
    s = s * jax.nn.sigmoid(s)
    o_ref[0] = jnp.dot(s, w_ref[0], preferred_element_type=F32, precision=HIGHEST) + b_ref[0]


def _ada(s_rows, ada_w, ada_b):
    depth, d, n6 = ada_w.shape
    tn = 768
    return pl.pallas_call(
        _ada_kernel,
        out_shape=jax.ShapeDtypeStruct((depth, 8, n6), F32),
        grid=(depth, n6 // tn),
        in_specs=[
            pl.BlockSpec((8, d), lambda l, j: (0, 0)),
            pl.BlockSpec((1, d, tn), lambda l, j: (l, 0, j)),
            pl.BlockSpec((1, 1, tn), lambda l, j: (l, 0, j)),
        ],
        out_specs=pl.BlockSpec((1, 8, tn), lambda l, j: (l, 0, j)),
        compiler_params=_cparams("parallel", "parallel"),
        name="ada_mod",
    )(s_rows, ada_w, ada_b.reshape(depth, 1, n6))


def _l0a_kernel(x_ref, c_ref, sc_ref, sh_ref, scc_ref, shc_ref, win_ref, qn_ref, kvn_ref,
                wqa_ref, wqb_ref, wk_ref, wv_ref, cos_ref, sin_ref, cost_ref, sint_ref,
                pu_ref, q_ref, k_ref, v_ref, u_scr, *, n_lat_tiles):
    i = pl.program_id(1)

    @pl.when(i < n_lat_tiles)
    def _():
        u_scr[...] = (x_ref[0] * (1.0 + sc_ref[0]) + sh_ref[0]).astype(BF16)

    @pl.when(i >= n_lat_tiles)
    def _():
        u_scr[...] = (c_ref[0] * (1.0 + scc_ref[0]) + shc_ref[0]).astype(BF16)

    proj = jnp.dot(u_scr[...], win_ref[...], preferred_element_type=F32)
    pu_ref[0] = proj[:, :POOL_W]
    cos = cos_ref[...]
    sin = sin_ref[...]
    q0 = POOL_W
    kv0 = POOL_W + Q_LORA
    r0 = kv0 + KV_LORA
    qn = _rms_norm(proj[:, q0:kv0], qn_ref[...]).astype(BF16)
    kvn = _rms_norm(proj[:, kv0:r0], kvn_ref[...]).astype(BF16)
    nt_dims = (((1,), (1,)), ((), ()))
    qa = lax.dot_general(wqa_ref[...], qn, nt_dims, preferred_element_type=F32)
    qb = lax.dot_general(wqb_ref[...], qn, nt_dims, preferred_element_type=F32)
    vt = lax.dot_general(wv_ref[...], kvn, nt_dims, preferred_element_type=F32)
    kn = jnp.dot(kvn, wk_ref[...], preferred_element_type=F32)
    krot = proj[:, r0:r0 + HEAD_SLOT] * cos + proj[:, r0 + HEAD_SLOT:r0 + 2 * HEAD_SLOT] * sin
    cos_t = cost_ref[...]
    sin_t = sint_ref[...]
    for h in range(MLA_HEADS):
        sl = slice(h * HEAD_SLOT, (h + 1) * HEAD_SLOT)
        q_ref[0, h] = (qa[sl] * cos_t + qb[sl] * sin_t).astype(BF16)
        k_ref[0, h] = (kn[:, sl] + krot).astype(BF16)
        v_ref[0, h] = vt[h * V_HEAD:(h + 1) * V_HEAD].astype(BF16)


def _rope_swap_index():
    half = ROPE_AXIS // 2
    idx = []
    for a in range(2):
        base = a * ROPE_AXIS
        idx += list(range(base + half, base + ROPE_AXIS)) + list(range(base, base + half))
    return np.array(idx)


def _rope_tables(n, n_ctx):
    rows = n // GRID_W
    r = jnp.repeat(jnp.arange(rows, dtype=F32), GRID_W)
    col = jnp.tile(jnp.arange(GRID_W, dtype=F32), rows)
    inv = ROPE_BASE ** (-jnp.arange(0, ROPE_AXIS, 2, dtype=F32) / ROPE_AXIS)
    ang_r = r[:, None] * inv
    ang_c = col[:, None] * inv
    cos32 = jnp.concatenate([jnp.cos(ang_r), jnp.cos(ang_r), jnp.cos(ang_c), jnp.cos(ang_c)], axis=-1)
    sin32 = jnp.concatenate([-jnp.sin(ang_r), jnp.sin(ang_r), -jnp.sin(ang_c), jnp.sin(ang_c)], axis=-1)
    pad = HEAD_SLOT - QK_NOPE - QK_ROPE
    cos = jnp.concatenate([jnp.ones((n, QK_NOPE), F32), cos32, jnp.ones((n, pad), F32)], axis=-1)
    sin = jnp.concatenate([jnp.zeros((n, QK_NOPE), F32), sin32, jnp.zeros((n, pad), F32)], axis=-1)
    cos = jnp.concatenate([cos, jnp.ones((n_ctx, HEAD_SLOT), F32)], axis=0)
    sin = jnp.concatenate([sin, jnp.zeros((n_ctx, HEAD_SLOT), F32)], axis=0)
    return cos, sin


def _l0a_weights(in_w, q_up, kv_up):
    swap = _rope_swap_index()
    d = in_w.shape[0]
    r0 = POOL_W + Q_LORA + KV_LORA
    w_rope = in_w[:, r0:]
    pad_l = jnp.zeros((d, QK_NOPE), F32)
    pad_r = jnp.zeros((d, HEAD_SLOT - QK_NOPE - QK_ROPE), F32)
    kr_a = jnp.concatenate([pad_l, w_rope, pad_r], axis=1)
    kr_b = jnp.concatenate([pad_l, w_rope[:, swap], pad_r], axis=1)
    w_in = jnp.concatenate([in_w[:, :r0], kr_a, kr_b], axis=1).astype(BF16)

    scale = (QK_NOPE + QK_ROPE) ** -0.5 * math.log2(math.e)
    qu = q_up.reshape(Q_LORA, MLA_HEADS, QK_NOPE + QK_ROPE) * scale
    zpad = jnp.zeros((Q_LORA, MLA_HEADS, HEAD_SLOT - QK_NOPE - QK_ROPE), F32)
    wq_a = jnp.concatenate([qu, zpad], axis=-1).reshape(Q_LORA, MLA_HEADS * HEAD_SLOT).T.astype(BF16)
    wq_b = jnp.concatenate([jnp.zeros((Q_LORA, MLA_HEADS, QK_NOPE), F32), qu[..., QK_NOPE:][..., swap], zpad],
                           axis=-1).reshape(Q_LORA, MLA_HEADS * HEAD_SLOT).T.astype(BF16)
    kvu = kv_up.reshape(KV_LORA, MLA_HEADS, QK_NOPE + V_HEAD)
    wk = jnp.concatenate([kvu[..., :QK_NOPE], jnp.zeros((KV_LORA, MLA_HEADS, HEAD_SLOT - QK_NOPE), F32)],
                         axis=-1).reshape(KV_LORA, MLA_HEADS * HEAD_SLOT).astype(BF16)
    wv_t = kvu[..., QK_NOPE:].reshape(KV_LORA, MLA_HEADS * V_HEAD).T.astype(BF16)
    return w_in, wq_a, wq_b, wk, wv_t


def _l0a(x, ctx, sc, sh, scc, shc, in_w, q_norm, q_up, kv_norm, kv_up):
    b, n, d = x.shape
    n_ctx = ctx.shape[1]
    tm = 256
    nl = n // tm
    nt = (n + n_ctx) // tm
    w_in, wq_a, wq_b, wk, wv_t = _l0a_weights(in_w, q_up, kv_up)
    cos, sin = _rope_tables(n, n_ctx)
    hw = MLA_HEADS * HEAD_SLOT
    full = lambda shape: pl.BlockSpec(shape, lambda bi, i: (0,) * len(shape))
    vec = pl.BlockSpec((1, 1, d), lambda bi, i: (bi, 0, 0))
    return pl.pallas_call(
        functools.partial(_l0a_kernel, n_lat_tiles=nl),
        out_shape=[
            jax.ShapeDtypeStruct((b, n + n_ctx, POOL_W), F32),
            jax.ShapeDtypeStruct((b, MLA_HEADS, HEAD_SLOT, n + n_ctx), BF16),
            jax.ShapeDtypeStruct((b, MLA_HEADS, n + n_ctx, HEAD_SLOT), BF16),
            jax.ShapeDtypeStruct((b, MLA_HEADS, V_HEAD, n + n_ctx), BF16),
        ],
        grid=(b, nt),
        in_specs=[
            pl.BlockSpec((1, tm, d), lambda bi, i: (bi, jnp.minimum(i, nl - 1), 0)),
            pl.BlockSpec((1, tm, d), lambda bi, i: (bi, jnp.maximum(i - nl, 0), 0)),
            vec, vec, full((1, 1, d)), full((1, 1, d)),
            full(w_in.shape), full((1, Q_LORA)), full((1, KV_LORA)),
            full((hw, Q_LORA)), full((hw, Q_LORA)), full((KV_LORA, hw)), full((MLA_HEADS * V_HEAD, KV_LORA)),
            pl.BlockSpec((tm, HEAD_SLOT), lambda bi, i: (i, 0)),
            pl.BlockSpec((tm, HEAD_SLOT), lambda bi, i: (i, 0)),
            pl.BlockSpec((HEAD_SLOT, tm), lambda bi, i: (0, i)),
            pl.BlockSpec((HEAD_SLOT, tm), lambda bi, i: (0, i)),
        ],
        out_specs=[
            pl.BlockSpec((1, tm, POOL_W), lambda bi, i: (bi, i, 0)),
            pl.BlockSpec((1, MLA_HEADS, HEAD_SLOT, tm), lambda bi, i: (bi, 0, 0, i)),
            pl.BlockSpec((1, MLA_HEADS, tm, HEAD_SLOT), lambda bi, i: (bi, 0, i, 0)),
            pl.BlockSpec((1, MLA_HEADS, V_HEAD, tm), lambda bi, i: (bi, 0, 0, i)),
        ],
        scratch_shapes=[pltpu.VMEM((tm, d), BF16)],
        compiler_params=_cparams("parallel", "arbitrary"),
        name="l0_in_proj",
    )(x, ctx, sc, sh, scc, shc, w_in, q_norm.reshape(1, -1), kv_norm.reshape(1, -1),
      wq_a, wq_b, wk, wv_t, cos, sin, cos.T, sin.T)


def _attn_kernel(q_ref, k_ref, v_ref, o_ref, m_ref, l_ref, acc_ref, *, nk):
    ki = pl.program_id(2)

    @pl.when(ki == 0)
    def _():
        m_ref[...] = jnp.full(m_ref.shape, -jnp.inf, F32)
        l_ref[...] = jnp.zeros(l_ref.shape, F32)
        acc_ref[...] = jnp.zeros(acc_ref.shape, F32)

    def scores(h):
        return jnp.dot(k_ref[0, h], q_ref[0, h], preferred_element_type=F32)

    pending = [scores(h) for h in range(ATTN_LOOKAHEAD)]
    for h in range(MLA_HEADS):
        if h + ATTN_LOOKAHEAD < MLA_HEADS:
            pending.append(scores(h + ATTN_LOOKAHEAD))
        st = pending.pop(0)
        m_prev = m_ref[h]
        m_new = jnp.maximum(m_prev, jnp.max(st, axis=0, keepdims=True))
        a = jnp.exp2(m_prev - m_new)
        p = jnp.exp2(st - m_new)
        l_ref[h] = a * l_ref[h] + jnp.sum(p, axis=0, keepdims=True)
        pv = jnp.dot(v_ref[0, h], p.astype(BF16), preferred_element_type=F32)
        rows = slice(h * V_HEAD, (h + 1) * V_HEAD)
        acc_ref[rows, :] = a * acc_ref[rows, :] + pv
        m_ref[h] = m_new

    @pl.when(ki == nk - 1)
    def _():
        for h in range(MLA_HEADS):
            rows = slice(h * V_HEAD, (h + 1) * V_HEAD)
            acc_ref[rows, :] = acc_ref[rows, :] / l_ref[h]
        o_ref[0] = acc_ref[...].T.astype(o_ref.dtype)


def _attention(q, k, vt, n):
    b, h, n_all, _ = k.shape
    tq, tk = 256, 4224
    nk = n_all // tk
    return pl.pallas_call(
        functools.partial(_attn_kernel, nk=nk),
        out_shape=jax.ShapeDtypeStruct((b, n, h * V_HEAD), BF16),
        grid=(b, n // tq, nk),
        in_specs=[
            pl.BlockSpec((1, h, HEAD_SLOT, tq), lambda bi, qi, ki: (bi, 0, 0, qi)),
            pl.BlockSpec((1, h, tk, HEAD_SLOT), lambda bi, qi, ki: (bi, 0, ki, 0)),
            pl.BlockSpec((1, h, V_HEAD, tk), lambda bi, qi, ki: (bi, 0, 0, ki)),
        ],
        out_specs=pl.BlockSpec((1, tq, h * V_HEAD), lambda bi, qi, ki: (bi, qi, 0)),
        scratch_shapes=[
            pltpu.VMEM((h, 1, tq), F32),
            pltpu.VMEM((h, 1, tq), F32),
            pltpu.VMEM((h * V_HEAD, tq), F32),
        ],
        compiler_params=_cparams("parallel", "parallel", "arbitrary"),
        name="mla_attention",
    )(q, k, vt)


def _l0b_kernel(pp_ref, pc_ref, pn_ref, o_ref, x_ref, g1_ref, sc2_ref, sh2_ref, pw_ref, ps_ref, ow_ref,
                lg_ref, lb_ref, h_ref, z_ref, *, tm, n):
    i = pl.program_id(1)
    halo = 8
    ext = jnp.concatenate([pp_ref[0], pc_ref[0], pn_ref[0]], axis=0)
    pos = i * tm - halo + lax.broadcasted_iota(jnp.int32, (tm + 2 * halo, 1), 0)
    ext = jnp.where((pos >= 0) & (pos < n), ext, 0.0)
    t = i * tm + lax.broadcasted_iota(jnp.int32, (tm, 1), 0)
    y = jnp.dot(o_ref[0], ow_ref[POOL_W:, :], preferred_element_type=F32)
    for g, w in enumerate(POOL_WINDOWS):
        hw = w // 2
        e = ext[:, g * POOL_GC:(g + 1) * POOL_GC]
        s = e
        width = 1
        while width < w:
            s = s[:s.shape[0] - width] + s[width:]
            width *= 2
        win = s[halo - hw:halo - hw + tm]
        cnt = (jnp.minimum(t + hw, n) - jnp.maximum(t - hw, 0)).astype(F32)
        dd = (win / cnt - e[halo:halo + tm]).astype(BF16)
        yg = jnp.dot(dd, pw_ref[g], preferred_element_type=F32) * ps_ref[:, g * POOL_GC:(g + 1) * POOL_GC]
        y = y + jnp.dot(yg.astype(BF16), ow_ref[g * POOL_GC:(g + 1) * POOL_GC, :], preferred_element_type=F32)
    hh = _layer_norm(ALPHA * x_ref[0] + g1_ref[0] * y, lg_ref[...], lb_ref[...])
    h_ref[0] = hh
    z_ref[0] = (hh * (1.0 + sc2_ref[0]) + sh2_ref[0]).astype(BF16)


def _l0b(pool_u, attn_o, x, g1, sc2, sh2, pool_w, pool_scale, out_w, ln_g, ln_b):
    b, n, d = x.shape
    tm = 512
    hb = tm // 8
    vec = pl.BlockSpec((1, 1, d), lambda bi, i: (bi, 0, 0))
    full = lambda shape: pl.BlockSpec(shape, lambda bi, i: (0,) * len(shape))
    return pl.pallas_call(
        functools.partial(_l0b_kernel, tm=tm, n=n),
        out_shape=[jax.ShapeDtypeStruct((b, n, d), F32), jax.ShapeDtypeStruct((b, n, d), BF16)],
        grid=(b, n // tm),
        in_specs=[
            pl.BlockSpec((1, 8, POOL_W), lambda bi, i: (bi, jnp.maximum(i * hb - 1, 0), 0)),
            pl.BlockSpec((1, tm, POOL_W), lambda bi, i: (bi, i, 0)),
            pl.BlockSpec((1, 8, POOL_W), lambda bi, i: (bi, (i + 1) * hb, 0)),
            pl.BlockSpec((1, tm, POOL_W), lambda bi, i: (bi, i, 0)),
            pl.BlockSpec((1, tm, d), lambda bi, i: (bi, i, 0)),
            vec, vec, vec,
            full(pool_w.shape), full((1, POOL_W)), full(out_w.shape), full((1, d)), full((1, d)),
        ],
        out_specs=[pl.BlockSpec((1, tm, d), lambda bi, i: (bi, i, 0)),
                   pl.BlockSpec((1, tm, d), lambda bi, i: (bi, i, 0))],
        compiler_params=_cparams("parallel", "parallel"),
        name="l0_pool_out_ln",
    )(pool_u, pool_u, pool_u, attn_o, x, g1, sc2, sh2, pool_w.astype(BF16), pool_scale.reshape(1, -1),
      out_w.astype(BF16), ln_g.reshape(1, -1), ln_b.reshape(1, -1))


def _swiglu_step(z, wg_ref, wu_ref, wd_ref, acc_ref):
    g = jnp.dot(z, wg_ref[0].astype(BF16), preferred_element_type=F32)
    u = jnp.dot(z, wu_ref[0].astype(BF16), preferred_element_type=F32)
    a = (g * jax.nn.sigmoid(g) * u).astype(BF16)
    acc_ref[...] += jnp.dot(a, wd_ref[0].astype(BF16), preferred_element_type=F32)


def _ffn_kernel(z_ref, wg_ref, wu_ref, wd_ref, o_ref, acc_ref, *, nj):
    j = pl.program_id(1)

    @pl.when(j == 0)
    def _():
        acc_ref[...] = jnp.zeros(acc_ref.shape, F32)

    _swiglu_step(z_ref[...], wg_ref, wu_ref, wd_ref, acc_ref)

    @pl.when(j == nj - 1)
    def _():
        o_ref[...] = acc_ref[...]


def _ffn_dense(z, wg, wu, wd, tm, tf):
    t, d = z.shape
    f = wg.shape[-1]
    nj = f // tf
    return pl.pallas_call(
        functools.partial(_ffn_kernel, nj=nj),
        out_shape=jax.ShapeDtypeStruct((t, d), F32),
        grid=(t // tm, nj),
        in_specs=[
            pl.BlockSpec((tm, d), lambda i, j: (i, 0)),
            pl.BlockSpec((1, d, tf), lambda i, j: (0, 0, j)),
            pl.BlockSpec((1, d, tf), lambda i, j: (0, 0, j)),
            pl.BlockSpec((1, tf, d), lambda i, j: (0, j, 0)),
        ],
        out_specs=pl.BlockSpec((tm, d), lambda i, j: (i, 0)),
        scratch_shapes=[pltpu.VMEM((tm, d), F32)],
        compiler_params=_cparams("parallel", "arbitrary"),
        name="swiglu_dense",
    )(z, wg, wu, wd)


def _cast_kernel(x_ref, o_ref):
    o_ref[...] = x_ref[...].astype(o_ref.dtype)


def _cast_bf16(w):
    e, r, c = w.shape
    tr = 512
    blk = pl.BlockSpec((1, tr, c), lambda ei, ri: (ei, ri, 0))
    return pl.pallas_call(
        _cast_kernel,
        out_shape=jax.ShapeDtypeStruct(w.shape, BF16),
        grid=(e, r // tr),
        in_specs=[blk],
        out_specs=blk,
        compiler_params=_cparams("parallel", "parallel"),
        name="cast_bf16",
    )(w)


MOE_LOOKAHEAD = 2


def _moe_kernel(te_ref, tv_ref, *refs, nj, tm, n_tiles):
    tok_first = refs[:MOE_LOOKAHEAD]
    tokn_ref, z_hbm, wg_ref, wu_ref, wd_ref, gate_ref, o_ref, zbuf, zb16, sem, acc_ref = refs[MOE_LOOKAHEAD:]
    i = pl.program_id(0)
    j = pl.program_id(1)
    slots = zbuf.shape[0]
    slot = lax.rem(i, slots)
    nxt = lax.rem(i + MOE_LOOKAHEAD, slots)
    segs = z_hbm.shape[1]
    per_step = zbuf.shape[1] // (segs * nj)

    def row_copy(tok_ref, row, s, priority=0):
        tok = tok_ref[0, 0, jnp.minimum(row, tm - 1)]
        pltpu.async_copy(z_hbm.at[tok], zbuf.at[s, pl.ds(row * segs, segs), :], sem.at[s], priority=priority)

    def wait_slot(s):
        pltpu.make_async_copy(zbuf.at[s], zbuf.at[s], sem.at[s]).wait()

    def issue_next():
        for r in range(per_step):
            row_copy(tokn_ref, j * per_step + r, nxt, priority=1)

    @pl.when((i == 0) & (j == 0))
    def _():
        for k, tok_ref in enumerate(tok_first):
            def body(r, carry):
                row_copy(tok_ref, r, k)
                return carry
            lax.fori_loop(0, per_step * nj, body, 0)

    @pl.when(j == 0)
    def _():
        wait_slot(slot)
        for g in range(segs):
            zb16[:, g * LANE:(g + 1) * LANE] = zbuf.at[slot][pl.ds(g, tm, stride=segs), :].astype(BF16)
        acc_ref[...] = jnp.zeros(acc_ref.shape, F32)

    @pl.when(tv_ref[i] > 0)
    def _():
        issue_next()
        _swiglu_step(zb16[...], wg_ref, wu_ref, wd_ref, acc_ref)

    @pl.when(tv_ref[i] == 0)
    def _():
        issue_next()

    @pl.when(j == nj - 1)
    def _():
        o_ref[...] = acc_ref[...] * gate_ref[...]

    @pl.when((i == n_tiles - 1) & (j == nj - 1))
    def _():
        for k in range(1, MOE_LOOKAHEAD + 1):
            wait_slot(lax.rem(i + k, slots))


def _moe_grouped(z, row_token, wg, wu, wd, gate, tile_expert, tile_valid, tm, tf):
    t, segs, _ = z.shape
    d = segs * LANE
    p = row_token.shape[0]
    n_tiles = p // tm
    f = wg.shape[-1]
    nj = f // tf
    tok = row_token.reshape(n_tiles, 1, tm)
    buf_rows = nj * (-(-tm // (8 * nj)) * 8)
    smem_blk = lambda imap: pl.BlockSpec((1, 1, tm), imap, memory_space=pltpu.SMEM)
    return pl.pallas_call(
        functools.partial(_moe_kernel, nj=nj, tm=tm, n_tiles=n_tiles),
        out_shape=jax.ShapeDtypeStruct((p, d), F32),
        grid_spec=pltpu.PrefetchScalarGridSpec(
            num_scalar_prefetch=2,
            grid=(n_tiles, nj),
            in_specs=[
                *[smem_blk(functools.partial(lambda i, j, te, tv, k: (k, 0, 0), k=k)) for k in range(MOE_LOOKAHEAD)],
                smem_blk(lambda i, j, te, tv: (jnp.minimum(i + MOE_LOOKAHEAD, n_tiles - 1), 0, 0)),
                pl.BlockSpec(memory_space=pl.ANY),
                pl.BlockSpec((1, d, tf), lambda i, j, te, tv: (te[i], 0, j)),
                pl.BlockSpec((1, d, tf), lambda i, j, te, tv: (te[i], 0, j)),
                pl.BlockSpec((1, tf, d), lambda i, j, te, tv: (te[i], j, 0)),
                pl.BlockSpec((tm, 1), lambda i, j, te, tv: (i, 0)),
            ],
            out_specs=pl.BlockSpec((tm, d), lambda i, j, te, tv: (i, 0)),
            scratch_shapes=[
                pltpu.VMEM((MOE_LOOKAHEAD + 1, buf_rows * segs, LANE), F32),
                pltpu.VMEM((tm, d), BF16),
                pltpu.SemaphoreType.DMA((MOE_LOOKAHEAD + 1,)),
                pltpu.VMEM((tm, d), F32),
            ],
        ),
        compiler_params=_cparams("arbitrary", "arbitrary"),
        name="moe_swiglu_gather",
    )(tile_expert, tile_valid, *([tok] * (MOE_LOOKAHEAD + 1)), z, wg, wu, wd, gate)


def _resid_ln_kernel(*refs, n_y, with_mod):
    h_ref = refs[0]
    y_refs = refs[1:1 + n_y]
    g_ref, lg_ref, lb_ref = refs[1 + n_y:4 + n_y]
    rest = refs[4 + n_y:]
    y = y_refs[0][0]
    for r in y_refs[1:]:
        y = y + r[0]
    hh = _layer_norm(ALPHA * h_ref[0] + g_ref[0] * y, lg_ref[...], lb_ref[...])
    if with_mod:
        sc_ref, sh_ref, ho_ref, u_ref = rest
        ho_ref[0] = hh
        u_ref[0] = (hh * (1.0 + sc_ref[0]) + sh_ref[0]).astype(BF16)
    else:
        rest[0][0] = hh


def _resid_ln(h, ys, g, ln_g, ln_b, mod=None):
    b, n, d = h.shape
    tm = 1024
    blk = pl.BlockSpec((1, tm, d), lambda bi, i: (bi, i, 0))
    vec = pl.BlockSpec((1, 1, d), lambda bi, i: (bi, 0, 0))
    row = pl.BlockSpec((1, d), lambda bi, i: (0, 0))
    in_specs = [blk] + [blk] * len(ys) + [vec, row, row]
    args = [h, *ys, g, ln_g.reshape(1, -1), ln_b.reshape(1, -1)]
    out_shape = [jax.ShapeDtypeStruct((b, n, d), F32)]
    out_specs = [blk]
    if mod is not None:
        in_specs += [vec, vec]
        args += list(mod)
        out_shape.append(jax.ShapeDtypeStruct((b, n, d), BF16))
        out_specs.append(blk)
    return pl.pallas_call(
        functools.partial(_resid_ln_kernel, n_y=len(ys), with_mod=mod is not None),
        out_shape=out_shape,
        grid=(b, n // tm),
        in_specs=in_specs,
        out_specs=out_specs,
        compiler_params=_cparams("parallel", "parallel"),
        name="resid_ln",
    )(*args)


def _l1a_kernel(up_ref, uc_ref, un_ref, w_ref, cw_ref, cb_ref, x0_ref, vx_ref, *, tm, n):
    i = pl.program_id(1)
    halo = 16
    ext = jnp.concatenate([up_ref[0], uc_ref[0], un_ref[0]], axis=0)
    pos = i * tm - halo + lax.broadcasted_iota(jnp.int32, (tm + 2 * halo, 1), 0)
    ext = jnp.where((pos >= 0) & (pos < n), ext, jnp.zeros_like(ext))
    z = jnp.dot(ext, w_ref[...], preferred_element_type=F32)
    cw = cw_ref[...]
    zc = (cw[0:1] * z[halo - 1:halo - 1 + tm] + cw[1:2] * z[halo:halo + tm]
          + cw[2:3] * z[halo + 1:halo + 1 + tm] + cb_ref[...])
    c = zc.shape[1] // 3
    x0_ref[0] = zc[:, :c].astype(x0_ref.dtype)
    vx_ref[0] = (zc[:, 2 * c:] * zc[:, c:2 * c]).astype(vx_ref.dtype)


def _l1a(u, hy_in_w, conv_w, conv_b):
    b, n, d = u.shape
    c3 = hy_in_w.shape[1]
    c = c3 // 3
    tm = 1024
    hb = tm // 16
    nb16 = n // 16
    full = lambda shape: pl.BlockSpec(shape, lambda bi, i: (0,) * len(shape))
    return pl.pallas_call(
        functools.partial(_l1a_kernel, tm=tm, n=n),
        out_shape=[jax.ShapeDtypeStruct((b, n, c), BF16), jax.ShapeDtypeStruct((b, n, c), BF16)],
        grid=(b, n // tm),
        in_specs=[
            pl.BlockSpec((1, 16, d), lambda bi, i: (bi, jnp.maximum(i * hb - 1, 0), 0)),
            pl.BlockSpec((1, tm, d), lambda bi, i: (bi, i, 0)),
            pl.BlockSpec((1, 16, d), lambda bi, i: (bi, jnp.minimum((i + 1) * hb, nb16 - 1), 0)),
            full((d, c3)), full((3, c3)), full((1, c3)),
        ],
        out_specs=[pl.BlockSpec((1, tm, c), lambda bi, i: (bi, i, 0)),
                   pl.BlockSpec((1, tm, c), lambda bi, i: (bi, i, 0))],
        compiler_params=_cparams("parallel", "parallel"),
        name="hyena_in_conv",
    )(u, u, u, hy_in_w.astype(BF16), conv_w, conv_b.reshape(1, -1))


def _filter_kernel(zf_ref, zb_ref, w1f_ref, w1b_ref, b1_ref, w2_ref, b2_ref, w3_ref, b3_ref, wof_ref, wob_ref,
                   fr_ref, dl_ref, k_ref):
    zf = zf_ref[...]
    zb = zb_ref[...]
    fr = fr_ref[...]
    dot = functools.partial(jnp.dot, preferred_element_type=F32, precision=HIGHEST)
    h = jnp.sin(fr * (dot(zf, w1f_ref[...]) + dot(zb, w1b_ref[...]) + b1_ref[...]))
    h = jnp.sin(fr * (dot(h, w2_ref[...]) + b2_ref[...]))
    h = jnp.sin(fr * (dot(h, w3_ref[...]) + b3_ref[...]))
    dl = dl_ref[...]
    h_hi = h.astype(BF16)
    h_lo = (h - h_hi.astype(F32)).astype(BF16)

    def dot3(w_ref):
        bdot = functools.partial(jnp.dot, preferred_element_type=F32)
        return bdot(h_hi, w_ref[0]) + (bdot(h_hi, w_ref[1]) + bdot(h_lo, w_ref[0]))

    k_ref[0] = dot3(wof_ref) * jnp.exp(-zf[:, 0:1] * dl)
    ob = dot3(wob_ref) * jnp.exp(-zb[:, 0:1] * dl)
    first = (pl.program_id(0) == 0) & (lax.broadcasted_iota(jnp.int32, (ob.shape[0], 1), 0) == 0)
    k_ref[1] = jnp.where(first, 0.0, ob)


def _hyena_filters(n, fw1, fb1, fw2, fb2, fw3, fb3, fout, freq):
    c = fout.shape[1] // 2
    fh = fw2.shape[0]
    bands = jnp.linspace(1e-4, HY_BANDS - 1, HY_BANDS, dtype=F32)[None, :]

    def features(p):
        w_ang = (2.0 * math.pi / n) * p
        return jnp.concatenate([p / (n - 1.0), jnp.cos(bands * w_ang), -jnp.sin(bands * w_ang),
                                jnp.zeros((n, LANE - HY_EMB), F32)], axis=-1)

    m = jnp.arange(n, dtype=F32)[:, None]
    z_f, z_b = features(m), features(n - m)
    deltas = jnp.abs(jnp.linspace(HY_MIN_DECAY, HY_MAX_DECAY, c, dtype=F32))[None, :]
    zeros = lambda r, cc: jnp.zeros((r, cc), F32)
    two = lambda a: jnp.concatenate([a.reshape(1, -1), a.reshape(1, -1)], axis=1)
    diag2 = lambda w: jnp.concatenate([jnp.concatenate([w, zeros(fh, fh)], axis=1),
                                       jnp.concatenate([zeros(fh, fh), w], axis=1)], axis=0)
    w1 = jnp.pad(fw1, ((0, LANE - fw1.shape[0]), (0, 0)))
    w1f = jnp.concatenate([w1, zeros(LANE, fh)], axis=1)
    w1b = jnp.concatenate([zeros(LANE, fh), w1], axis=1)

    def split(w):
        hi = w.astype(BF16)
        return jnp.stack([hi, (w - hi.astype(F32)).astype(BF16)])

    wof = split(jnp.concatenate([fout[:, :c], zeros(fh, c)], axis=0))
    wob = split(jnp.concatenate([zeros(fh, c), fout[:, c:]], axis=0))
    tm = 1024
    full = lambda shape: pl.BlockSpec(shape, lambda i: (0,) * len(shape))
    rows = pl.BlockSpec((tm, LANE), lambda i: (i, 0))
    out = pl.pallas_call(
        _filter_kernel,
        out_shape=jax.ShapeDtypeStruct((2, n, c), F32),
        grid=(n // tm,),
        in_specs=[rows, rows, full((LANE, LANE)), full((LANE, LANE)), full((1, LANE)),
                  full((LANE, LANE)), full((1, LANE)), full((LANE, LANE)), full((1, LANE)),
                  full((2, LANE, c)), full((2, LANE, c)), full((1, LANE)), full((1, c))],
        out_specs=pl.BlockSpec((2, tm, c), lambda i: (0, i, 0)),
        compiler_params=_cparams("parallel"),
        name="hyena_filters",
    )(z_f, z_b, w1f, w1b, two(fb1), diag2(fw2), two(fb2), diag2(fw3), two(fb3), wof, wob, two(freq), deltas)
    return out.reshape(2 * n, c)


def _dft_consts():
    kk = np.arange(DFT_N)[:, None] * np.arange(DFT_N)[None, :]
    c, s = np.cos(2.0 * np.pi * kk / DFT_N), np.sin(2.0 * np.pi * kk / DFT_N)
    g_fwd = np.block([[c, s], [-s, c]])
    g_inv = np.block([[c, -s], [s, c]])
    return jnp.asarray(g_fwd, BF16), jnp.asarray(g_inv, BF16)


DFT_ROW_PAD = 72
CONV_UNROLL = 16
CONV_UNROLL_MID = 5
FFT_PITCH = 136


def _conv_tables(n1):
    length = DFT_N * DFT_N
    k1 = np.arange(DFT_HALF)[None, :, None]
    n = (DFT_N * np.arange(n1)[None, None, :] + np.arange(DFT_N)[:, None, None])
    th = 2.0 * np.pi * k1 * n / length
    pad = np.zeros((DFT_N, DFT_ROW_PAD - DFT_HALF, n1))
    fa = np.concatenate([np.cos(th), pad, -np.sin(th), pad], axis=1)
    wgt = np.full((1, 1, DFT_HALF), 2.0)
    wgt[..., 0] = wgt[..., -1] = 1.0
    tht = np.transpose(th, (0, 2, 1))
    padk = np.zeros((DFT_N, n1, DFT_ROW_PAD - DFT_HALF))
    fo_r = np.concatenate([wgt * np.cos(tht) / length, padk], axis=2)
    fo_i = np.concatenate([-wgt * np.sin(tht) / length, padk], axis=2)
    return (jnp.asarray(fa, BF16), jnp.asarray(fo_r, BF16), jnp.asarray(fo_i, BF16))


def _conv_fused_kernel(vx_ref, x0_ref, sk_ref, gf_ref, gi_ref, fa_hbm, for_hbm, foi_hbm, hr_hbm, hi_hbm, o_ref,
                       xs, ar, ai, fa_s, for_s, foi_s, hr_s, hi_s):
    ci = pl.program_id(0)
    bi = pl.program_id(1)
    lanes = vx_ref.shape[2]
    n1c = vx_ref.shape[1] // DFT_N

    @pl.when((ci == 0) & (bi == 0))
    def _():
        pltpu.sync_copy(fa_hbm, fa_s)
        pltpu.sync_copy(for_hbm, for_s)
        pltpu.sync_copy(foi_hbm, foi_s)

    @pl.when(bi == 0)
    def _():
        lane0 = pl.multiple_of(ci * lanes, lanes)
        pltpu.sync_copy(hr_hbm.at[:, :, pl.ds(lane0, lanes)], hr_s)
        pltpu.sync_copy(hi_hbm.at[:, :, pl.ds(lane0, lanes)], hi_s)

    for p in range(n1c):
        xs[pl.ds(p * FFT_PITCH, DFT_N), :] = vx_ref[0, pl.ds(p * DFT_N, DFT_N), :].astype(F32)

    def fwd(g, carry):
        n2s = [g * CONV_UNROLL + u for u in range(CONV_UNROLL)]
        xgs = [xs[pl.ds(n2, n1c, stride=FFT_PITCH), :].astype(BF16) for n2 in n2s]
        ys = [jnp.dot(fa_s[n2], xg, preferred_element_type=F32) for n2, xg in zip(n2s, xgs)]
        for n2, y in zip(n2s, ys):
            ar[pl.ds(n2, DFT_ROW_PAD, stride=FFT_PITCH), :] = y[:DFT_ROW_PAD]
            ai[pl.ds(n2, DFT_ROW_PAD, stride=FFT_PITCH), :] = y[DFT_ROW_PAD:]
        return carry
    lax.fori_loop(0, DFT_N // CONV_UNROLL, fwd, 0)

    def mid_one(a2, k1):
        x = jnp.dot(gf_ref[...], a2, preferred_element_type=F32)
        xr, xi = x[:DFT_N], x[DFT_N:]
        hr = hr_s[k1]
        hi = hi_s[k1]
        z = jnp.concatenate([xr * hr - xi * hi, xr * hi + xi * hr], axis=0).astype(BF16)
        return jnp.dot(gi_ref[...], z, preferred_element_type=F32)

    def mid(g, carry):
        k1s = [jnp.minimum(g * CONV_UNROLL_MID + u, DFT_HALF - 1) for u in range(CONV_UNROLL_MID)]
        bases = [pl.multiple_of(k1 * FFT_PITCH, 8) for k1 in k1s]
        a2s = [jnp.concatenate([ar[pl.ds(bs, DFT_N), :], ai[pl.ds(bs, DFT_N), :]], axis=0).astype(BF16)
               for bs in bases]
        ys = [mid_one(a2, k1) for a2, k1 in zip(a2s, k1s)]
        for bs, y in zip(bases, ys):
            ar[pl.ds(bs, DFT_N), :] = y[:DFT_N]
            ai[pl.ds(bs, DFT_N), :] = y[DFT_N:]
        return carry
    lax.fori_loop(0, -(-DFT_HALF // CONV_UNROLL_MID), mid, 0)

    def inv(g, carry):
        n2s = [g * CONV_UNROLL + u for u in range(CONV_UNROLL)]
        brs = [ar[pl.ds(n2, DFT_ROW_PAD, stride=FFT_PITCH), :].astype(BF16) for n2 in n2s]
        bis = [ai[pl.ds(n2, DFT_ROW_PAD, stride=FFT_PITCH), :].astype(BF16) for n2 in n2s]
        ys = [jnp.dot(for_s[n2], br, preferred_element_type=F32) + jnp.dot(foi_s[n2], bi_, preferred_element_type=F32)
              for n2, br, bi_ in zip(n2s, brs, bis)]
        for n2, y in zip(n2s, ys):
            xs[pl.ds(n2, n1c, stride=FFT_PITCH), :] = y
        return carry
    lax.fori_loop(0, DFT_N // CONV_UNROLL, inv, 0)

    sk = sk_ref[...]
    for p in range(n1c):
        rows = pl.ds(p * DFT_N, DFT_N)
        conv = xs[pl.ds(p * FFT_PITCH, DFT_N), :]
        o_ref[0, rows, :] = (x0_ref[0, rows, :].astype(F32)
                             * (conv + vx_ref[0, rows, :].astype(F32) * sk)).astype(o_ref.dtype)


def _conv_fused(x0, vx, hr, hi, skip, g_fwd, g_inv):
    b, n, c = vx.shape
    n1 = n // DFT_N
    lanes = LANE
    fa, fo_r, fo_i = _conv_tables(n1)
    blk = pl.BlockSpec((1, n, lanes), lambda ci, bi: (bi, 0, ci))
    full = lambda a: pl.BlockSpec(a.shape, lambda ci, bi: (0,) * a.ndim)
    hbm = pl.BlockSpec(memory_space=pl.ANY)
    return pl.pallas_call(
        _conv_fused_kernel,
        out_shape=jax.ShapeDtypeStruct((b, n, c), BF16),
        grid=(c // lanes, b),
        in_specs=[blk, blk, pl.BlockSpec((1, lanes), lambda ci, bi: (0, ci)), full(g_fwd), full(g_inv),
                  hbm, hbm, hbm, hbm, hbm],
        out_specs=blk,
        scratch_shapes=[
            pltpu.VMEM((n1 * FFT_PITCH, lanes), F32),
            pltpu.VMEM((DFT_ROW_PAD * FFT_PITCH, lanes), F32),
            pltpu.VMEM((DFT_ROW_PAD * FFT_PITCH, lanes), F32),
            pltpu.VMEM(fa.shape, BF16), pltpu.VMEM(fo_r.shape, BF16), pltpu.VMEM(fo_i.shape, BF16),
            pltpu.VMEM((DFT_HALF, DFT_N, lanes), F32), pltpu.VMEM((DFT_HALF, DFT_N, lanes), F32),
        ],
        compiler_params=_cparams("arbitrary", "arbitrary"),
        name="long_conv_fused",
    )(vx, x0, skip.reshape(1, c), g_fwd, g_inv, fa, fo_r, fo_i, hr, hi)


def _filter_spec_kernel(k_hbm, gf_ref, fa_hbm, hr_ref, hi_ref, ks, ar, ai, fa_s, sem):
    ci = pl.program_id(0)
    lanes = ks.shape[1]
    lane0 = pl.multiple_of(ci * lanes, lanes)

    @pl.when(ci == 0)
    def _():
        pltpu.sync_copy(fa_hbm, fa_s)

    def plane_copy(p):
        return pltpu.make_async_copy(k_hbm.at[pl.ds(p * DFT_N, DFT_N), pl.ds(lane0, lanes)],
                                     ks.at[pl.ds(p * FFT_PITCH, DFT_N), :], sem.at[0])
    for p in range(DFT_N):
        plane_copy(p).start()
    for p in range(DFT_N):
        plane_copy(p).wait()

    def fwd(g, carry):
        n2s = [g * CONV_UNROLL + u for u in range(CONV_UNROLL)]
        xgs = [ks[pl.ds(n2, DFT_N, stride=FFT_PITCH), :].astype(BF16) for n2 in n2s]
        ys = [jnp.dot(fa_s[n2], xg, preferred_element_type=F32) for n2, xg in zip(n2s, xgs)]
        for n2, y in zip(n2s, ys):
            ar[pl.ds(n2, DFT_ROW_PAD, stride=FFT_PITCH), :] = y[:DFT_ROW_PAD]
            ai[pl.ds(n2, DFT_ROW_PAD, stride=FFT_PITCH), :] = y[DFT_ROW_PAD:]
        return carry
    lax.fori_loop(0, DFT_N // CONV_UNROLL, fwd, 0)

    def spec(g, carry):
        k1s = [jnp.minimum(g * CONV_UNROLL_MID + u, DFT_HALF - 1) for u in range(CONV_UNROLL_MID)]
        bases = [pl.multiple_of(k1 * FFT_PITCH, 8) for k1 in k1s]
        a2s = [jnp.concatenate([ar[pl.ds(bs, DFT_N), :], ai[pl.ds(bs, DFT_N), :]], axis=0).astype(BF16)
               for bs in bases]
        xs_ = [jnp.dot(gf_ref[...], a2, preferred_element_type=F32) for a2 in a2s]
        for k1, x in zip(k1s, xs_):
            hr_ref[k1] = x[:DFT_N]
            hi_ref[k1] = x[DFT_N:]
        return carry
    lax.fori_loop(0, -(-DFT_HALF // CONV_UNROLL_MID), spec, 0)


def _filter_spectrum_fused(kfull, g_fwd):
    rows, c = kfull.shape
    lanes = LANE
    fa = _conv_tables(rows // DFT_N)[0]
    out = pl.BlockSpec((DFT_HALF, DFT_N, lanes), lambda ci: (0, 0, ci))
    hbm = pl.BlockSpec(memory_space=pl.ANY)
    return pl.pallas_call(
        _filter_spec_kernel,
        out_shape=[jax.ShapeDtypeStruct((DFT_HALF, DFT_N, c), F32)] * 2,
        grid=(c // lanes,),
        in_specs=[hbm, pl.BlockSpec(g_fwd.shape, lambda ci: (0, 0)), hbm],
        out_specs=[out, out],
        scratch_shapes=[
            pltpu.VMEM((DFT_N * FFT_PITCH, lanes), F32),
            pltpu.VMEM((DFT_ROW_PAD * FFT_PITCH, lanes), F32),
            pltpu.VMEM((DFT_ROW_PAD * FFT_PITCH, lanes), F32),
            pltpu.VMEM(fa.shape, BF16),
            pltpu.SemaphoreType.DMA((1,)),
        ],
        compiler_params=_cparams("arbitrary"),
        name="filter_spectrum_fused",
    )(kfull, g_fwd, fa)


def _hyena_long_conv(x0, vx, kfull, skip):
    g_fwd, g_inv = _dft_consts()
    hr, hi = _filter_spectrum_fused(kfull, g_fwd)
    return _conv_fused(x0, vx, hr, hi, skip, g_fwd, g_inv)


def _l1c_kernel(y_ref, h_ref, g1_ref, sc2_ref, sh2_ref, w_ref, lg_ref, lb_ref, rw_ref, ho_ref, z_ref, lo_ref):
    y = jnp.dot(y_ref[0], w_ref[...], preferred_element_type=F32)
    hh = _layer_norm(ALPHA * h_ref[0] + g1_ref[0] * y, lg_ref[...], lb_ref[...])
    ho_ref[0] = hh
    z = hh * (1.0 + sc2_ref[0]) + sh2_ref[0]
    segs = z.shape[1] // LANE
    for s in range(segs):
        z_ref[0, pl.ds(s, z.shape[0], stride=segs), :] = z[:, s * LANE:(s + 1) * LANE]
    z_hi = z.astype(BF16)
    z_lo = (z - z_hi.astype(F32)).astype(BF16)
    bdot = functools.partial(jnp.dot, preferred_element_type=F32)
    lo_ref[0] = bdot(z_hi, rw_ref[0]) + (bdot(z_hi, rw_ref[1]) + bdot(z_lo, rw_ref[0]))


def _l1c(y, h, g1, sc2, sh2, out_w, ln_g, ln_b, router_w):
    b, n, d = h.shape
    tm = 1024
    blk = pl.BlockSpec((1, tm, d), lambda bi, i: (bi, i, 0))
    vec = pl.BlockSpec((1, 1, d), lambda bi, i: (bi, 0, 0))
    full = lambda shape: pl.BlockSpec(shape, lambda bi, i: (0,) * len(shape))
    rw = jnp.pad(router_w, ((0, 0), (0, LANE - router_w.shape[1])))
    rw_hi = rw.astype(BF16)
    rw = jnp.stack([rw_hi, (rw - rw_hi.astype(F32)).astype(BF16)])
    segs = d // LANE
    return pl.pallas_call(
        _l1c_kernel,
        out_shape=[jax.ShapeDtypeStruct((b, n, d), F32), jax.ShapeDtypeStruct((b, n * segs, LANE), F32),
                   jax.ShapeDtypeStruct((b, n, LANE), F32)],
        grid=(b, n // tm),
        in_specs=[blk, blk, vec, vec, vec, full((d, d)), full((1, d)), full((1, d)), full((2, d, LANE))],
        out_specs=[blk, pl.BlockSpec((1, tm * segs, LANE), lambda bi, i: (bi, i, 0)),
                   pl.BlockSpec((1, tm, LANE), lambda bi, i: (bi, i, 0))],
        compiler_params=_cparams("parallel", "parallel"),
        name="hyena_out_ln_router",
    )(y, h, g1, sc2, sh2, out_w.astype(BF16), ln_g.reshape(1, -1), ln_b.reshape(1, -1), rw)


def _take(a, idx):
    return a.at[idx].get(mode="promise_in_bounds")


def _route(logits, tm):
    t = logits.shape[0]
    top_v, top_i = lax.top_k(logits, TOP_K)
    gates = jax.nn.softmax(top_v, axis=-1)
    flat_e = top_i.reshape(-1).astype(jnp.int32)
    flat_g = gates.reshape(-1)
    n_sel = t * TOP_K
    p = n_sel + N_EXPERTS * tm
    eids = jnp.arange(N_EXPERTS, dtype=jnp.int32)[None, :]
    onehot = (flat_e[:, None] == eids).astype(jnp.int32)
    csum = jnp.cumsum(onehot, axis=0)
    counts = csum[-1]
    rank = jnp.sum((csum - onehot) * onehot, axis=1)
    padded = ((counts + tm - 1) // tm) * tm
    end_p = jnp.cumsum(padded)
    start_p = end_p - padded
    start = jnp.cumsum(counts) - counts
    pos = (jnp.sum(onehot * start_p[None, :], axis=1) + rank).reshape(t, TOP_K)
    order = jnp.argsort(flat_e, stable=True).astype(jnp.int32)
    r = jnp.arange(p, dtype=jnp.int32)
    e_row = jnp.sum(r[:, None] >= end_p[None, :], axis=1).astype(jnp.int32)
    oh_r = (jnp.minimum(e_row, N_EXPERTS - 1)[:, None] == eids).astype(jnp.int32)
    j = r - jnp.sum(oh_r * start_p[None, :], axis=1)
    valid = (e_row < N_EXPERTS) & (j < jnp.sum(oh_r * counts[None, :], axis=1))
    src = jnp.clip(jnp.sum(oh_r * start[None, :], axis=1) + j, 0, n_sel - 1)
    flat_idx = _take(order, src)
    row_token = jnp.where(valid, flat_idx // TOP_K, 0)
    row_gate = jnp.where(valid, _take(flat_g, flat_idx), 0.0)
    tile_start = jnp.arange(p // tm, dtype=jnp.int32) * tm
    tile_expert = jnp.sum(tile_start[:, None] >= end_p[None, :], axis=1).astype(jnp.int32)
    tile_valid = (tile_expert < N_EXPERTS).astype(jnp.int32)
    tile_expert = jnp.minimum(tile_expert, N_EXPERTS - 1)
    return row_token, row_gate[:, None], tile_expert, tile_valid, pos


def kernel(x, c, ctx, c_ctx, ada_w, ada_b, ln_g, ln_b, mix_in_w, pool_w, pool_scale, q_norm, q_up, kv_norm, kv_up, mix_out_w, ffn_gate, ffn_up, ffn_down, hy_in_w, hy_conv_w, hy_conv_b, hy_fw1, hy_fb1, hy_fw2, hy_fb2, hy_fw3, hy_fb3, hy_fout, hy_freq, hy_skip, hy_out_w, router_w, moe_gate, moe_up, moe_down):
    b, n, d = x.shape
    t = b * n
    assert b + 1 <= 8
    s_rows = jnp.concatenate([c, c_ctx[None, :], jnp.zeros((8 - b - 1, d), F32)], axis=0)
    mod = _ada(s_rows, ada_w, ada_b)

    def chunks(l, rows):
        m = mod[l, rows].reshape(-1, 6, d)
        return [m[:, k][:, None, :] for k in range(6)]

    sh1, sc1, g1, sh2, sc2, g2 = chunks(0, slice(0, b))
    shc, scc = chunks(0, slice(b, b + 1))[:2]
    pool_u, q, k, v = _l0a(x, ctx, sc1, sh1, scc, shc, mix_in_w[0], q_norm[0], q_up[0], kv_norm[0], kv_up[0])
    attn_o = _attention(q, k, v, n)
    h1, z1 = _l0b(pool_u, attn_o, x, g1, sc2, sh2, pool_w[0], pool_scale[0], mix_out_w[0], ln_g[0, 0], ln_b[0, 0])
    f0 = _ffn_dense(z1.reshape(t, d), ffn_gate.astype(BF16), ffn_up.astype(BF16), ffn_down.astype(BF16), 1024, 1408)
    sh1, sc1, g1b, sh2b, sc2b, g2b = chunks(1, slice(0, b))
    h2, u2 = _resid_ln(h1, [f0.reshape(b, n, d)], g2, ln_g[0, 1], ln_b[0, 1], mod=(sc1, sh1))

    x0, vx = _l1a(u2, hy_in_w[0], hy_conv_w[0], hy_conv_b[0])
    kfull = _hyena_filters(n, hy_fw1[0], hy_fb1[0], hy_fw2[0], hy_fb2[0], hy_fw3[0], hy_fb3[0],
                           hy_fout[0], hy_freq[0])
    yl = _hyena_long_conv(x0, vx, kfull, hy_skip[0])
    h3, z3, logits = _l1c(yl, h2, g1b, sc2b, sh2b, hy_out_w[0], ln_g[1, 0], ln_b[1, 0], router_w[0])

    tm_moe = 512
    row_token, row_gate, tile_expert, tile_valid, pos = _route(logits.reshape(t, LANE)[:, :N_EXPERTS], tm_moe)
    ys = _moe_grouped(z3.reshape(t, d // LANE, LANE), row_token, _cast_bf16(moe_gate[0]), _cast_bf16(moe_up[0]),
                      _cast_bf16(moe_down[0]), row_gate, tile_expert, tile_valid, tm_moe, 1792)
    y_a = _take(ys, pos[:, 0]).reshape(b, n, d)
    y_b = _take(ys, pos[:, 1]).reshape(b, n, d)
    (h4,) = _resid_ln(h3, [y_a, y_b], g2b, ln_g[1, 1], ln_b[1, 1])
    return h4
```

```python
import functools
import math

import numpy as np
import jax
import jax.numpy as jnp
from jax import lax
from jax.experimental import pallas as pl
from jax.experimental.pallas import tpu as pltpu

F32 = jnp.float32
BF16 = jnp.bfloat16
HIGHEST = lax.Precision.HIGHEST

D_MODEL = 1024
GRID_W = 64
POOL_W = 512
POOL_WINDOWS = (2, 4, 8, 16)
POOL_GC = 128
MLA_HEADS = 8
QK_NOPE = 64
QK_ROPE = 32
V_HEAD = 64
Q_LORA = 256
KV_LORA = 256
ROPE_AXIS = 16
ROPE_BASE = 10000.0
N_EXPERTS = 8
TOP_K = 2
HY_BANDS = 16
HY_EMB = 1 + 2 * HY_BANDS
HY_FAST_DECAY = 0.3
HY_SLOW_DECAY = 1.5
HY_TARGET = 1e-2
HY_MIN_DECAY = math.log(HY_TARGET) / HY_SLOW_DECAY
HY_MAX_DECAY = math.log(HY_TARGET) / HY_FAST_DECAY
LN_EPS = 1e-5
RMS_EPS = 1e-6
DEPTH = 2
ALPHA = (2.0 * DEPTH) ** 0.25

LANE = 128
HEAD_SLOT = 128
DFT_N = 128
DFT_HALF = DFT_N // 2 + 1
VMEM_LIMIT = 56 * 1024 * 1024
ATTN_LOOKAHEAD = 2


def _cparams(*sem):
    return pltpu.CompilerParams(dimension_semantics=sem, vmem_limit_bytes=VMEM_LIMIT)


def _layer_norm(v, g, b):
    mu = jnp.mean(v, axis=-1, keepdims=True)
    c = v - mu
    var = jnp.mean(c * c, axis=-1, keepdims=True)
    return c * lax.rsqrt(var + LN_EPS) * g + b


def _rms_norm(v, g):
    return v * lax.rsqrt(jnp.mean(v * v, axis=-1, keepdims=True) + RMS_EPS) * g


def _ada_kernel(s_ref, w_ref, b_ref, o_ref):
    s = s_ref[...]
    s = s * jax.nn.sigmoid(s)
    o_ref[0] = jnp.dot(s, w_ref[0], preferred_element_type=F32, precision=HIGHEST) + b_ref[0]


def _ada(s_rows, ada_w, ada_b):
    depth, d, n6 = ada_w.shape
    tn = 768
    return pl.pallas_call(
        _ada_kernel,
        out_shape=jax.ShapeDtypeStruct((depth, 8, n6), F32),
        grid=(depth, n6 // tn),
        in_specs=[
            pl.BlockSpec((8, d), lambda l, j: (0, 0)),
            pl.BlockSpec((1, d, tn), lambda l, j: (l, 0, j)),
            pl.BlockSpec((1, 1, tn), lambda l, j: (l, 0, j)),
        ],
        out_specs=pl.BlockSpec((1, 8, tn), lambda l, j: (l, 0, j)),
        compiler_params=_cparams("parallel", "parallel"),
        name="ada_mod",
    )(s_rows, ada_w, ada_b.reshape(depth, 1, n6))


def _l0a_kernel(x_ref, c_ref, sc_ref, sh_ref, scc_ref, shc_ref, win_ref, qn_ref, kvn_ref,
                wqa_ref, wqb_ref, wk_ref, wv_ref, cos_ref, sin_ref, cost_ref, sint_ref,
                pu_ref, q_ref, k_ref, v_ref, u_scr, *, n_lat_tiles):
    i = pl.program_id(1)

    @pl.when(i < n_lat_tiles)
    def _():
        u_scr[...] = (x_ref[0] * (1.0 + sc_ref[0]) + sh_ref[0]).astype(BF16)

    @pl.when(i >= n_lat_tiles)
    def _():
        u_scr[...] = (c_ref[0] * (1.0 + scc_ref[0]) + shc_ref[0]).astype(BF16)

    proj = jnp.dot(u_scr[...], win_ref[...], preferred_element_type=F32)
    pu_ref[0] = proj[:, :POOL_W]
    cos = cos_ref[...]
    sin = sin_ref[...]
    q0 = POOL_W
    kv0 = POOL_W + Q_LORA
    r0 = kv0 + KV_LORA
    qn = _rms_norm(proj[:, q0:kv0], qn_ref[...]).astype(BF16)
    kvn = _rms_norm(proj[:, kv0:r0], kvn_ref[...]).astype(BF16)
    nt_dims = (((1,), (1,)), ((), ()))
    qa = lax.dot_general(wqa_ref[...], qn, nt_dims, preferred_element_type=F32)
    qb = lax.dot_general(wqb_ref[...], qn, nt_dims, preferred_element_type=F32)
    vt = lax.dot_general(wv_ref[...], kvn, nt_dims, preferred_element_type=F32)
    kn = jnp.dot(kvn, wk_ref[...], preferred_element_type=F32)
    krot = proj[:, r0:r0 + HEAD_SLOT] * cos + proj[:, r0 + HEAD_SLOT:r0 + 2 * HEAD_SLOT] * sin
    cos_t = cost_ref[...]
    sin_t = sint_ref[...]
    for h in range(MLA_HEADS):
        sl = slice(h * HEAD_SLOT, (h + 1) * HEAD_SLOT)
        q_ref[0, h] = (qa[sl] * cos_t + qb[sl] * sin_t).astype(BF16)
        k_ref[0, h] = (kn[:, sl] + krot).astype(BF16)
        v_ref[0, h] = vt[h * V_HEAD:(h + 1) * V_HEAD].astype(BF16)


def _rope_swap_index():
    half = ROPE_AXIS // 2
    idx = []
    for a in range(2):
        base = a * ROPE_AXIS
        idx += list(range(base + half, base + ROPE_AXIS)) + list(range(base, base + half))
    return np.array(idx)


def _rope_tables(n, n_ctx):
    rows = n // GRID_W
    r = jnp.repeat(jnp.arange(rows, dtype=F32), GRID_W)
    col = jnp.tile(jnp.arange(GRID_W, dtype=F32), rows)
    inv = ROPE_BASE ** (-jnp.arange(0, ROPE_AXIS, 2, dtype=F32) / ROPE_AXIS)
    ang_r = r[:, None] * inv
    ang_c = col[:, None] * inv
    cos32 = jnp.concatenate([jnp.cos(ang_r), jnp.cos(ang_r), jnp.cos(ang_c), jnp.cos(ang_c)], axis=-1)
    sin32 = jnp.concatenate([-jnp.sin(ang_r), jnp.sin(ang_r), -jnp.sin(ang_c), jnp.sin(ang_c)], axis=-1)
    pad = HEAD_SLOT - QK_NOPE - QK_ROPE
    cos = jnp.concatenate([jnp.ones((n, QK_NOPE), F32), cos32, jnp.ones((n, pad), F32)], axis=-1)
    sin = jnp.concatenate([jnp.zeros((n, QK_NOPE), F32), sin32, jnp.zeros((n, pad), F32)], axis=-1)
    cos = jnp.concatenate([cos, jnp.ones((n_ctx, HEAD_SLOT), F32)], axis=0)
    sin = jnp.concatenate([sin, jnp.zeros((n_ctx, HEAD_SLOT), F32)], axis=0)
    return cos, sin


def _l0a_weights(in_w, q_up, kv_up):
    swap = _rope_swap_index()
    d = in_w.shape[0]
    r0 = POOL_W + Q_LORA + KV_LORA
    w_rope = in_w[:, r0:]
    pad_l = jnp.zeros((d, QK_NOPE), F32)
    pad_r = jnp.zeros((d, HEAD_SLOT - QK_NOPE - QK_ROPE), F32)
    kr_a = jnp.concatenate([pad_l, w_rope, pad_r], axis=1)
    kr_b = jnp.concatenate([pad_l, w_rope[:, swap], pad_r], axis=1)
    w_in = jnp.concatenate([in_w[:, :r0], kr_a, kr_b], axis=1).astype(BF16)

    scale = (QK_NOPE + QK_ROPE) ** -0.5 * math.log2(math.e)
    qu = q_up.reshape(Q_LORA, MLA_HEADS, QK_NOPE + QK_ROPE) * scale
    zpad = jnp.zeros((Q_LORA, MLA_HEADS, HEAD_SLOT - QK_NOPE - QK_ROPE), F32)
    wq_a = jnp.concatenate([qu, zpad], axis=-1).reshape(Q_LORA, MLA_HEADS * HEAD_SLOT).T.astype(BF16)
    wq_b = jnp.concatenate([jnp.zeros((Q_LORA, MLA_HEADS, QK_NOPE), F32), qu[..., QK_NOPE:][..., swap], zpad],
                           axis=-1).reshape(Q_LORA, MLA_HEADS * HEAD_SLOT).T.astype(BF16)
    kvu = kv_up.reshape(KV_LORA, MLA_HEADS, QK_NOPE + V_HEAD)
    wk = jnp.concatenate([kvu[..., :QK_NOPE], jnp.zeros((KV_LORA, MLA_HEADS, HEAD_SLOT - QK_NOPE), F32)],
                         axis=-1).reshape(KV_LORA, MLA_HEADS * HEAD_SLOT).astype(BF16)
    wv_t = kvu[..., QK_NOPE:].reshape(KV_LORA, MLA_HEADS * V_HEAD).T.astype(BF16)
    return w_in, wq_a, wq_b, wk, wv_t


def _l0a(x, ctx, sc, sh, scc, shc, in_w, q_norm, q_up, kv_norm, kv_up):
    b, n, d = x.shape
    n_ctx = ctx.shape[1]
    tm = 256
    nl = n // tm
    nt = (n + n_ctx) // tm
    w_in, wq_a, wq_b, wk, wv_t = _l0a_weights(in_w, q_up, kv_up)
    cos, sin = _rope_tables(n, n_ctx)
    hw = MLA_HEADS * HEAD_SLOT
    full = lambda shape: pl.BlockSpec(shape, lambda bi, i: (0,) * len(shape))
    vec = pl.BlockSpec((1, 1, d), lambda bi, i: (bi, 0, 0))
    return pl.pallas_call(
        functools.partial(_l0a_kernel, n_lat_tiles=nl),
        out_shape=[
            jax.ShapeDtypeStruct((b, n + n_ctx, POOL_W), F32),
            jax.ShapeDtypeStruct((b, MLA_HEADS, HEAD_SLOT, n + n_ctx), BF16),
            jax.ShapeDtypeStruct((b, MLA_HEADS, n + n_ctx, HEAD_SLOT), BF16),
            jax.ShapeDtypeStruct((b, MLA_HEADS, V_HEAD, n + n_ctx), BF16),
        ],
        grid=(b, nt),
        in_specs=[
            pl.BlockSpec((1, tm, d), lambda bi, i: (bi, jnp.minimum(i, nl - 1), 0)),
            pl.BlockSpec((1, tm, d), lambda bi, i: (bi, jnp.maximum(i - nl, 0), 0)),
            vec, vec, full((1, 1, d)), full((1, 1, d)),
            full(w_in.shape), full((1, Q_LORA)), full((1, KV_LORA)),
            full((hw, Q_LORA)), full((hw, Q_LORA)), full((KV_LORA, hw)), full((MLA_HEADS * V_HEAD, KV_LORA)),
            pl.BlockSpec((tm, HEAD_SLOT), lambda bi, i: (i, 0)),
            pl.BlockSpec((tm, HEAD_SLOT), lambda bi, i: (i, 0)),
            pl.BlockSpec((HEAD_SLOT, tm), lambda bi, i: (0, i)),
            pl.BlockSpec((HEAD_SLOT, tm), lambda bi, i: (0, i)),
        ],
        out_specs=[
            pl.BlockSpec((1, tm, POOL_W), lambda bi, i: (bi, i, 0)),
            pl.BlockSpec((1, MLA_HEADS, HEAD_SLOT, tm), lambda bi, i: (bi, 0, 0, i)),
            pl.BlockSpec((1, MLA_HEADS, tm, HEAD_SLOT), lambda bi, i: (bi, 0, i, 0)),
            pl.BlockSpec((1, MLA_HEADS, V_HEAD, tm), lambda bi, i: (bi, 0, 0, i)),
        ],
        scratch_shapes=[pltpu.VMEM((tm, d), BF16)],
        compiler_params=_cparams("parallel", "arbitrary"),
        name="l0_in_proj",
    )(x, ctx, sc, sh, scc, shc, w_in, q_norm.reshape(1, -1), kv_norm.reshape(1, -1),
      wq_a, wq_b, wk, wv_t, cos, sin, cos.T, sin.T)


def _attn_kernel(q_ref, k_ref, v_ref, o_ref, m_ref, l_ref, acc_ref, *, nk):
    ki = pl.program_id(2)

    @pl.when(ki == 0)
    def _():
        m_ref[...] = jnp.full(m_ref.shape, -jnp.inf, F32)
        l_ref[...] = jnp.zeros(l_ref.shape, F32)
        acc_ref[...] = jnp.zeros(acc_ref.shape, F32)

    def scores(h):
        return jnp.dot(k_ref[0, h], q_ref[0, h], preferred_element_type=F32)

    pending = [scores(h) for h in range(ATTN_LOOKAHEAD)]
    for h in range(MLA_HEADS):
        if h + ATTN_LOOKAHEAD < MLA_HEADS:
            pending.append(scores(h + ATTN_LOOKAHEAD))
        st = pending.pop(0)
        m_prev = m_ref[h]
        m_new = jnp.maximum(m_prev, jnp.max(st, axis=0, keepdims=True))
        a = jnp.exp2(m_prev - m_new)
        p = jnp.exp2(st - m_new)
        l_ref[h] = a * l_ref[h] + jnp.sum(p, axis=0, keepdims=True)
        pv = jnp.dot(v_ref[0, h], p.astype(BF16), preferred_element_type=F32)
        rows = slice(h * V_HEAD, (h + 1) * V_HEAD)
        acc_ref[rows, :] = a * acc_ref[rows, :] + pv
        m_ref[h] = m_new

    @pl.when(ki == nk - 1)
    def _():
        for h in range(MLA_HEADS):
            rows = slice(h * V_HEAD, (h + 1) * V_HEAD)
            acc_ref[rows, :] = acc_ref[rows, :] / l_ref[h]
        o_ref[0] = acc_ref[...].T.astype(o_ref.dtype)


def _attention(q, k, vt, n):
    b, h, n_all, _ = k.shape
    tq, tk = 256, 4224
    nk = n_all // tk
    return pl.pallas_call(
        functools.partial(_attn_kernel, nk=nk),
        out_shape=jax.ShapeDtypeStruct((b, n, h * V_HEAD), BF16),
        grid=(b, n // tq, nk),
        in_specs=[
            pl.BlockSpec((1, h, HEAD_SLOT, tq), lambda bi, qi, ki: (bi, 0, 0, qi)),
            pl.BlockSpec((1, h, tk, HEAD_SLOT), lambda bi, qi, ki: (bi, 0, ki, 0)),
            pl.BlockSpec((1, h, V_HEAD, tk), lambda bi, qi, ki: (bi, 0, 0, ki)),
        ],
        out_specs=pl.BlockSpec((1, tq, h * V_HEAD), lambda bi, qi, ki: (bi, qi, 0)),
        scratch_shapes=[
            pltpu.VMEM((h, 1, tq), F32),
            pltpu.VMEM((h, 1, tq), F32),
            pltpu.VMEM((h * V_HEAD, tq), F32),
        ],
        compiler_params=_cparams("parallel", "parallel", "arbitrary"),
        name="mla_attention",
    )(q, k, vt)


def _l0b_kernel(pp_ref, pc_ref, pn_ref, o_ref, x_ref, g1_ref, sc2_ref, sh2_ref, pw_ref, ps_ref, ow_ref,
                lg_ref, lb_ref, h_ref, z_ref, *, tm, n):
    i = pl.program_id(1)
    halo = 8
    ext = jnp.concatenate([pp_ref[0], pc_ref[0], pn_ref[0]], axis=0)
    pos = i * tm - halo + lax.broadcasted_iota(jnp.int32, (tm + 2 * halo, 1), 0)
    ext = jnp.where((pos >= 0) & (pos < n), ext, 0.0)
    t = i * tm + lax.broadcasted_iota(jnp.int32, (tm, 1), 0)
    y = jnp.dot(o_ref[0], ow_ref[POOL_W:, :], preferred_element_type=F32)
    for g, w in enumerate(POOL_WINDOWS):
        hw = w // 2
        e = ext[:, g * POOL_GC:(g + 1) * POOL_GC]
        s = e
        width = 1
        while width < w:
            s = s[:s.shape[0] - width] + s[width:]
            width *= 2
        win = s[halo - hw:halo - hw + tm]
        cnt = (jnp.minimum(t + hw, n) - jnp.maximum(t - hw, 0)).astype(F32)
        dd = (win / cnt - e[halo:halo + tm]).astype(BF16)
        yg = jnp.dot(dd, pw_ref[g], preferred_element_type=F32) * ps_ref[:, g * POOL_GC:(g + 1) * POOL_GC]
        y = y + jnp.dot(yg.astype(BF16), ow_ref[g * POOL_GC:(g + 1) * POOL_GC, :], preferred_element_type=F32)
    hh = _layer_norm(ALPHA * x_ref[0] + g1_ref[0] * y, lg_ref[...], lb_ref[...])
    h_ref[0] = hh
    z_ref[0] = (hh * (1.0 + sc2_ref[0]) + sh2_ref[0]).astype(BF16)


def _l0b(pool_u, attn_o, x, g1, sc2, sh2, pool_w, pool_scale, out_w, ln_g, ln_b):
    b, n, d = x.shape
    tm = 512
    hb = tm // 8
    vec = pl.BlockSpec((1, 1, d), lambda bi, i: (bi, 0, 0))
    full = lambda shape: pl.BlockSpec(shape, lambda bi, i: (0,) * len(shape))
    return pl.pallas_call(
        functools.partial(_l0b_kernel, tm=tm, n=n),
        out_shape=[jax.ShapeDtypeStruct((b, n, d), F32), jax.ShapeDtypeStruct((b, n, d), BF16)],
        grid=(b, n // tm),
        in_specs=[
            pl.BlockSpec((1, 8, POOL_W), lambda bi, i: (bi, jnp.maximum(i * hb - 1, 0), 0)),
            pl.BlockSpec((1, tm, POOL_W), lambda bi, i: (bi, i, 0)),
            pl.BlockSpec((1, 8, POOL_W), lambda bi, i: (bi, (i + 1) * hb, 0)),
            pl.BlockSpec((1, tm, POOL_W), lambda bi, i: (bi, i, 0)),
            pl.BlockSpec((1, tm, d), lambda bi, i: (bi, i, 0)),
            vec, vec, vec,
            full(pool_w.shape), full((1, POOL_W)), full(out_w.shape), full((1, d)), full((1, d)),
        ],
        out_specs=[pl.BlockSpec((1, tm, d), lambda bi, i: (bi, i, 0)),
                   pl.BlockSpec((1, tm, d), lambda bi, i: (bi, i, 0))],
        compiler_params=_cparams("parallel", "parallel"),
        name="l0_pool_out_ln",
    )(pool_u, pool_u, pool_u, attn_o, x, g1, sc2, sh2, pool_w.astype(BF16), pool_scale.reshape(1, -1),
      out_w.astype(BF16), ln_g.reshape(1, -1), ln_b.reshape(1, -1))


def _swiglu_step(z, wg_ref, wu_ref, wd_ref, acc_ref):
    g = jnp.dot(z, wg_ref[0].astype(BF16), preferred_element_type=F32)
    u = jnp.dot(z, wu_ref[0].astype(BF16), preferred_element_type=F32)
    a = (g * jax.nn.sigmoid(g) * u).astype(BF16)
    acc_ref[...] += jnp.dot(a, wd_ref[0].astype(BF16), preferred_element_type=F32)


def _ffn_kernel(z_ref, wg_ref, wu_ref, wd_ref, o_ref, acc_ref, *, nj):
    j = pl.program_id(1)

    @pl.when(j == 0)
    def _():
        acc_ref[...] = jnp.zeros(acc_ref.shape, F32)

    _swiglu_step(z_ref[...], wg_ref, wu_ref, wd_ref, acc_ref)

    @pl.when(j == nj - 1)
    def _():
        o_ref[...] = acc_ref[...]


def _ffn_dense(z, wg, wu, wd, tm, tf):
    t, d = z.shape
    f = wg.shape[-1]
    nj = f // tf
    return pl.pallas_call(
        functools.partial(_ffn_kernel, nj=nj),
        out_shape=jax.ShapeDtypeStruct((t, d), F32),
        grid=(t // tm, nj),
        in_specs=[
            pl.BlockSpec((tm, d), lambda i, j: (i, 0)),
            pl.BlockSpec((1, d, tf), lambda i, j: (0, 0, j)),
            pl.BlockSpec((1, d, tf), lambda i, j: (0, 0, j)),
            pl.BlockSpec((1, tf, d), lambda i, j: (0, j, 0)),
        ],
        out_specs=pl.BlockSpec((tm, d), lambda i, j: (i, 0)),
        scratch_shapes=[pltpu.VMEM((tm, d), F32)],
        compiler_params=_cparams("parallel", "arbitrary"),
        name="swiglu_dense",
    )(z, wg, wu, wd)


def _cast_kernel(x_ref, o_ref):
    o_ref[...] = x_ref[...].astype(o_ref.dtype)


def _cast_bf16(w):
    e, r, c = w.shape
    tr = 512
    blk = pl.BlockSpec((1, tr, c), lambda ei, ri: (ei, ri, 0))
    return pl.pallas_call(
        _cast_kernel,
        out_shape=jax.ShapeDtypeStruct(w.shape, BF16),
        grid=(e, r // tr),
        in_specs=[blk],
        out_specs=blk,
        compiler_params=_cparams("parallel", "parallel"),
        name="cast_bf16",
    )(w)


MOE_LOOKAHEAD = 2


def _moe_kernel(te_ref, tv_ref, *refs, nj, tm, n_tiles):
    tok_first = refs[:MOE_LOOKAHEAD]
    tokn_ref, z_hbm, wg_ref, wu_ref, wd_ref, gate_ref, o_ref, zbuf, zb16, sem, acc_ref = refs[MOE_LOOKAHEAD:]
    i = pl.program_id(0)
    j = pl.program_id(1)
    slots = zbuf.shape[0]
    slot = lax.rem(i, slots)
    nxt = lax.rem(i + MOE_LOOKAHEAD, slots)
    segs = z_hbm.shape[1]
    per_step = zbuf.shape[1] // (segs * nj)

    def row_copy(tok_ref, row, s, priority=0):
        tok = tok_ref[0, 0, jnp.minimum(row, tm - 1)]
        pltpu.async_copy(z_hbm.at[tok], zbuf.at[s, pl.ds(row * segs, segs), :], sem.at[s], priority=priority)

    def wait_slot(s):
        pltpu.make_async_copy(zbuf.at[s], zbuf.at[s], sem.at[s]).wait()

    def issue_next():
        for r in range(per_step):
            row_copy(tokn_ref, j * per_step + r, nxt, priority=1)

    @pl.when((i == 0) & (j == 0))
    def _():
        for k, tok_ref in enumerate(tok_first):
            def body(r, carry):
                row_copy(tok_ref, r, k)
                return carry
            lax.fori_loop(0, per_step * nj, body, 0)

    @pl.when(j == 0)
    def _():
        wait_slot(slot)
        for g in range(segs):
            zb16[:, g * LANE:(g + 1) * LANE] = zbuf.at[slot][pl.ds(g, tm, stride=segs), :].astype(BF16)
        acc_ref[...] = jnp.zeros(acc_ref.shape, F32)

    @pl.when(tv_ref[i] > 0)
    def _():
        issue_next()
        _swiglu_step(zb16[...], wg_ref, wu_ref, wd_ref, acc_ref)

    @pl.when(tv_ref[i] == 0)
    def _():
        issue_next()

    @pl.when(j == nj - 1)
    def _():
        o_ref[...] = acc_ref[...] * gate_ref[...]

    @pl.when((i == n_tiles - 1) & (j == nj - 1))
    def _():
        for k in range(1, MOE_LOOKAHEAD + 1):
            wait_slot(lax.rem(i + k, slots))


def _moe_grouped(z, row_token, wg, wu, wd, gate, tile_expert, tile_valid, tm, tf):
    t, segs, _ = z.shape
    d = segs * LANE
    p = row_token.shape[0]
    n_tiles = p // tm
    f = wg.shape[-1]
    nj = f // tf
    tok = row_token.reshape(n_tiles, 1, tm)
    buf_rows = nj * (-(-tm // (8 * nj)) * 8)
    smem_blk = lambda imap: pl.BlockSpec((1, 1, tm), imap, memory_space=pltpu.SMEM)
    return pl.pallas_call(
        functools.partial(_moe_kernel, nj=nj, tm=tm, n_tiles=n_tiles),
        out_shape=jax.ShapeDtypeStruct((p, d), F32),
        grid_spec=pltpu.PrefetchScalarGridSpec(
            num_scalar_prefetch=2,
            grid=(n_tiles, nj),
            in_specs=[
                *[smem_blk(functools.partial(lambda i, j, te, tv, k: (k, 0, 0), k=k)) for k in range(MOE_LOOKAHEAD)],
                smem_blk(lambda i, j, te, tv: (jnp.minimum(i + MOE_LOOKAHEAD, n_tiles - 1), 0, 0)),
                pl.BlockSpec(memory_space=pl.ANY),
                pl.BlockSpec((1, d, tf), lambda i, j, te, tv: (te[i], 0, j)),
                pl.BlockSpec((1, d, tf), lambda i, j, te, tv: (te[i], 0, j)),
                pl.BlockSpec((1, tf, d), lambda i, j, te, tv: (te[i], j, 0)),
                pl.BlockSpec((tm, 1), lambda i, j, te, tv: (i, 0)),
            ],
            out_specs=pl.BlockSpec((tm, d), lambda i, j, te, tv: (i, 0)),
            scratch_shapes=[
                pltpu.VMEM((MOE_LOOKAHEAD + 1, buf_rows * segs, LANE), F32),
                pltpu.VMEM((tm, d), BF16),
                pltpu.SemaphoreType.DMA((MOE_LOOKAHEAD + 1,)),
                pltpu.VMEM((tm, d), F32),
            ],
        ),
        compiler_params=_cparams("arbitrary", "arbitrary"),
        name="moe_swiglu_gather",
    )(tile_expert, tile_valid, *([tok] * (MOE_LOOKAHEAD + 1)), z, wg, wu, wd, gate)


def _resid_ln_kernel(*refs, n_y, with_mod):
    h_ref = refs[0]
    y_refs = refs[1:1 + n_y]
    g_ref, lg_ref, lb_ref = refs[1 + n_y:4 + n_y]
    rest = refs[4 + n_y:]
    y = y_refs[0][0]
    for r in y_refs[1:]:
        y = y + r[0]
    hh = _layer_norm(ALPHA * h_ref[0] + g_ref[0] * y, lg_ref[...], lb_ref[...])
    if with_mod:
        sc_ref, sh_ref, ho_ref, u_ref = rest
        ho_ref[0] = hh
        u_ref[0] = (hh * (1.0 + sc_ref[0]) + sh_ref[0]).astype(BF16)
    else:
        rest[0][0] = hh


def _resid_ln(h, ys, g, ln_g, ln_b, mod=None):
    b, n, d = h.shape
    tm = 1024
    blk = pl.BlockSpec((1, tm, d), lambda bi, i: (bi, i, 0))
    vec = pl.BlockSpec((1, 1, d), lambda bi, i: (bi, 0, 0))
    row = pl.BlockSpec((1, d), lambda bi, i: (0, 0))
    in_specs = [blk] + [blk] * len(ys) + [vec, row, row]
    args = [h, *ys, g, ln_g.reshape(1, -1), ln_b.reshape(1, -1)]
    out_shape = [jax.ShapeDtypeStruct((b, n, d), F32)]
    out_specs = [blk]
    if mod is not None:
        in_specs += [vec, vec]
        args += list(mod)
        out_shape.append(jax.ShapeDtypeStruct((b, n, d), BF16))
        out_specs.append(blk)
    return pl.pallas_call(
        functools.partial(_resid_ln_kernel, n_y=len(ys), with_mod=mod is not None),
        out_shape=out_shape,
        grid=(b, n // tm),
        in_specs=in_specs,
        out_specs=out_specs,
        compiler_params=_cparams("parallel", "parallel"),
        name="resid_ln",
    )(*args)


def _l1a_kernel(up_ref, uc_ref, un_ref, w_ref, cw_ref, cb_ref, x0_ref, vx_ref, *, tm, n):
    i = pl.program_id(1)
    halo = 16
    ext = jnp.concatenate([up_ref[0], uc_ref[0], un_ref[0]], axis=0)
    pos = i * tm - halo + lax.broadcasted_iota(jnp.int32, (tm + 2 * halo, 1), 0)
    ext = jnp.where((pos >= 0) & (pos < n), ext, jnp.zeros_like(ext))
    z = jnp.dot(ext, w_ref[...], preferred_element_type=F32)
    cw = cw_ref[...]
    zc = (cw[0:1] * z[halo - 1:halo - 1 + tm] + cw[1:2] * z[halo:halo + tm]
          + cw[2:3] * z[halo + 1:halo + 1 + tm] + cb_ref[...])
    c = zc.shape[1] // 3
    x0_ref[0] = zc[:, :c].astype(x0_ref.dtype)
    vx_ref[0] = (zc[:, 2 * c:] * zc[:, c:2 * c]).astype(vx_ref.dtype)


def _l1a(u, hy_in_w, conv_w, conv_b):
    b, n, d = u.shape
    c3 = hy_in_w.shape[1]
    c = c3 // 3
    tm = 1024
    hb = tm // 16
    nb16 = n // 16
    full = lambda shape: pl.BlockSpec(shape, lambda bi, i: (0,) * len(shape))
    return pl.pallas_call(
        functools.partial(_l1a_kernel, tm=tm, n=n),
        out_shape=[jax.ShapeDtypeStruct((b, n, c), BF16), jax.ShapeDtypeStruct((b, n, c), BF16)],
        grid=(b, n // tm),
        in_specs=[
            pl.BlockSpec((1, 16, d), lambda bi, i: (bi, jnp.maximum(i * hb - 1, 0), 0)),
            pl.BlockSpec((1, tm, d), lambda bi, i: (bi, i, 0)),
            pl.BlockSpec((1, 16, d), lambda bi, i: (bi, jnp.minimum((i + 1) * hb, nb16 - 1), 0)),
            full((d, c3)), full((3, c3)), full((1, c3)),
        ],
        out_specs=[pl.BlockSpec((1, tm, c), lambda bi, i: (bi, i, 0)),
                   pl.BlockSpec((1, tm, c), lambda bi, i: (bi, i, 0))],
        compiler_params=_cparams("parallel", "parallel"),
        name="hyena_in_conv",
    )(u, u, u, hy_in_w.astype(BF16), conv_w, conv_b.reshape(1, -1))


def _filter_kernel(zf_ref, zb_ref, w1f_ref, w1b_ref, b1_ref, w2_ref, b2_ref, w3_ref, b3_ref, wof_ref, wob_ref,
                   fr_ref, dl_ref, k_ref):
    zf = zf_ref[...]
    zb = zb_ref[...]
    fr = fr_ref[...]
    dot = functools.partial(jnp.dot, preferred_element_type=F32, precision=HIGHEST)
    h = jnp.sin(fr * (dot(zf, w1f_ref[...]) + dot(zb, w1b_ref[...]) + b1_ref[...]))
    h = jnp.sin(fr * (dot(h, w2_ref[...]) + b2_ref[...]))
    h = jnp.sin(fr * (dot(h, w3_ref[...]) + b3_ref[...]))
    dl = dl_ref[...]
    h_hi = h.astype(BF16)
    h_lo = (h - h_hi.astype(F32)).astype(BF16)

    def dot3(w_ref):
        bdot = functools.partial(jnp.dot, preferred_element_type=F32)
        return bdot(h_hi, w_ref[0]) + (bdot(h_hi, w_ref[1]) + bdot(h_lo, w_ref[0]))

    k_ref[0] = dot3(wof_ref) * jnp.exp(-zf[:, 0:1] * dl)
    ob = dot3(wob_ref) * jnp.exp(-zb[:, 0:1] * dl)
    first = (pl.program_id(0) == 0) & (lax.broadcasted_iota(jnp.int32, (ob.shape[0], 1), 0) == 0)
    k_ref[1] = jnp.where(first, 0.0, ob)


def _hyena_filters(n, fw1, fb1, fw2, fb2, fw3, fb3, fout, freq):
    c = fout.shape[1] // 2
    fh = fw2.shape[0]
    bands = jnp.linspace(1e-4, HY_BANDS - 1, HY_BANDS, dtype=F32)[None, :]

    def features(p):
        w_ang = (2.0 * math.pi / n) * p
        return jnp.concatenate([p / (n - 1.0), jnp.cos(bands * w_ang), -jnp.sin(bands * w_ang),
                                jnp.zeros((n, LANE - HY_EMB), F32)], axis=-1)

    m = jnp.arange(n, dtype=F32)[:, None]
    z_f, z_b = features(m), features(n - m)
    deltas = jnp.abs(jnp.linspace(HY_MIN_DECAY, HY_MAX_DECAY, c, dtype=F32))[None, :]
    zeros = lambda r, cc: jnp.zeros((r, cc), F32)
    two = lambda a: jnp.concatenate([a.reshape(1, -1), a.reshape(1, -1)], axis=1)
    diag2 = lambda w: jnp.concatenate([jnp.concatenate([w, zeros(fh, fh)], axis=1),
                                       jnp.concatenate([zeros(fh, fh), w], axis=1)], axis=0)
    w1 = jnp.pad(fw1, ((0, LANE - fw1.shape[0]), (0, 0)))
    w1f = jnp.concatenate([w1, zeros(LANE, fh)], axis=1)
    w1b = jnp.concatenate([zeros(LANE, fh), w1], axis=1)

    def split(w):
        hi = w.astype(BF16)
        return jnp.stack([hi, (w - hi.astype(F32)).astype(BF16)])

    wof = split(jnp.concatenate([fout[:, :c], zeros(fh, c)], axis=0))
    wob = split(jnp.concatenate([zeros(fh, c), fout[:, c:]], axis=0))
    tm = 1024
    full = lambda shape: pl.BlockSpec(shape, lambda i: (0,) * len(shape))
    rows = pl.BlockSpec((tm, LANE), lambda i: (i, 0))
    out = pl.pallas_call(
        _filter_kernel,
        out_shape=jax.ShapeDtypeStruct((2, n, c), F32),
        grid=(n // tm,),
        in_specs=[rows, rows, full((LANE, LANE)), full((LANE, LANE)), full((1, LANE)),
                  full((LANE, LANE)), full((1, LANE)), full((LANE, LANE)), full((1, LANE)),
                  full((2, LANE, c)), full((2, LANE, c)), full((1, LANE)), full((1, c))],
        out_specs=pl.BlockSpec((2, tm, c), lambda i: (0, i, 0)),
        compiler_params=_cparams("parallel"),
        name="hyena_filters",
    )(z_f, z_b, w1f, w1b, two(fb1), diag2(fw2), two(fb2), diag2(fw3), two(fb3), wof, wob, two(freq), deltas)
    return out.reshape(2 * n, c)


def _dft_consts():
    kk = np.arange(DFT_N)[:, None] * np.arange(DFT_N)[None, :]
    c, s = np.cos(2.0 * np.pi * kk / DFT_N), np.sin(2.0 * np.pi * kk / DFT_N)
    g_fwd = np.block([[c, s], [-s, c]])
    g_inv = np.block([[c, -s], [s, c]])
    return jnp.asarray(g_fwd, BF16), jnp.asarray(g_inv, BF16)


DFT_ROW_PAD = 72
CONV_UNROLL = 64
CONV_UNROLL_MID = 13
FFT_PITCH = 136


def _conv_tables(n1):
    length = DFT_N * DFT_N
    k1 = np.arange(DFT_HALF)[None, :, None]
    n = (DFT_N * np.arange(n1)[None, None, :] + np.arange(DFT_N)[:, None, None])
    th = 2.0 * np.pi * k1 * n / length
    pad = np.zeros((DFT_N, DFT_ROW_PAD - DFT_HALF, n1))
    fa = np.concatenate([np.cos(th), pad, -np.sin(th), pad], axis=1)
    wgt = np.full((1, 1, DFT_HALF), 2.0)
    wgt[..., 0] = wgt[..., -1] = 1.0
    tht = np.transpose(th, (0, 2, 1))
    padk = np.zeros((DFT_N, n1, DFT_ROW_PAD - DFT_HALF))
    fo_r = np.concatenate([wgt * np.cos(tht) / length, padk], axis=2)
    fo_i = np.concatenate([-wgt * np.sin(tht) / length, padk], axis=2)
    return (jnp.asarray(fa, BF16), jnp.asarray(fo_r, BF16), jnp.asarray(fo_i, BF16))


def _conv_fused_kernel(vx_ref, x0_ref, sk_ref, gf_ref, gi_ref, fa_hbm, for_hbm, foi_hbm, hr_hbm, hi_hbm, o_ref,
                       xs, ar, ai, fa_s, for_s, foi_s, hr_s, hi_s):
    ci = pl.program_id(0)
    bi = pl.program_id(1)
    lanes = vx_ref.shape[2]
    n1c = vx_ref.shape[1] // DFT_N

    @pl.when((ci == 0) & (bi == 0))
    def _():
        pltpu.sync_copy(fa_hbm, fa_s)
        pltpu.sync_copy(for_hbm, for_s)
        pltpu.sync_copy(foi_hbm, foi_s)

    @pl.when(bi == 0)
    def _():
        lane0 = pl.multiple_of(ci * lanes, lanes)
        pltpu.sync_copy(hr_hbm.at[:, :, pl.ds(lane0, lanes)], hr_s)
        pltpu.sync_copy(hi_hbm.at[:, :, pl.ds(lane0, lanes)], hi_s)

    for p in range(n1c):
        xs[pl.ds(p * FFT_PITCH, DFT_N), :] = vx_ref[0, pl.ds(p * DFT_N, DFT_N), :].astype(F32)

    def fwd(g, carry):
        n2s = [g * CONV_UNROLL + u for u in range(CONV_UNROLL)]
        xgs = [xs[pl.ds(n2, n1c, stride=FFT_PITCH), :].astype(BF16) for n2 in n2s]
        ys = [jnp.dot(fa_s[n2], xg, preferred_element_type=F32) for n2, xg in zip(n2s, xgs)]
        for n2, y in zip(n2s, ys):
            ar[pl.ds(n2, DFT_ROW_PAD, stride=FFT_PITCH), :] = y[:DFT_ROW_PAD]
            ai[pl.ds(n2, DFT_ROW_PAD, stride=FFT_PITCH), :] = y[DFT_ROW_PAD:]
        return carry
    lax.fori_loop(0, DFT_N // CONV_UNROLL, fwd, 0)

    def mid_one(a2, k1):
        x = jnp.dot(gf_ref[...], a2, preferred_element_type=F32)
        xr, xi = x[:DFT_N], x[DFT_N:]
        hr = hr_s[k1]
        hi = hi_s[k1]
        z = jnp.concatenate([xr * hr - xi * hi, xr * hi + xi * hr], axis=0).astype(BF16)
        return jnp.dot(gi_ref[...], z, preferred_element_type=F32)

    def mid(g, carry):
        k1s = [jnp.minimum(g * CONV_UNROLL_MID + u, DFT_HALF - 1) for u in range(CONV_UNROLL_MID)]
        bases = [pl.multiple_of(k1 * FFT_PITCH, 8) for k1 in k1s]
        a2s = [jnp.concatenate([ar[pl.ds(bs, DFT_N), :], ai[pl.ds(bs, DFT_N), :]], axis=0).astype(BF16)
               for bs in bases]
        ys = [mid_one(a2, k1) for a2, k1 in zip(a2s, k1s)]
        for bs, y in zip(bases, ys):
            ar[pl.ds(bs, DFT_N), :] = y[:DFT_N]
            ai[pl.ds(bs, DFT_N), :] = y[DFT_N:]
        return carry
    lax.fori_loop(0, -(-DFT_HALF // CONV_UNROLL_MID), mid, 0)

    def inv(g, carry):
        n2s = [g * CONV_UNROLL + u for u in range(CONV_UNROLL)]
        brs = [ar[pl.ds(n2, DFT_ROW_PAD, stride=FFT_PITCH), :].astype(BF16) for n2 in n2s]
        bis = [ai[pl.ds(n2, DFT_ROW_PAD, stride=FFT_PITCH), :].astype(BF16) for n2 in n2s]
        ys = [jnp.dot(for_s[n2], br, preferred_element_type=F32) + jnp.dot(foi_s[n2], bi_, preferred_element_type=F32)
              for n2, br, bi_ in zip(n2s, brs, bis)]
        for n2, y in zip(n2s, ys):
            xs[pl.ds(n2, n1c, stride=FFT_PITCH), :] = y
        return carry
    lax.fori_loop(0, DFT_N // CONV_UNROLL, inv, 0)

    sk = sk_ref[...]
    for p in range(n1c):
        rows = pl.ds(p * DFT_N, DFT_N)
        conv = xs[pl.ds(p * FFT_PITCH, DFT_N), :]
        o_ref[0, rows, :] = (x0_ref[0, rows, :].astype(F32)
                             * (conv + vx_ref[0, rows, :].astype(F32) * sk)).astype(o_ref.dtype)


def _conv_fused(x0, vx, hr, hi, skip, g_fwd, g_inv):
    b, n, c = vx.shape
    n1 = n // DFT_N
    lanes = LANE
    fa, fo_r, fo_i = _conv_tables(n1)
    blk = pl.BlockSpec((1, n, lanes), lambda ci, bi: (bi, 0, ci))
    full = lambda a: pl.BlockSpec(a.shape, lambda ci, bi: (0,) * a.ndim)
    hbm = pl.BlockSpec(memory_space=pl.ANY)
    return pl.pallas_call(
        _conv_fused_kernel,
        out_shape=jax.ShapeDtypeStruct((b, n, c), BF16),
        grid=(c // lanes, b),
        in_specs=[blk, blk, pl.BlockSpec((1, lanes), lambda ci, bi: (0, ci)), full(g_fwd), full(g_inv),
                  hbm, hbm, hbm, hbm, hbm],
        out_specs=blk,
        scratch_shapes=[
            pltpu.VMEM((n1 * FFT_PITCH, lanes), F32),
            pltpu.VMEM((DFT_ROW_PAD * FFT_PITCH, lanes), F32),
            pltpu.VMEM((DFT_ROW_PAD * FFT_PITCH, lanes), F32),
            pltpu.VMEM(fa.shape, BF16), pltpu.VMEM(fo_r.shape, BF16), pltpu.VMEM(fo_i.shape, BF16),
            pltpu.VMEM((DFT_HALF, DFT_N, lanes), F32), pltpu.VMEM((DFT_HALF, DFT_N, lanes), F32),
        ],
        compiler_params=_cparams("arbitrary", "arbitrary"),
        name="long_conv_fused",
    )(vx, x0, skip.reshape(1, c), g_fwd, g_inv, fa, fo_r, fo_i, hr, hi)


def _filter_spec_kernel(k_hbm, gf_ref, fa_hbm, hr_ref, hi_ref, ks, ar, ai, fa_s, sem):
    ci = pl.program_id(0)
    lanes = ks.shape[1]
    lane0 = pl.multiple_of(ci * lanes, lanes)

    @pl.when(ci == 0)
    def _():
        pltpu.sync_copy(fa_hbm, fa_s)

    def plane_copy(p):
        return pltpu.make_async_copy(k_hbm.at[pl.ds(p * DFT_N, DFT_N), pl.ds(lane0, lanes)],
                                     ks.at[pl.ds(p * FFT_PITCH, DFT_N), :], sem.at[0])
    for p in range(DFT_N):
        plane_copy(p).start()
    for p in range(DFT_N):
        plane_copy(p).wait()

    def fwd(g, carry):
        n2s = [g * CONV_UNROLL + u for u in range(CONV_UNROLL)]
        xgs = [ks[pl.ds(n2, DFT_N, stride=FFT_PITCH), :].astype(BF16) for n2 in n2s]
        ys = [jnp.dot(fa_s[n2], xg, preferred_element_type=F32) for n2, xg in zip(n2s, xgs)]
        for n2, y in zip(n2s, ys):
            ar[pl.ds(n2, DFT_ROW_PAD, stride=FFT_PITCH), :] = y[:DFT_ROW_PAD]
            ai[pl.ds(n2, DFT_ROW_PAD, stride=FFT_PITCH), :] = y[DFT_ROW_PAD:]
        return carry
    lax.fori_loop(0, DFT_N // CONV_UNROLL, fwd, 0)

    def spec(g, carry):
        k1s = [jnp.minimum(g * CONV_UNROLL_MID + u, DFT_HALF - 1) for u in range(CONV_UNROLL_MID)]
        bases = [pl.multiple_of(k1 * FFT_PITCH, 8) for k1 in k1s]
        a2s = [jnp.concatenate([ar[pl.ds(bs, DFT_N), :], ai[pl.ds(bs, DFT_N), :]], axis=0).astype(BF16)
               for bs in bases]
        xs_ = [jnp.dot(gf_ref[...], a2, preferred_element_type=F32) for a2 in a2s]
        for k1, x in zip(k1s, xs_):
            hr_ref[k1] = x[:DFT_N]
            hi_ref[k1] = x[DFT_N:]
        return carry
    lax.fori_loop(0, -(-DFT_HALF // CONV_UNROLL_MID), spec, 0)


def _filter_spectrum_fused(kfull, g_fwd):
    rows, c = kfull.shape
    lanes = LANE
    fa = _conv_tables(rows // DFT_N)[0]
    out = pl.BlockSpec((DFT_HALF, DFT_N, lanes), lambda ci: (0, 0, ci))
    hbm = pl.BlockSpec(memory_space=pl.ANY)
    return pl.pallas_call(
        _filter_spec_kernel,
        out_shape=[jax.ShapeDtypeStruct((DFT_HALF, DFT_N, c), F32)] * 2,
        grid=(c // lanes,),
        in_specs=[hbm, pl.BlockSpec(g_fwd.shape, lambda ci: (0, 0)), hbm],
        out_specs=[out, out],
        scratch_shapes=[
            pltpu.VMEM((DFT_N * FFT_PITCH, lanes), F32),
            pltpu.VMEM((DFT_ROW_PAD * FFT_PITCH, lanes), F32),
            pltpu.VMEM((DFT_ROW_PAD * FFT_PITCH, lanes), F32),
            pltpu.VMEM(fa.shape, BF16),
            pltpu.SemaphoreType.DMA((1,)),
        ],
        compiler_params=_cparams("arbitrary"),
        name="filter_spectrum_fused",
    )(kfull, g_fwd, fa)


def _hyena_long_conv(x0, vx, kfull, skip):
    g_fwd, g_inv = _dft_consts()
    hr, hi = _filter_spectrum_fused(kfull, g_fwd)
    return _conv_fused(x0, vx, hr, hi, skip, g_fwd, g_inv)


def _l1c_kernel(y_ref, h_ref, g1_ref, sc2_ref, sh2_ref, w_ref, lg_ref, lb_ref, rw_ref, ho_ref, z_ref, lo_ref):
    y = jnp.dot(y_ref[0], w_ref[...], preferred_element_type=F32)
    hh = _layer_norm(ALPHA * h_ref[0] + g1_ref[0] * y, lg_ref[...], lb_ref[...])
    ho_ref[0] = hh
    z = hh * (1.0 + sc2_ref[0]) + sh2_ref[0]
    segs = z.shape[1] // LANE
    for s in range(segs):
        z_ref[0, pl.ds(s, z.shape[0], stride=segs), :] = z[:, s * LANE:(s + 1) * LANE]
    z_hi = z.astype(BF16)
    z_lo = (z - z_hi.astype(F32)).astype(BF16)
    bdot = functools.partial(jnp.dot, preferred_element_type=F32)
    lo_ref[0] = bdot(z_hi, rw_ref[0]) + (bdot(z_hi, rw_ref[1]) + bdot(z_lo, rw_ref[0]))


def _l1c(y, h, g1, sc2, sh2, out_w, ln_g, ln_b, router_w):
    b, n, d = h.shape
    tm = 1024
    blk = pl.BlockSpec((1, tm, d), lambda bi, i: (bi, i, 0))
    vec = pl.BlockSpec((1, 1, d), lambda bi, i: (bi, 0, 0))
    full = lambda shape: pl.BlockSpec(shape, lambda bi, i: (0,) * len(shape))
    rw = jnp.pad(router_w, ((0, 0), (0, LANE - router_w.shape[1])))
    rw_hi = rw.astype(BF16)
    rw = jnp.stack([rw_hi, (rw - rw_hi.astype(F32)).astype(BF16)])
    segs = d // LANE
    return pl.pallas_call(
        _l1c_kernel,
        out_shape=[jax.ShapeDtypeStruct((b, n, d), F32), jax.ShapeDtypeStruct((b, n * segs, LANE), F32),
                   jax.ShapeDtypeStruct((b, n, LANE), F32)],
        grid=(b, n // tm),
        in_specs=[blk, blk, vec, vec, vec, full((d, d)), full((1, d)), full((1, d)), full((2, d, LANE))],
        out_specs=[blk, pl.BlockSpec((1, tm * segs, LANE), lambda bi, i: (bi, i, 0)),
                   pl.BlockSpec((1, tm, LANE), lambda bi, i: (bi, i, 0))],
        compiler_params=_cparams("parallel", "parallel"),
        name="hyena_out_ln_router",
    )(y, h, g1, sc2, sh2, out_w.astype(BF16), ln_g.reshape(1, -1), ln_b.reshape(1, -1), rw)


def _take(a, idx):
    return a.at[idx].get(mode="promise_in_bounds")


def _route(logits, tm):
    t = logits.shape[0]
    top_v, top_i = lax.top_k(logits, TOP_K)
    gates = jax.nn.softmax(top_v, axis=-1)
    flat_e = top_i.reshape(-1).astype(jnp.int32)
    flat_g = gates.reshape(-1)
    n_sel = t * TOP_K
    p = n_sel + N_EXPERTS * tm
    eids = jnp.arange(N_EXPERTS, dtype=jnp.int32)[None, :]
    onehot = (flat_e[:, None] == eids).astype(jnp.int32)
    csum = jnp.cumsum(onehot, axis=0)
    counts = csum[-1]
    rank = jnp.sum((csum - onehot) * onehot, axis=1)
    padded = ((counts + tm - 1) // tm) * tm
    end_p = jnp.cumsum(padded)
    start_p = end_p - padded
    start = jnp.cumsum(counts) - counts
    pos = (jnp.sum(onehot * start_p[None, :], axis=1) + rank).reshape(t, TOP_K)
    order = jnp.argsort(flat_e, stable=True).astype(jnp.int32)
    r = jnp.arange(p, dtype=jnp.int32)
    e_row = jnp.sum(r[:, None] >= end_p[None, :], axis=1).astype(jnp.int32)
    oh_r = (jnp.minimum(e_row, N_EXPERTS - 1)[:, None] == eids).astype(jnp.int32)
    j = r - jnp.sum(oh_r * start_p[None, :], axis=1)
    valid = (e_row < N_EXPERTS) & (j < jnp.sum(oh_r * counts[None, :], axis=1))
    src = jnp.clip(jnp.sum(oh_r * start[None, :], axis=1) + j, 0, n_sel - 1)
    flat_idx = _take(order, src)
    row_token = jnp.where(valid, flat_idx // TOP_K, 0)
    row_gate = jnp.where(valid, _take(flat_g, flat_idx), 0.0)
    tile_start = jnp.arange(p // tm, dtype=jnp.int32) * tm
    tile_expert = jnp.sum(tile_start[:, None] >= end_p[None, :], axis=1).astype(jnp.int32)
    tile_valid = (tile_expert < N_EXPERTS).astype(jnp.int32)
    tile_expert = jnp.minimum(tile_expert, N_EXPERTS - 1)
    return row_token, row_gate[:, None], tile_expert, tile_valid, pos


def kernel(x, c, ctx, c_ctx, ada_w, ada_b, ln_g, ln_b, mix_in_w, pool_w, pool_scale, q_norm, q_up, kv_norm, kv_up, mix_out_w, ffn_gate, ffn_up, ffn_down, hy_in_w, hy_conv_w, hy_conv_b, hy_fw1, hy_fb1, hy_fw2, hy_fb2, hy_fw3, hy_fb3, hy_fout, hy_freq, hy_skip, hy_out_w, router_w, moe_gate, moe_up, moe_down):
    b, n, d = x.shape
    t = b * n
    assert b + 1 <= 8
    s_rows = jnp.concatenate([c, c_ctx[None, :], jnp.zeros((8 - b - 1, d), F32)], axis=0)
    mod = _ada(s_rows, ada_w, ada_b)

    def chunks(l, rows):
        m = mod[l, rows].reshape(-1, 6, d)
        return [m[:, k][:, None, :] for k in range(6)]

    sh1, sc1, g1, sh2, sc2, g2 = chunks(0, slice(0, b))
    shc, scc = chunks(0, slice(b, b + 1))[:2]
    pool_u, q, k, v = _l0a(x, ctx, sc1, sh1, scc, shc, mix_in_w[0], q_norm[0], q_up[0], kv_norm[0], kv_up[0])
    attn_o = _attention(q, k, v, n)
    h1, z1 = _l0b(pool_u, attn_o, x, g1, sc2, sh2, pool_w[0], pool_scale[0], mix_out_w[0], ln_g[0, 0], ln_b[0, 0])
    f0 = _ffn_dense(z1.reshape(t, d), ffn_gate.astype(BF16), ffn_up.astype(BF16), ffn_down.astype(BF16), 1024, 1408)
    sh1, sc1, g1b, sh2b, sc2b, g2b = chunks(1, slice(0, b))
    h2, u2 = _resid_ln(h1, [f0.reshape(b, n, d)], g2, ln_g[0, 1], ln_b[0, 1], mod=(sc1, sh1))

    x0, vx = _l1a(u2, hy_in_w[0], hy_conv_w[0], hy_conv_b[0])
    kfull = _hyena_filters(n, hy_fw1[0], hy_fb1[0], hy_fw2[0], hy_fb2[0], hy_fw3[0], hy_fb3[0],
                           hy_fout[0], hy_freq[0])
    yl = _hyena_long_conv(x0, vx, kfull, hy_skip[0])
    h3, z3, logits = _l1c(yl, h2, g1b, sc2b, sh2b, hy_out_w[0], ln_g[1, 0], ln_b[1, 0], router_w[0])

    tm_moe = 512
    row_token, row_gate, tile_expert, tile_valid, pos = _route(logits.reshape(t, LANE)[:, :N_EXPERTS], tm_moe)
    ys = _moe_grouped(z3.reshape(t, d // LANE, LANE), row_token, _cast_bf16(moe_gate[0]), _cast_bf16(moe_up[0]),
                      _cast_bf16(moe_down[0]), row_gate, tile_expert, tile_valid, tm_moe, 1792)
    y_a = _take(ys, pos[:, 0]).reshape(b, n, d)
    y_b = _take(ys, pos[:, 1]).reshape(b, n, d)
    (h4,) = _resid_ln(h3, [y_a, y_b], g2b, ln_g[1, 1], ln_b[1, 1])
    return h4
```
